```python
import math
import jax, jax.numpy as jnp
from jax import lax
import numpy as np

D_MODEL = 1024
BATCH = 4
SEQ = 4096
DEPTH = 4

ROPE_THETA = 10000.0
QBLK = 128
MLA_HEADS = 8
MLA_NOPE = 64
MLA_ROPE = 32
MLA_V = 64
MLA_Q_RANK = 256
MLA_KV_RANK = 128
FOX_HEADS = 8
FOX_DIM = 64
SWA_HEADS = 16
SWA_KV_HEADS = 2
SWA_DIM = 64
WINDOW = 128

RMS_EPS = 1e-6
LN_EPS = 1e-5
ALPHA = (2 * DEPTH) ** 0.25
BETA = (8 * DEPTH) ** -0.25

EVEN_WIDTH = MLA_HEADS * MLA_V + FOX_HEADS * FOX_DIM
ODD_WIDTH = SWA_HEADS * SWA_DIM
EVEN_SIZES = (MLA_Q_RANK, MLA_KV_RANK, MLA_ROPE, FOX_HEADS * FOX_DIM,
              FOX_HEADS * FOX_DIM, FOX_HEADS * FOX_DIM, FOX_HEADS, EVEN_WIDTH)
EVEN_IN = sum(EVEN_SIZES)
EVEN_V_START = MLA_Q_RANK + MLA_KV_RANK + MLA_ROPE + 2 * FOX_HEADS * FOX_DIM
ODD_SIZES = (SWA_HEADS * SWA_DIM, SWA_KV_HEADS * SWA_DIM, SWA_KV_HEADS * SWA_DIM, ODD_WIDTH)
ODD_IN = sum(ODD_SIZES)
ODD_V_START = SWA_HEADS * SWA_DIM + SWA_KV_HEADS * SWA_DIM
N_EVEN = (DEPTH + 1) // 2
N_ODD = DEPTH // 2

kernel_name = "hybrid_mla_fox_swa_deepnorm"


def _split(h, sizes):
    cuts = [int(c) for c in np.cumsum(sizes)[:-1]]
    return jnp.split(h, cuts, axis=-1)


def _heads(t, n_heads):
    b, s, _ = t.shape
    return t.reshape(b, s, n_heads, -1).transpose(0, 2, 1, 3)


def _merge(t):
    b, h, s, d = t.shape
    return t.transpose(0, 2, 1, 3).reshape(b, s, h * d)


def rms_norm(t, g):
    tf = t.astype(jnp.float32)
    tf = tf * lax.rsqrt(jnp.mean(tf * tf, axis=-1, keepdims=True) + RMS_EPS)
    return (tf * g.astype(jnp.float32)).astype(t.dtype)


def layer_norm(t, g, b):
    tf = t.astype(jnp.float32)
    mu = jnp.mean(tf, axis=-1, keepdims=True)
    var = jnp.mean(jnp.square(tf - mu), axis=-1, keepdims=True)
    y = (tf - mu) * lax.rsqrt(var + LN_EPS) * g.astype(jnp.float32) + b.astype(jnp.float32)
    return y.astype(t.dtype)


def rope(t, pos):
    d = t.shape[-1]
    inv = ROPE_THETA ** (-jnp.arange(0, d, 2, dtype=jnp.float32) / d)
    ang = pos.astype(jnp.float32)[:, None] * inv[None, :]
    cos, sin = jnp.cos(ang), jnp.sin(ang)
    t1, t2 = jnp.split(t.astype(jnp.float32), 2, axis=-1)
    return jnp.concatenate([t1 * cos - t2 * sin, t2 * cos + t1 * sin], axis=-1).astype(t.dtype)


def causal_block_attention(q, k, v, scale, cum_logf=None):
    b, h, s, dk = q.shape
    dv = v.shape[-1]
    nb = s // QBLK
    qb = q.reshape(b, h, nb, QBLK, dk).transpose(2, 0, 1, 3, 4)
    kpos = jnp.arange(s)
    idx = jnp.arange(nb)

    def block(args):
        if cum_logf is None:
            q_i, i = args
        else:
            q_i, c_i, i = args
        sc = jnp.einsum('bhqd,bhkd->bhqk', q_i, k,
                        preferred_element_type=jnp.float32) * scale
        if cum_logf is not None:
            sc = sc + c_i[..., :, None] - cum_logf[..., None, :]
        qpos = i * QBLK + jnp.arange(QBLK)
        mask = kpos[None, :] <= qpos[:, None]
        sc = jnp.where(mask, sc, -jnp.inf)
        p = jax.nn.softmax(sc, axis=-1)
        return jnp.einsum('bhqk,bhkd->bhqd', p.astype(v.dtype), v)

    if cum_logf is None:
        xs = (qb, idx)
    else:
        cb = cum_logf.reshape(b, h, nb, QBLK).transpose(2, 0, 1, 3)
        xs = (qb, cb, idx)
    out = lax.map(block, xs)
    return out.transpose(1, 2, 0, 3, 4).reshape(b, h, s, dv)


def sliding_window_sink_attention(q, k, v, sinks):
    b, h, s, d = q.shape
    hkv = k.shape[1]
    g = h // hkv
    nb = s // QBLK
    scale = d ** -0.5
    qb = q.reshape(b, hkv, g, nb, QBLK, d).transpose(3, 0, 1, 2, 4, 5)
    pad = ((0, 0), (0, 0), (QBLK, 0), (0, 0))
    kp = jnp.pad(k, pad)
    vp = jnp.pad(v, pad)
    sink = sinks.astype(jnp.float32).reshape(1, hkv, g, 1, 1)

    def block(args):
        q_i, i = args
        start = i * QBLK
        k_i = lax.dynamic_slice_in_dim(kp, start, 2 * QBLK, axis=2)
        v_i = lax.dynamic_slice_in_dim(vp, start, 2 * QBLK, axis=2)
        sc = jnp.einsum('bkgqd,bkjd->bkgqj', q_i, k_i,
                        preferred_element_type=jnp.float32) * scale
        qpos = start + jnp.arange(QBLK)
        kpos = start - QBLK + jnp.arange(2 * QBLK)
        diff = qpos[:, None] - kpos[None, :]
        mask = (diff >= 0) & (diff < WINDOW) & (kpos[None, :] >= 0)
        sc = jnp.where(mask, sc, -jnp.inf)
        logits = jnp.concatenate(
            [sc, jnp.broadcast_to(sink, sc.shape[:-1] + (1,))], axis=-1)
        p = jax.nn.softmax(logits, axis=-1)[..., :-1]
        return jnp.einsum('bkgqj,bkjd->bkgqd', p.astype(v.dtype), v_i)

    out = lax.map(block, (qb, jnp.arange(nb)))
    return out.transpose(1, 2, 3, 0, 4, 5).reshape(b, h, s, d)


def even_mixer(x, w_in, q_norm, w_uq, kv_norm, w_ukv, b_f, w_out, pos):
    b, s, _ = x.shape
    h = x @ w_in
    cq, ckv, k_pe, fq, fk, fv, f_logit, gate = _split(h, EVEN_SIZES)
    q = (rms_norm(cq, q_norm) @ w_uq).reshape(b, s, MLA_HEADS, MLA_NOPE + MLA_ROPE)
    q = q.transpose(0, 2, 1, 3)
    q_nope, q_pe = q[..., :MLA_NOPE], rope(q[..., MLA_NOPE:], pos)
    kv = (rms_norm(ckv, kv_norm) @ w_ukv).reshape(b, s, MLA_HEADS, MLA_NOPE + MLA_V)
    kv = kv.transpose(0, 2, 1, 3)
    k_nope, v_mla = kv[..., :MLA_NOPE], kv[..., MLA_NOPE:]
    k_pe = rope(k_pe[:, None], pos)
    q_mla = jnp.concatenate([q_nope, q_pe], axis=-1)
    k_mla = jnp.concatenate(
        [k_nope, jnp.broadcast_to(k_pe, (b, MLA_HEADS, s, MLA_ROPE))], axis=-1)
    o_mla = causal_block_attention(q_mla, k_mla, v_mla, (MLA_NOPE + MLA_ROPE) ** -0.5)
    log_f = jax.nn.log_sigmoid((f_logit + b_f).astype(jnp.float32))
    cum = lax.cumsum(log_f, axis=1).transpose(0, 2, 1)
    o_fox = causal_block_attention(_heads(fq, FOX_HEADS), _heads(fk, FOX_HEADS),
                                   _heads(fv, FOX_HEADS), FOX_DIM ** -0.5, cum)
    o = jnp.concatenate([_merge(o_mla), _merge(o_fox)], axis=-1)
    return (o * jax.nn.silu(gate)) @ w_out


def odd_mixer(x, w_in, sinks, w_out, pos):
    h = x @ w_in
    q, k, v, gate = _split(h, ODD_SIZES)
    q = rope(_heads(q, SWA_HEADS), pos)
    k = rope(_heads(k, SWA_KV_HEADS), pos)
    v = _heads(v, SWA_KV_HEADS)
    o = _merge(sliding_window_sink_attention(q, k, v, sinks))
    return (o * jax.nn.silu(gate)) @ w_out


def setup_inputs(seed: int = 0) -> dict:
    key = jax.random.key(seed)
    ks = jax.random.split(key, 16)
    nrm = jax.random.normal
    even_in_scale = jnp.ones((EVEN_IN,), jnp.float32).at[
        EVEN_V_START:EVEN_V_START + FOX_HEADS * FOX_DIM].set(BETA)
    ukv_scale = jnp.tile(jnp.concatenate([jnp.ones((MLA_NOPE,), jnp.float32),
                                          jnp.full((MLA_V,), BETA, jnp.float32)]), MLA_HEADS)
    odd_in_scale = jnp.ones((ODD_IN,), jnp.float32).at[
        ODD_V_START:ODD_V_START + SWA_KV_HEADS * SWA_DIM].set(BETA)
    return {
        "x": nrm(ks[0], (BATCH, SEQ, D_MODEL), jnp.float32),
        "even_w_in": nrm(ks[1], (N_EVEN, D_MODEL, EVEN_IN), jnp.float32) * D_MODEL ** -0.5 * even_in_scale,
        "even_q_norm": 1.0 + 0.02 * nrm(ks[2], (N_EVEN, MLA_Q_RANK), jnp.float32),
        "even_w_uq": nrm(ks[3], (N_EVEN, MLA_Q_RANK, MLA_HEADS * (MLA_NOPE + MLA_ROPE)), jnp.float32) * MLA_Q_RANK ** -0.5,
        "even_kv_norm": 1.0 + 0.02 * nrm(ks[4], (N_EVEN, MLA_KV_RANK), jnp.float32),
        "even_w_ukv": nrm(ks[5], (N_EVEN, MLA_KV_RANK, MLA_HEADS * (MLA_NOPE + MLA_V)), jnp.float32) * MLA_KV_RANK ** -0.5 * ukv_scale,
        "even_b_f": jax.random.uniform(ks[6], (N_EVEN, FOX_HEADS), jnp.float32, 1.0, 6.0),
        "even_w_out": nrm(ks[7], (N_EVEN, EVEN_WIDTH, D_MODEL), jnp.float32) * EVEN_WIDTH ** -0.5 * BETA,
        "even_ln_g": 1.0 + 0.02 * nrm(ks[8], (N_EVEN, D_MODEL), jnp.float32),
        "even_ln_b": 0.02 * nrm(ks[9], (N_EVEN, D_MODEL), jnp.float32),
        "odd_w_in": nrm(ks[10], (N_ODD, D_MODEL, ODD_IN), jnp.float32) * D_MODEL ** -0.5 * odd_in_scale,
        "odd_sinks": 0.5 * nrm(ks[11], (N_ODD, SWA_HEADS), jnp.float32),
        "odd_w_out": nrm(ks[12], (N_ODD, ODD_WIDTH, D_MODEL), jnp.float32) * ODD_WIDTH ** -0.5 * BETA,
        "odd_ln_g": 1.0 + 0.02 * nrm(ks[13], (N_ODD, D_MODEL), jnp.float32),
        "odd_ln_b": 0.02 * nrm(ks[14], (N_ODD, D_MODEL), jnp.float32),
    }


def reference(x, even_w_in, even_q_norm, even_w_uq, even_kv_norm, even_w_ukv, even_b_f,
              even_w_out, even_ln_g, even_ln_b, odd_w_in, odd_sinks, odd_w_out,
              odd_ln_g, odd_ln_b):
    pos = jnp.arange(x.shape[1])
    for layer in range(DEPTH):
        j = layer // 2
        if layer % 2 == 0:
            y = even_mixer(x, even_w_in[j], even_q_norm[j], even_w_uq[j], even_kv_norm[j],
                           even_w_ukv[j], even_b_f[j], even_w_out[j], pos)
            x = layer_norm(ALPHA * x + y, even_ln_g[j], even_ln_b[j])
        else:
            y = odd_mixer(x, odd_w_in[j], odd_sinks[j], odd_w_out[j], pos)
            x = layer_norm(ALPHA * x + y, odd_ln_g[j], odd_ln_b[j])
    return x
```

```python
import functools
import math

import jax
import jax.numpy as jnp
import numpy as np
from jax import lax
from jax.experimental import pallas as pl
from jax.experimental.pallas import tpu as pltpu

D_MODEL = 1024
DEPTH = 4
ROPE_THETA = 10000.0
QBLK = 128
MLA_HEADS = 8
MLA_NOPE = 64
MLA_ROPE = 32
MLA_V = 64
MLA_Q_RANK = 256
MLA_KV_RANK = 128
FOX_HEADS = 8
FOX_DIM = 64
SWA_HEADS = 16
SWA_KV_HEADS = 2
SWA_DIM = 64
WINDOW = 128
RMS_EPS = 1e-6
LN_EPS = 1e-5
ALPHA = (2 * DEPTH) ** 0.25
LOG2E = math.log2(math.e)

LANES = 128
VMEM_LIMIT = 56 * 1024 * 1024
BF16 = jnp.bfloat16
F32 = jnp.float32

PROJ_TM = 256
ATT_T = 256
SWA_TQ = 512


def _dot(a, b):
    return jnp.dot(a, b, preferred_element_type=F32)


def _dot_nt(a, b):
    return lax.dot_general(a, b, (((1,), (1,)), ((), ())), preferred_element_type=F32)


def _rms(t, g):
    return t * lax.rsqrt(jnp.mean(t * t, axis=-1, keepdims=True) + RMS_EPS) * g


def _silu(g):
    return g / (1.0 + jnp.exp(-g))


def _params(n_axes):
    return pltpu.CompilerParams(dimension_semantics=("arbitrary",) * n_axes,
                                vmem_limit_bytes=VMEM_LIMIT)


def _even_proj_kernel(x_ref, wlat_ref, wf_ref, wfox_ref, wg_ref, wuq_ref, wukv_ref,
                      qn_ref, kvn_ref, bf_ref, cq_ref, sq_ref, ck_ref, sk_ref,
                      qm_ref, km_ref, vm_ref, fq_ref, fk_ref, fv_ref, cs_ref, sg_ref,
                      carry_ref):
    tm = x_ref.shape[0]
    hw = MLA_HEADS * LANES
    xb = x_ref[...].astype(BF16)

    lane = lax.broadcasted_iota(jnp.int32, (1, hw), 1) % (2 * LANES)
    ones_row = jnp.where((lane == MLA_V) | (lane == LANES), 1.0, 0.0).astype(F32)

    lat = _dot(xb, wlat_ref[...])
    cq = lat[:, :MLA_Q_RANK]
    ckv = lat[:, MLA_Q_RANK:MLA_Q_RANK + MLA_KV_RANK]
    o = MLA_Q_RANK + MLA_KV_RANK
    kpe = lat[:, o:o + LANES] * ck_ref[...] + lat[:, o + LANES:o + 2 * LANES] * sk_ref[...]

    rq = _rms(cq, qn_ref[...]).astype(BF16)
    qa = _dot(rq, wuq_ref[:, :hw])
    qb = _dot(rq, wuq_ref[:, hw:])
    cqt = cq_ref[...]
    sqt = sq_ref[...]
    for h in range(MLA_HEADS):
        hs = slice(h * LANES, (h + 1) * LANES)
        qm_ref[:, hs] = (qa[:, hs] * cqt + qb[:, hs] * sqt).astype(BF16)

    rkv = _rms(ckv, kvn_ref[...]).astype(BF16)
    kk = _dot(rkv, wukv_ref[:, :hw])
    for h in range(MLA_HEADS):
        hs = slice(h * LANES, (h + 1) * LANES)
        km_ref[:, hs] = (kk[:, hs] + kpe).astype(BF16)
    vm_ref[...] = (_dot(rkv, wukv_ref[:, hw:]) + ones_row).astype(BF16)

    fq_ref[...] = (_dot(xb, wfox_ref[:, :hw]) * (FOX_DIM ** -0.5 * LOG2E)).astype(BF16)
    fk_ref[...] = _dot(xb, wfox_ref[:, hw:2 * hw]).astype(BF16)
    fv_ref[...] = (_dot(xb, wfox_ref[:, 2 * hw:]) + ones_row).astype(BF16)

    sg_ref[...] = _silu(_dot(xb, wg_ref[...])).astype(BF16)

    fl = _dot_nt(wf_ref[...], xb) + bf_ref[...]
    lf = (jnp.minimum(fl, 0.0) - jnp.log1p(jnp.exp(-jnp.abs(fl)))) * LOG2E
    hi = lf.astype(BF16)
    r1 = lf - hi.astype(F32)
    mid = r1.astype(BF16)
    lo = (r1 - mid.astype(F32)).astype(BF16)
    row = lax.broadcasted_iota(jnp.int32, (tm, tm), 0)
    col = lax.broadcasted_iota(jnp.int32, (tm, tm), 1)
    tri = jnp.where(row <= col, 1.0, 0.0).astype(BF16)
    c_loc = _dot(hi, tri) + _dot(mid, tri) + _dot(lo, tri)

    @pl.when(pl.program_id(1) == 0)
    def _():
        carry_ref[...] = jnp.zeros_like(carry_ref)

    c = c_loc + carry_ref[:, :1]
    cs_ref[...] = c
    carry_ref[...] = jnp.broadcast_to(c[:, tm - 1:tm], carry_ref.shape)


def _even_proj(x, w, tabs):
    b, s, d = x.shape
    tm = PROJ_TM
    hw = MLA_HEADS * LANES
    const = lambda bi, si: (0, 0)
    tok = lambda bi, si: (bi, si, 0)
    tab = lambda bi, si: (si, 0)
    wspec = lambda a: pl.BlockSpec(a.shape, const)
    act = jax.ShapeDtypeStruct((b, s, hw), BF16)
    in_specs = [pl.BlockSpec((None, tm, d), tok)]
    in_specs += [wspec(w[k]) for k in ("wlat", "wf", "wfox", "wg", "wuq", "wukv", "qn", "kvn", "bf")]
    in_specs += [pl.BlockSpec((tm, LANES), tab)] * 4
    out_specs = [pl.BlockSpec((None, tm, hw), tok)] * 6
    out_specs += [pl.BlockSpec((None, FOX_HEADS, tm), lambda bi, si: (bi, 0, si)),
                  pl.BlockSpec((None, tm, hw), tok)]
    out_shape = [act] * 6 + [jax.ShapeDtypeStruct((b, FOX_HEADS, s), F32), act]
    return pl.pallas_call(
        _even_proj_kernel,
        grid=(b, s // tm),
        in_specs=in_specs,
        out_specs=out_specs,
        out_shape=out_shape,
        scratch_shapes=[pltpu.VMEM((FOX_HEADS, LANES), F32)],
        compiler_params=_params(2),
        name="even_proj",
    )(x, w["wlat"], w["wf"], w["wfox"], w["wg"], w["wuq"], w["wukv"], w["qn"], w["kvn"], w["bf"],
      tabs["cq"], tabs["sq"], tabs["ck"], tabs["sk"])


def _causal_attn_kernel(*refs, fox):
    if fox:
        q_ref, k_ref, v_ref, cs_ref, sg_ref, o_ref = refs
    else:
        q_ref, k_ref, v_ref, sg_ref, o_ref = refs
        cs_ref = None
    t = q_ref.shape[0]
    hp = pl.program_id(1)
    qi = pl.program_id(2)
    row = lax.broadcasted_iota(jnp.int32, (t, t), 0)
    col = lax.broadcasted_iota(jnp.int32, (t, t), 1)
    causal = col <= row

    accs = []
    for hh in range(2):
        hs = slice(hh * LANES, (hh + 1) * LANES)
        q = q_ref[:, hs]

        def tile(j, carry, masked, hs=hs, hh=hh, q=q):
            m, acc = carry
            ks = pl.ds(pl.multiple_of(j * t, t), t)
            s = _dot_nt(q, k_ref[ks, hs])
            if fox:
                s = s - cs_ref[pl.ds(2 * hp + hh, 1), ks]
            if masked:
                s = jnp.where(causal, s, -jnp.inf)
            m_new = jnp.maximum(m, jnp.max(s, axis=1, keepdims=True))
            p = jnp.exp2(s - m_new)
            acc = jnp.exp2(m - m_new) * acc + _dot(p.astype(BF16), v_ref[ks, hs])
            return m_new, acc

        init = (jnp.full((t, 1), -jnp.inf, F32), jnp.zeros((t, LANES), F32))
        carry = lax.fori_loop(0, qi, functools.partial(tile, masked=False), init)
        _, acc = tile(qi, carry, True)
        accs.append(acc)

    o0 = accs[0] / accs[0][:, MLA_V:MLA_V + 1]
    o1 = accs[1] / accs[1][:, 0:1]
    lane = lax.broadcasted_iota(jnp.int32, (t, LANES), 1)
    o = jnp.where(lane < MLA_V, o0, o1)
    o_ref[...] = (o * sg_ref[...].astype(F32)).astype(BF16)


def _causal_attn(q, k, v, sg, cs, sg_col0):
    b, s, hw = q.shape
    t = ATT_T
    n_pairs = hw // (2 * LANES)
    fox = cs is not None
    qspec = pl.BlockSpec((None, t, 2 * LANES), lambda bi, hp, qi: (bi, qi, hp))
    kvspec = pl.BlockSpec((None, s, 2 * LANES), lambda bi, hp, qi: (bi, 0, hp))
    sgspec = pl.BlockSpec((None, t, LANES), lambda bi, hp, qi: (bi, qi, hp + sg_col0))
    ospec = pl.BlockSpec((None, t, LANES), lambda bi, hp, qi: (bi, qi, hp))
    in_specs = [qspec, kvspec, kvspec]
    args = [q, k, v]
    if fox:
        in_specs.append(pl.BlockSpec((None, FOX_HEADS, s), lambda bi, hp, qi: (bi, 0, 0)))
        args.append(cs)
    in_specs.append(sgspec)
    args.append(sg)
    return pl.pallas_call(
        functools.partial(_causal_attn_kernel, fox=fox),
        grid=(b, n_pairs, s // t),
        in_specs=in_specs,
        out_specs=ospec,
        out_shape=jax.ShapeDtypeStruct((b, s, n_pairs * LANES), BF16),
        compiler_params=_params(3),
        name="fox_attn" if fox else "mla_attn",
    )(*args)


def _out_kernel(*refs, n_pieces):
    o_refs = refs[:n_pieces]
    x_ref, w_ref, g_ref, b_ref, y_ref = refs[n_pieces:]
    y = None
    r0 = 0
    for o_ref in o_refs:
        kw = o_ref.shape[1]
        part = _dot(o_ref[...], w_ref[r0:r0 + kw, :])
        y = part if y is None else y + part
        r0 += kw
    z = ALPHA * x_ref[...] + y
    mu = jnp.mean(z, axis=-1, keepdims=True)
    zc = z - mu
    var = jnp.mean(zc * zc, axis=-1, keepdims=True)
    y_ref[...] = zc * lax.rsqrt(var + LN_EPS) * g_ref[...] + b_ref[...]


def _out_proj(pieces, x, w_out, g, bias):
    b, s, d = x.shape
    tm = PROJ_TM
    tok = lambda bi, si: (bi, si, 0)
    const = lambda bi, si: (0, 0)
    in_specs = [pl.BlockSpec((None, tm, p.shape[2]), tok) for p in pieces]
    in_specs += [pl.BlockSpec((None, tm, d), tok), pl.BlockSpec(w_out.shape, const),
                 pl.BlockSpec((1, d), const), pl.BlockSpec((1, d), const)]
    return pl.pallas_call(
        functools.partial(_out_kernel, n_pieces=len(pieces)),
        grid=(b, s // tm),
        in_specs=in_specs,
        out_specs=pl.BlockSpec((None, tm, d), tok),
        out_shape=jax.ShapeDtypeStruct((b, s, d), F32),
        compiler_params=_params(2),
        name="out_proj_ln",
    )(*pieces, x, w_out, g, bias)


def _odd_proj_kernel(x_ref, wq_ref, wkv_ref, wg_ref, c_ref, s_ref,
                     q_ref, k_ref, v_ref, sg_ref):
    d = SWA_HEADS * SWA_DIM
    xb = x_ref[...].astype(BF16)
    ct = c_ref[...]
    st = s_ref[...]
    qq = _dot(xb, wq_ref[...])
    scale = SWA_DIM ** -0.5 * LOG2E
    for p in range(d // LANES):
        ps = slice(p * LANES, (p + 1) * LANES)
        sw = slice(d + p * LANES, d + (p + 1) * LANES)
        q_ref[:, ps] = ((qq[:, ps] * ct + qq[:, sw] * st) * scale).astype(BF16)
    kv = _dot(xb, wkv_ref[...])
    k_ref[...] = (kv[:, :LANES] * ct + kv[:, LANES:2 * LANES] * st).astype(BF16)
    v_ref[...] = kv[:, 2 * LANES:].astype(BF16)
    sg_ref[...] = _silu(_dot(xb, wg_ref[...])).astype(BF16)


def _odd_proj(x, w, tabs):
    b, s, d = x.shape
    tm = PROJ_TM
    const = lambda bi, si: (0, 0)
    tok = lambda bi, si: (bi, si, 0)
    tab = lambda bi, si: (si, 0)
    wspec = lambda a: pl.BlockSpec(a.shape, const)
    wide = jax.ShapeDtypeStruct((b, s, d), BF16)
    narrow = jax.ShapeDtypeStruct((b, s, LANES), BF16)
    return pl.pallas_call(
        _odd_proj_kernel,
        grid=(b, s // tm),
        in_specs=[pl.BlockSpec((None, tm, d), tok), wspec(w["wq"]), wspec(w["wkv"]), wspec(w["wg"]),
                  pl.BlockSpec((tm, LANES), tab), pl.BlockSpec((tm, LANES), tab)],
        out_specs=[pl.BlockSpec((None, tm, d), tok), pl.BlockSpec((None, tm, LANES), tok),
                   pl.BlockSpec((None, tm, LANES), tok), pl.BlockSpec((None, tm, d), tok)],
        out_shape=[wide, narrow, narrow, wide],
        compiler_params=_params(2),
        name="odd_proj",
    )(x, w["wq"], w["wkv"], w["wg"], tabs["c"], tabs["s"])


def _swa_kernel(sink_ref, q_ref, k_ref, v_ref, sg_ref, o_ref):
    n_blk = q_ref.shape[0] // QBLK
    n_pair = q_ref.shape[1] // LANES
    qi = pl.program_id(1)
    lane = lax.broadcasted_iota(jnp.int32, (QBLK, LANES), 1)
    lo_half = lane < SWA_DIM
    r = lax.broadcasted_iota(jnp.int32, (QBLK, 2 * QBLK), 0)
    c = lax.broadcasted_iota(jnp.int32, (QBLK, 2 * QBLK), 1)

    def block(blk, _):
        gb = qi * n_blk + blk
        q0 = pl.multiple_of(blk * QBLK, QBLK)
        k0 = pl.multiple_of(jnp.maximum(gb - 1, 0) * QBLK, QBLK)
        diff = (gb * QBLK + r) - (k0 + c)
        band = (diff >= 0) & (diff < WINDOW)
        kw = k_ref[pl.ds(k0, 2 * QBLK), :]
        vw = v_ref[pl.ds(k0, 2 * QBLK), :]
        qrows = []
        for p in range(n_pair):
            qp = q_ref[pl.ds(q0, QBLK), p * LANES:(p + 1) * LANES]
            zero = jnp.zeros_like(qp)
            qrows.append(jnp.where(lo_half, qp, zero))
            qrows.append(jnp.where(lo_half, zero, qp))
        s_all = _dot_nt(jnp.concatenate(qrows, axis=0), kw)
        ps = []
        inv = []
        for rb in range(2 * n_pair):
            sink = sink_ref[rb] * LOG2E
            s = jnp.where(band, s_all[rb * QBLK:(rb + 1) * QBLK, :], -jnp.inf)
            m = jnp.maximum(jnp.max(s, axis=1, keepdims=True), sink)
            p_ = jnp.exp2(s - m)
            inv.append(1.0 / (jnp.sum(p_, axis=1, keepdims=True) + jnp.exp2(sink - m)))
            ps.append(p_.astype(BF16))
        o_all = _dot(jnp.concatenate(ps, axis=0), vw)
        for p in range(n_pair):
            oa = o_all[(2 * p) * QBLK:(2 * p + 1) * QBLK, :] * inv[2 * p]
            ob = o_all[(2 * p + 1) * QBLK:(2 * p + 2) * QBLK, :] * inv[2 * p + 1]
            o = jnp.where(lo_half, oa, ob)
            cols = slice(p * LANES, (p + 1) * LANES)
            o_ref[pl.ds(q0, QBLK), cols] = (
                o * sg_ref[pl.ds(q0, QBLK), cols].astype(F32)).astype(BF16)
        return 0

    lax.fori_loop(0, n_blk, block, 0)


def _swa_attn(sinks, q, k, v, sg):
    b, s, d = q.shape
    tq = SWA_TQ
    tok = lambda bi, qi: (bi, qi, 0)
    full = lambda bi, qi: (bi, 0, 0)
    return pl.pallas_call(
        _swa_kernel,
        grid=(b, s // tq),
        in_specs=[pl.BlockSpec(memory_space=pltpu.SMEM),
                  pl.BlockSpec((None, tq, d), tok),
                  pl.BlockSpec((None, s, LANES), full), pl.BlockSpec((None, s, LANES), full),
                  pl.BlockSpec((None, tq, d), tok)],
        out_specs=pl.BlockSpec((None, tq, d), tok),
        out_shape=jax.ShapeDtypeStruct((b, s, d), BF16),
        compiler_params=_params(2),
        name="swa_attn",
    )(sinks, q, k, v, sg)


def _pad_heads(w, n_heads, dim, offsets):
    k = w.shape[0]
    w3 = w.reshape(k, n_heads, dim)
    out = jnp.zeros((k, n_heads, LANES), w.dtype)
    for par, off in enumerate(offsets):
        out = out.at[:, par::len(offsets), off:off + dim].set(w3[:, par::len(offsets), :])
    return out.reshape(k, n_heads * LANES)


def _swap_halves(w, n_heads, dim):
    k = w.shape[0]
    w3 = w.reshape(k, n_heads, dim)
    return jnp.concatenate([w3[..., dim // 2:], w3[..., :dim // 2]], axis=-1).reshape(k, n_heads * dim)


def _even_weights(w_in, q_norm, w_uq, kv_norm, w_ukv, b_f):
    sizes = (MLA_Q_RANK, MLA_KV_RANK, MLA_ROPE, FOX_HEADS * FOX_DIM, FOX_HEADS * FOX_DIM,
             FOX_HEADS * FOX_DIM, FOX_HEADS, MLA_HEADS * MLA_V + FOX_HEADS * FOX_DIM)
    cuts = [int(c) for c in np.cumsum(sizes)[:-1]]
    w_cq, w_ckv, w_kpe, w_fq, w_fk, w_fv, w_f, w_g = jnp.split(w_in, cuts, axis=1)
    place = lambda w: _pad_heads(w, 1, MLA_ROPE, (MLA_NOPE,))
    wlat = jnp.concatenate([w_cq, w_ckv, place(w_kpe), place(_swap_halves(w_kpe, 1, MLA_ROPE))], axis=1)
    v_off = (0, MLA_V)
    wfox = jnp.concatenate([_pad_heads(w_fq, FOX_HEADS, FOX_DIM, (0,)),
                            _pad_heads(w_fk, FOX_HEADS, FOX_DIM, (0,)),
                            _pad_heads(w_fv, FOX_HEADS, FOX_DIM, v_off)], axis=1)
    dq = MLA_NOPE + MLA_ROPE
    uq3 = w_uq.reshape(MLA_Q_RANK, MLA_HEADS, dq)
    uq_pe = uq3[..., MLA_NOPE:].reshape(MLA_Q_RANK, MLA_HEADS * MLA_ROPE)
    uq_a = _pad_heads(w_uq, MLA_HEADS, dq, (0,))
    uq_b = _pad_heads(_swap_halves(uq_pe, MLA_HEADS, MLA_ROPE), MLA_HEADS, MLA_ROPE, (MLA_NOPE,))
    ukv3 = w_ukv.reshape(MLA_KV_RANK, MLA_HEADS, MLA_NOPE + MLA_V)
    uk = ukv3[..., :MLA_NOPE].reshape(MLA_KV_RANK, MLA_HEADS * MLA_NOPE)
    uv = ukv3[..., MLA_NOPE:].reshape(MLA_KV_RANK, MLA_HEADS * MLA_V)
    wukv = jnp.concatenate([_pad_heads(uk, MLA_HEADS, MLA_NOPE, (0,)),
                            _pad_heads(uv, MLA_HEADS, MLA_V, v_off)], axis=1)
    return dict(wlat=wlat.astype(BF16), wf=w_f.T.astype(BF16), wfox=wfox.astype(BF16),
                wg=w_g.astype(BF16), wuq=jnp.concatenate([uq_a, uq_b], axis=1).astype(BF16),
                wukv=wukv.astype(BF16), qn=q_norm.reshape(1, -1), kvn=kv_norm.reshape(1, -1),
                bf=b_f.reshape(-1, 1))


def _pair_heads(w):
    lead = w.shape[:-1]
    w4 = w.reshape(*lead, SWA_KV_HEADS, SWA_HEADS // SWA_KV_HEADS, -1)
    return jnp.swapaxes(w4, -3, -2).reshape(*lead, -1)


def _odd_weights(w_in, sinks, w_out):
    d = SWA_HEADS * SWA_DIM
    dkv = SWA_KV_HEADS * SWA_DIM
    w_q, w_k, w_v, w_g = jnp.split(w_in, [d, d + dkv, d + 2 * dkv], axis=1)
    wq = jnp.concatenate([_pair_heads(w_q), _pair_heads(_swap_halves(w_q, SWA_HEADS, SWA_DIM))], axis=1)
    wkv = jnp.concatenate([w_k, _swap_halves(w_k, SWA_KV_HEADS, SWA_DIM), w_v], axis=1)
    return dict(wq=wq.astype(BF16), wkv=wkv.astype(BF16), wg=_pair_heads(w_g).astype(BF16),
                sinks=_pair_heads(sinks.reshape(-1, 1).T).reshape(-1),
                w_out=_pair_heads(w_out.T).T.astype(BF16))


def _rope_tables(s):
    pos = jnp.arange(s).astype(F32)

    def cs(dim):
        inv = ROPE_THETA ** (-jnp.arange(0, dim, 2, dtype=F32) / dim)
        ang = pos[:, None] * inv[None, :]
        cos, sin = jnp.cos(ang), jnp.sin(ang)
        return jnp.concatenate([cos, cos], axis=1), jnp.concatenate([-sin, sin], axis=1)

    c32, s32 = cs(MLA_ROPE)
    z = lambda n: jnp.zeros((s, n), F32)
    pad = LANES - MLA_NOPE - MLA_ROPE
    scale = (MLA_NOPE + MLA_ROPE) ** -0.5 * LOG2E
    even = dict(cq=jnp.concatenate([jnp.ones((s, MLA_NOPE), F32), c32, z(pad)], axis=1) * scale,
                sq=jnp.concatenate([z(MLA_NOPE), s32, z(pad)], axis=1) * scale,
                ck=jnp.concatenate([z(MLA_NOPE), c32, z(pad)], axis=1),
                sk=jnp.concatenate([z(MLA_NOPE), s32, z(pad)], axis=1))
    c64, s64 = cs(SWA_DIM)
    odd = dict(c=jnp.concatenate([c64, c64], axis=1), s=jnp.concatenate([s64, s64], axis=1))
    return even, odd


def kernel(x, even_w_in, even_q_norm, even_w_uq, even_kv_norm, even_w_ukv, even_b_f, even_w_out,
           even_ln_g, even_ln_b, odd_w_in, odd_sinks, odd_w_out, odd_ln_g, odd_ln_b):
    s = x.shape[1]
    even_tabs, odd_tabs = _rope_tables(s)
    n_mla = MLA_HEADS // 2
    for layer in range(DEPTH):
        j = layer // 2
        if layer % 2 == 0:
            w = _even_weights(even_w_in[j], even_q_norm[j], even_w_uq[j], even_kv_norm[j],
                              even_w_ukv[j], even_b_f[j])
            qm, km, vm, fq, fk, fv, cs, sg = _even_proj(x, w, even_tabs)
            o_mla = _causal_attn(qm, km, vm, sg, None, 0)
            o_fox = _causal_attn(fq, fk, fv, sg, cs, n_mla)
            x = _out_proj([o_mla, o_fox], x, even_w_out[j].astype(BF16),
                          even_ln_g[j].reshape(1, -1), even_ln_b[j].reshape(1, -1))
        else:
            w = _odd_weights(odd_w_in[j], odd_sinks[j], odd_w_out[j])
            q, k, v, sg = _odd_proj(x, w, odd_tabs)
            o = _swa_attn(w["sinks"], q, k, v, sg)
            x = _out_proj([o], x, w["w_out"], odd_ln_g[j].reshape(1, -1), odd_ln_b[j].reshape(1, -1))
    return x
```

```python
import functools
import math

import jax
import jax.numpy as jnp
import numpy as np
from jax import lax
from jax.experimental import pallas as pl
from jax.experimental.pallas import tpu as pltpu

D_MODEL = 1024
DEPTH = 4
ROPE_THETA = 10000.0
QBLK = 128
MLA_HEADS = 8
MLA_NOPE = 64
MLA_ROPE = 32
MLA_V = 64
MLA_Q_RANK = 256
MLA_KV_RANK = 128
FOX_HEADS = 8
FOX_DIM = 64
SWA_HEADS = 16
SWA_KV_HEADS = 2
SWA_DIM = 64
WINDOW = 128
RMS_EPS = 1e-6
LN_EPS = 1e-5
ALPHA = (2 * DEPTH) ** 0.25
LOG2E = math.log2(math.e)

LANES = 128
VMEM_LIMIT = 56 * 1024 * 1024
BF16 = jnp.bfloat16
F32 = jnp.float32

PROJ_TM = 256
ATT_T = 512
ATT_HG = 4
SWA_TQ = 512
EVEN_HEADS = MLA_HEADS + FOX_HEADS
EVEN_HW = EVEN_HEADS * LANES
MLA_HW = MLA_HEADS * LANES
C_LANE = FOX_DIM


def _dot(a, b):
    return jnp.dot(a, b, preferred_element_type=F32)


def _dot_nt(a, b):
    return lax.dot_general(a, b, (((1,), (1,)), ((), ())), preferred_element_type=F32)


def _rms(t, g):
    return t * lax.rsqrt(jnp.mean(t * t, axis=-1, keepdims=True) + RMS_EPS) * g


def _silu(g):
    return g / (1.0 + jnp.exp(-g))


def _split3(v):
    hi = v.astype(BF16)
    r1 = v - hi.astype(F32)
    mid = r1.astype(BF16)
    lo = (r1 - mid.astype(F32)).astype(BF16)
    return hi, mid, lo


def _params(n_axes):
    return pltpu.CompilerParams(dimension_semantics=("arbitrary",) * n_axes,
                                vmem_limit_bytes=VMEM_LIMIT)


def _even_proj_kernel(x_ref, wlat_ref, wuqt_ref, wuk_ref, wuvt_ref, wfqt_ref, wfk_ref, wfvt_ref,
                      wg_ref, qn_ref, kvn_ref, bf_ref, cqt_ref, sqt_ref, ck_ref, sk_ref,
                      qt_ref, k_ref, vt_ref, sg_ref, carry_ref):
    tm = x_ref.shape[0]
    xb = x_ref[...].astype(BF16)

    hrow = lax.broadcasted_iota(jnp.int32, (MLA_HW, 1), 0) % (2 * LANES)
    ones_col = jnp.where((hrow == MLA_V) | (hrow == LANES), 1.0, 0.0).astype(F32)
    frow = lax.broadcasted_iota(jnp.int32, (MLA_HW, 1), 0) % LANES
    neg_col = jnp.where((frow >= C_LANE) & (frow < C_LANE + 3), -1.0, 0.0).astype(F32)

    lat = _dot(xb, wlat_ref[...])
    o = 0
    cq = lat[:, o:o + MLA_Q_RANK]
    o += MLA_Q_RANK
    ckv = lat[:, o:o + MLA_KV_RANK]
    o += MLA_KV_RANK
    kpe = lat[:, o:o + LANES] * ck_ref[...] + lat[:, o + LANES:o + 2 * LANES] * sk_ref[...]
    o += 2 * LANES
    fl = lat[:, o:o + LANES] + bf_ref[...]

    rq = _rms(cq, qn_ref[...]).astype(BF16)
    qa = _dot_nt(wuqt_ref[:MLA_HW, :], rq)
    qb = _dot_nt(wuqt_ref[MLA_HW:, :], rq)
    cqt = cqt_ref[...]
    sqt = sqt_ref[...]
    for h in range(MLA_HEADS):
        hs = slice(h * LANES, (h + 1) * LANES)
        qt_ref[hs, :] = (qa[hs, :] * cqt + qb[hs, :] * sqt).astype(BF16)
    rkv = _rms(ckv, kvn_ref[...]).astype(BF16)
    kk = _dot(rkv, wuk_ref[...])
    for h in range(MLA_HEADS):
        hs = slice(h * LANES, (h + 1) * LANES)
        k_ref[:, hs] = (kk[:, hs] + kpe).astype(BF16)
    vt_ref[:MLA_HW, :] = (_dot_nt(wuvt_ref[...], rkv) + ones_col).astype(BF16)

    lane = lax.broadcasted_iota(jnp.int32, (1, LANES), 1)
    lf = (jnp.minimum(fl, 0.0) - jnp.log1p(jnp.exp(-jnp.abs(fl)))) * LOG2E
    lf = jnp.where(lane < FOX_HEADS, lf, 0.0)
    row = lax.broadcasted_iota(jnp.int32, (tm, tm), 0)
    col = lax.broadcasted_iota(jnp.int32, (tm, tm), 1)
    tri = jnp.where(col <= row, 1.0, 0.0).astype(BF16)
    hi, mid, lo = _split3(lf)
    c_loc = _dot(tri, hi) + _dot(tri, mid) + _dot(tri, lo)

    @pl.when(pl.program_id(1) == 0)
    def _():
        carry_ref[...] = jnp.zeros_like(carry_ref)

    c = c_loc + carry_ref[0:1, :]
    carry_ref[...] = jnp.broadcast_to(c[tm - 1:tm, :], carry_ref.shape)
    src = lax.broadcasted_iota(jnp.int32, (LANES, MLA_HW), 0)
    dst = lax.broadcasted_iota(jnp.int32, (LANES, MLA_HW), 1)
    k_extra = None
    for t, part in enumerate(_split3(c)):
        place = jnp.where((dst == src * LANES + C_LANE + t) & (src < FOX_HEADS), 1.0, 0.0).astype(BF16)
        term = _dot(part, place)
        k_extra = term if k_extra is None else k_extra + term

    qt_ref[MLA_HW:, :] = (_dot_nt(wfqt_ref[...], xb) * (FOX_DIM ** -0.5 * LOG2E) + neg_col).astype(BF16)
    k_ref[:, MLA_HW:] = (_dot(xb, wfk_ref[...]) + k_extra).astype(BF16)
    vt_ref[MLA_HW:, :] = (_dot_nt(wfvt_ref[...], xb) + ones_col).astype(BF16)

    sg_ref[...] = _silu(_dot(xb, wg_ref[...])).astype(BF16)


def _even_proj(x, w, tabs):
    b, s, d = x.shape
    tm = PROJ_TM
    const = lambda bi, si: (0, 0)
    tok = lambda bi, si: (bi, si, 0)
    tok_t = lambda bi, si: (bi, 0, si)
    wspec = lambda a: pl.BlockSpec(a.shape, const)
    names = ("wlat", "wuqt", "wuk", "wuvt", "wfqt", "wfk", "wfvt", "wg", "qn", "kvn", "bf")
    in_specs = [pl.BlockSpec((None, tm, d), tok)] + [wspec(w[k]) for k in names]
    in_specs += [pl.BlockSpec((LANES, tm), lambda bi, si: (0, si))] * 2
    in_specs += [pl.BlockSpec((tm, LANES), lambda bi, si: (si, 0))] * 2
    out_specs = [pl.BlockSpec((None, EVEN_HW, tm), tok_t), pl.BlockSpec((None, tm, EVEN_HW), tok),
                 pl.BlockSpec((None, EVEN_HW, tm), tok_t), pl.BlockSpec((None, tm, d), tok)]
    out_shape = [jax.ShapeDtypeStruct((b, EVEN_HW, s), BF16), jax.ShapeDtypeStruct((b, s, EVEN_HW), BF16),
                 jax.ShapeDtypeStruct((b, EVEN_HW, s), BF16), jax.ShapeDtypeStruct((b, s, d), BF16)]
    return pl.pallas_call(
        _even_proj_kernel,
        grid=(b, s // tm),
        in_specs=in_specs,
        out_specs=out_specs,
        out_shape=out_shape,
        scratch_shapes=[pltpu.VMEM((8, LANES), F32)],
        compiler_params=_params(2),
        name="even_proj",
    )(x, *[w[k] for k in names], tabs["cqt"], tabs["sqt"], tabs["ck"], tabs["sk"])


def _causal_attn_kernel(qt_ref, k_ref, vt_ref, sg_ref, o_ref):
    t = qt_ref.shape[1]
    n_heads = qt_ref.shape[0] // LANES
    qi = pl.program_id(2)
    key = lax.broadcasted_iota(jnp.int32, (t, t), 0)
    qry = lax.broadcasted_iota(jnp.int32, (t, t), 1)
    causal = key <= qry

    def tile(j, carry, masked):
        ks = pl.ds(pl.multiple_of(j * t, t), t)
        out = []
        for h in range(n_heads):
            hs = slice(h * LANES, (h + 1) * LANES)
            m, acc = carry[h]
            s = _dot(k_ref[ks, hs], qt_ref[hs, :])
            if masked:
                s = jnp.where(causal, s, -jnp.inf)
            m_new = jnp.maximum(m, jnp.max(s, axis=0, keepdims=True))
            p = jnp.exp2(s - m_new).astype(BF16)
            acc = jnp.exp2(m - m_new) * acc + _dot(vt_ref[hs, ks], p)
            out.append((m_new, acc))
        return tuple(out)

    init = tuple((jnp.full((1, t), -jnp.inf, F32), jnp.zeros((LANES, t), F32)) for _ in range(n_heads))
    carry = lax.fori_loop(0, qi, functools.partial(tile, masked=False), init)
    carry = tile(qi, carry, True)

    lane = lax.broadcasted_iota(jnp.int32, (t, LANES), 1)
    for hp in range(n_heads // 2):
        a0 = carry[2 * hp][1]
        a1 = carry[2 * hp + 1][1]
        o0 = (a0 / a0[MLA_V:MLA_V + 1, :]).T
        o1 = (a1 / a1[0:1, :]).T
        o = jnp.where(lane < MLA_V, o0, o1)
        cols = slice(hp * LANES, (hp + 1) * LANES)
        o_ref[:, cols] = (o * sg_ref[:, cols].astype(F32)).astype(BF16)


def _causal_attn(qt, k, vt, sg):
    b, s, hw = k.shape
    t = ATT_T
    gw = ATT_HG * LANES
    ow = ATT_HG * MLA_V
    return pl.pallas_call(
        _causal_attn_kernel,
        grid=(b, hw // gw, s // t),
        in_specs=[pl.BlockSpec((None, gw, t), lambda bi, g, qi: (bi, g, qi)),
                  pl.BlockSpec((None, s, gw), lambda bi, g, qi: (bi, 0, g)),
                  pl.BlockSpec((None, gw, s), lambda bi, g, qi: (bi, g, 0)),
                  pl.BlockSpec((None, t, ow), lambda bi, g, qi: (bi, qi, g))],
        out_specs=pl.BlockSpec((None, t, ow), lambda bi, g, qi: (bi, qi, g)),
        out_shape=jax.ShapeDtypeStruct((b, s, (hw // LANES) * MLA_V), BF16),
        compiler_params=_params(3),
        name="causal_attn",
    )(qt, k, vt, sg)


def _out_kernel(o_ref, x_ref, w_ref, g_ref, b_ref, y_ref):
    z = ALPHA * x_ref[...] + _dot(o_ref[...], w_ref[...])
    mu = jnp.mean(z, axis=-1, keepdims=True)
    zc = z - mu
    var = jnp.mean(zc * zc, axis=-1, keepdims=True)
    y_ref[...] = zc * lax.rsqrt(var + LN_EPS) * g_ref[...] + b_ref[...]


def _out_proj(o, x, w_out, g, bias):
    b, s, d = x.shape
    tm = PROJ_TM
    tok = lambda bi, si: (bi, si, 0)
    const = lambda bi, si: (0, 0)
    return pl.pallas_call(
        _out_kernel,
        grid=(b, s // tm),
        in_specs=[pl.BlockSpec((None, tm, d), tok), pl.BlockSpec((None, tm, d), tok),
                  pl.BlockSpec(w_out.shape, const), pl.BlockSpec((1, d), const),
                  pl.BlockSpec((1, d), const)],
        out_specs=pl.BlockSpec((None, tm, d), tok),
        out_shape=jax.ShapeDtypeStruct((b, s, d), F32),
        compiler_params=_params(2),
        name="out_proj_ln",
    )(o, x, w_out, g, bias)


def _odd_proj_kernel(x_ref, wq_ref, wkv_ref, wg_ref, c_ref, s_ref,
                     q_ref, k_ref, v_ref, sg_ref):
    d = SWA_HEADS * SWA_DIM
    xb = x_ref[...].astype(BF16)
    ct = c_ref[...]
    st = s_ref[...]
    qq = _dot(xb, wq_ref[...])
    scale = SWA_DIM ** -0.5 * LOG2E
    for p in range(d // LANES):
        ps = slice(p * LANES, (p + 1) * LANES)
        sw = slice(d + p * LANES, d + (p + 1) * LANES)
        q_ref[:, ps] = ((qq[:, ps] * ct + qq[:, sw] * st) * scale).astype(BF16)
    kv = _dot(xb, wkv_ref[...])
    k_ref[...] = (kv[:, :LANES] * ct + kv[:, LANES:2 * LANES] * st).astype(BF16)
    v_ref[...] = kv[:, 2 * LANES:].astype(BF16)
    sg_ref[...] = _silu(_dot(xb, wg_ref[...])).astype(BF16)


def _odd_proj(x, w, tabs):
    b, s, d = x.shape
    tm = PROJ_TM
    const = lambda bi, si: (0, 0)
    tok = lambda bi, si: (bi, si, 0)
    tab = lambda bi, si: (si, 0)
    wspec = lambda a: pl.BlockSpec(a.shape, const)
    wide = jax.ShapeDtypeStruct((b, s, d), BF16)
    narrow = jax.ShapeDtypeStruct((b, s, LANES), BF16)
    return pl.pallas_call(
        _odd_proj_kernel,
        grid=(b, s // tm),
        in_specs=[pl.BlockSpec((None, tm, d), tok), wspec(w["wq"]), wspec(w["wkv"]), wspec(w["wg"]),
                  pl.BlockSpec((tm, LANES), tab), pl.BlockSpec((tm, LANES), tab)],
        out_specs=[pl.BlockSpec((None, tm, d), tok), pl.BlockSpec((None, tm, LANES), tok),
                   pl.BlockSpec((None, tm, LANES), tok), pl.BlockSpec((None, tm, d), tok)],
        out_shape=[wide, narrow, narrow, wide],
        compiler_params=_params(2),
        name="odd_proj",
    )(x, w["wq"], w["wkv"], w["wg"], tabs["c"], tabs["s"])


def _swa_kernel(sink_ref, q_ref, k_ref, v_ref, sg_ref, o_ref):
    n_blk = q_ref.shape[0] // QBLK
    n_pair = q_ref.shape[1] // LANES
    qi = pl.program_id(1)
    lane = lax.broadcasted_iota(jnp.int32, (QBLK, LANES), 1)
    lo_half = lane < SWA_DIM
    r = lax.broadcasted_iota(jnp.int32, (QBLK, 2 * QBLK), 0)
    c = lax.broadcasted_iota(jnp.int32, (QBLK, 2 * QBLK), 1)

    def block(blk, _):
        gb = qi * n_blk + blk
        q0 = pl.multiple_of(blk * QBLK, QBLK)
        k0 = pl.multiple_of(jnp.maximum(gb - 1, 0) * QBLK, QBLK)
        diff = (gb * QBLK + r) - (k0 + c)
        band = (diff >= 0) & (diff < WINDOW)
        kw = k_ref[pl.ds(k0, 2 * QBLK), :]
        vw = v_ref[pl.ds(k0, 2 * QBLK), :]
        qrows = []
        for p in range(n_pair):
            qp = q_ref[pl.ds(q0, QBLK), p * LANES:(p + 1) * LANES]
            zero = jnp.zeros_like(qp)
            qrows.append(jnp.where(lo_half, qp, zero))
            qrows.append(jnp.where(lo_half, zero, qp))
        s_all = _dot_nt(jnp.concatenate(qrows, axis=0), kw)
        ps = []
        inv = []
        for rb in range(2 * n_pair):
            sink = sink_ref[rb] * LOG2E
            s = jnp.where(band, s_all[rb * QBLK:(rb + 1) * QBLK, :], -jnp.inf)
            m = jnp.maximum(jnp.max(s, axis=1, keepdims=True), sink)
            p_ = jnp.exp2(s - m)
            inv.append(1.0 / (jnp.sum(p_, axis=1, keepdims=True) + jnp.exp2(sink - m)))
            ps.append(p_.astype(BF16))
        o_all = _dot(jnp.concatenate(ps, axis=0), vw)
        for p in range(n_pair):
            oa = o_all[(2 * p) * QBLK:(2 * p + 1) * QBLK, :] * inv[2 * p]
            ob = o_all[(2 * p + 1) * QBLK:(2 * p + 2) * QBLK, :] * inv[2 * p + 1]
            o = jnp.where(lo_half, oa, ob)
            cols = slice(p * LANES, (p + 1) * LANES)
            o_ref[pl.ds(q0, QBLK), cols] = (
                o * sg_ref[pl.ds(q0, QBLK), cols].astype(F32)).astype(BF16)
        return 0

    lax.fori_loop(0, n_blk, block, 0)


def _swa_attn(sinks, q, k, v, sg):
    b, s, d = q.shape
    tq = SWA_TQ
    tok = lambda bi, qi: (bi, qi, 0)
    full = lambda bi, qi: (bi, 0, 0)
    return pl.pallas_call(
        _swa_kernel,
        grid=(b, s // tq),
        in_specs=[pl.BlockSpec(memory_space=pltpu.SMEM),
                  pl.BlockSpec((None, tq, d), tok),
                  pl.BlockSpec((None, s, LANES), full), pl.BlockSpec((None, s, LANES), full),
                  pl.BlockSpec((None, tq, d), tok)],
        out_specs=pl.BlockSpec((None, tq, d), tok),
        out_shape=jax.ShapeDtypeStruct((b, s, d), BF16),
        compiler_params=_params(2),
        name="swa_attn",
    )(sinks, q, k, v, sg)


def _pad_heads(w, n_heads, dim, offsets):
    k = w.shape[0]
    n_off = len(offsets)
    w4 = w.reshape(k, n_heads // n_off, n_off, dim)
    parts = [jnp.pad(w4[:, :, i, :], ((0, 0), (0, 0), (off, LANES - off - dim)))
             for i, off in enumerate(offsets)]
    return jnp.stack(parts, axis=2).reshape(k, n_heads * LANES)


def _swap_halves(w, n_heads, dim):
    k = w.shape[0]
    w3 = w.reshape(k, n_heads, dim)
    return jnp.concatenate([w3[..., dim // 2:], w3[..., :dim // 2]], axis=-1).reshape(k, n_heads * dim)


def _even_weights(w_in, q_norm, w_uq, kv_norm, w_ukv, b_f):
    sizes = (MLA_Q_RANK, MLA_KV_RANK, MLA_ROPE, FOX_HEADS * FOX_DIM, FOX_HEADS * FOX_DIM,
             FOX_HEADS * FOX_DIM, FOX_HEADS, MLA_HEADS * MLA_V + FOX_HEADS * FOX_DIM)
    cuts = [int(c) for c in np.cumsum(sizes)[:-1]]
    w_cq, w_ckv, w_kpe, w_fq, w_fk, w_fv, w_f, w_g = jnp.split(w_in, cuts, axis=1)
    place = lambda w: _pad_heads(w, 1, MLA_ROPE, (MLA_NOPE,))
    wlat = jnp.concatenate([w_cq, w_ckv, place(w_kpe), place(_swap_halves(w_kpe, 1, MLA_ROPE)),
                            _pad_heads(w_f, 1, FOX_HEADS, (0,))], axis=1)
    v_off = (0, MLA_V)
    dq = MLA_NOPE + MLA_ROPE
    uq3 = w_uq.reshape(MLA_Q_RANK, MLA_HEADS, dq)
    uq_pe = uq3[..., MLA_NOPE:].reshape(MLA_Q_RANK, MLA_HEADS * MLA_ROPE)
    uq_a = _pad_heads(w_uq, MLA_HEADS, dq, (0,))
    uq_b = _pad_heads(_swap_halves(uq_pe, MLA_HEADS, MLA_ROPE), MLA_HEADS, MLA_ROPE, (MLA_NOPE,))
    ukv3 = w_ukv.reshape(MLA_KV_RANK, MLA_HEADS, MLA_NOPE + MLA_V)
    uk = ukv3[..., :MLA_NOPE].reshape(MLA_KV_RANK, MLA_HEADS * MLA_NOPE)
    uv = ukv3[..., MLA_NOPE:].reshape(MLA_KV_RANK, MLA_HEADS * MLA_V)
    bf = lambda a: a.astype(BF16)
    return dict(wlat=bf(wlat),
                wuqt=bf(jnp.concatenate([uq_a, uq_b], axis=1).T),
                wuk=bf(_pad_heads(uk, MLA_HEADS, MLA_NOPE, (0,))),
                wuvt=bf(_pad_heads(uv, MLA_HEADS, MLA_V, v_off).T),
                wfqt=bf(_pad_heads(w_fq, FOX_HEADS, FOX_DIM, (0,)).T),
                wfk=bf(_pad_heads(w_fk, FOX_HEADS, FOX_DIM, (0,))),
                wfvt=bf(_pad_heads(w_fv, FOX_HEADS, FOX_DIM, v_off).T),
                wg=bf(w_g), qn=q_norm.reshape(1, -1), kvn=kv_norm.reshape(1, -1),
                bf=jnp.pad(b_f, (0, LANES - FOX_HEADS)).reshape(1, LANES))


def _pair_heads(w):
    lead = w.shape[:-1]
    w4 = w.reshape(*lead, SWA_KV_HEADS, SWA_HEADS // SWA_KV_HEADS, -1)
    return jnp.swapaxes(w4, -3, -2).reshape(*lead, -1)


def _odd_weights(w_in, sinks, w_out):
    d = SWA_HEADS * SWA_DIM
    dkv = SWA_KV_HEADS * SWA_DIM
    w_q, w_k, w_v, w_g = jnp.split(w_in, [d, d + dkv, d + 2 * dkv], axis=1)
    wq = jnp.concatenate([_pair_heads(w_q), _pair_heads(_swap_halves(w_q, SWA_HEADS, SWA_DIM))], axis=1)
    wkv = jnp.concatenate([w_k, _swap_halves(w_k, SWA_KV_HEADS, SWA_DIM), w_v], axis=1)
    return dict(wq=wq.astype(BF16), wkv=wkv.astype(BF16), wg=_pair_heads(w_g).astype(BF16),
                sinks=_pair_heads(sinks.reshape(-1, 1).T).reshape(-1),
                w_out=_pair_heads(w_out.T).T.astype(BF16))


def _rope_tables(s):
    pos = jnp.arange(s).astype(F32)

    def cs(dim):
        inv = ROPE_THETA ** (-jnp.arange(0, dim, 2, dtype=F32) / dim)
        ang = pos[:, None] * inv[None, :]
        cos, sin = jnp.cos(ang), jnp.sin(ang)
        return jnp.concatenate([cos, cos], axis=1), jnp.concatenate([-sin, sin], axis=1)

    c32, s32 = cs(MLA_ROPE)
    z = lambda n: jnp.zeros((s, n), F32)
    pad = LANES - MLA_NOPE - MLA_ROPE
    scale = (MLA_NOPE + MLA_ROPE) ** -0.5 * LOG2E
    even = dict(cqt=(jnp.concatenate([jnp.ones((s, MLA_NOPE), F32), c32, z(pad)], axis=1) * scale).T,
                sqt=(jnp.concatenate([z(MLA_NOPE), s32, z(pad)], axis=1) * scale).T,
                ck=jnp.concatenate([z(MLA_NOPE), c32, z(pad)], axis=1),
                sk=jnp.concatenate([z(MLA_NOPE), s32, z(pad)], axis=1))
    c64, s64 = cs(SWA_DIM)
    odd = dict(c=jnp.concatenate([c64, c64], axis=1), s=jnp.concatenate([s64, s64], axis=1))
    return even, odd


def kernel(x, even_w_in, even_q_norm, even_w_uq, even_kv_norm, even_w_ukv, even_b_f, even_w_out,
           even_ln_g, even_ln_b, odd_w_in, odd_sinks, odd_w_out, odd_ln_g, odd_ln_b):
    s = x.shape[1]
    even_tabs, odd_tabs = _rope_tables(s)
    for layer in range(DEPTH):
        j = layer // 2
        if layer % 2 == 0:
            w = _even_weights(even_w_in[j], even_q_norm[j], even_w_uq[j], even_kv_norm[j],
                              even_w_ukv[j], even_b_f[j])
            qt, k, vt, sg = _even_proj(x, w, even_tabs)
            o = _causal_attn(qt, k, vt, sg)
            x = _out_proj(o, x, even_w_out[j].astype(BF16),
                          even_ln_g[j].reshape(1, -1), even_ln_b[j].reshape(1, -1))
        else:
            w = _odd_weights(odd_w_in[j], odd_sinks[j], odd_w_out[j])
            q, k, v, sg = _odd_proj(x, w, odd_tabs)
            o = _swa_attn(w["sinks"], q, k, v, sg)
            x = _out_proj(o, x, w["w_out"], odd_ln_g[j].reshape(1, -1), odd_ln_b[j].reshape(1, -1))
    return x
```

```python
import functools
import math

import jax
import jax.numpy as jnp
import numpy as np
from jax import lax
from jax.experimental import pallas as pl
from jax.experimental.pallas import tpu as pltpu

D_MODEL = 1024
DEPTH = 4
ROPE_THETA = 10000.0
QBLK = 128
MLA_HEADS = 8
MLA_NOPE = 64
MLA_ROPE = 32
MLA_V = 64
MLA_Q_RANK = 256
MLA_KV_RANK = 128
FOX_HEADS = 8
FOX_DIM = 64
SWA_HEADS = 16
SWA_KV_HEADS = 2
SWA_DIM = 64
WINDOW = 128
RMS_EPS = 1e-6
LN_EPS = 1e-5
ALPHA = (2 * DEPTH) ** 0.25
LOG2E = math.log2(math.e)

LANES = 128
VMEM_LIMIT = 56 * 1024 * 1024
BF16 = jnp.bfloat16
F32 = jnp.float32

PROJ_TM = 256
ATT_T = 512
ATT_HG = 4
SWA_TQ = 512
EVEN_HEADS = MLA_HEADS + FOX_HEADS
EVEN_HW = EVEN_HEADS * LANES
MLA_HW = MLA_HEADS * LANES
C_LANE = FOX_DIM


def _dot(a, b):
    return jnp.dot(a, b, preferred_element_type=F32)


def _dot_nt(a, b):
    return lax.dot_general(a, b, (((1,), (1,)), ((), ())), preferred_element_type=F32)


def _rms(t, g):
    return t * lax.rsqrt(jnp.mean(t * t, axis=-1, keepdims=True) + RMS_EPS) * g


def _silu(g):
    return g / (1.0 + jnp.exp(-g))


def _split3(v):
    hi = v.astype(BF16)
    r1 = v - hi.astype(F32)
    mid = r1.astype(BF16)
    lo = (r1 - mid.astype(F32)).astype(BF16)
    return hi, mid, lo


def _params(n_axes, flags=None):
    return pltpu.CompilerParams(dimension_semantics=("arbitrary",) * n_axes,
                                vmem_limit_bytes=VMEM_LIMIT, flags=flags)


def _even_proj_kernel(x_ref, wlat_ref, wuqt_ref, wuk_ref, wuvt_ref, wfqt_ref, wfk_ref, wfvt_ref,
                      wg_ref, qn_ref, kvn_ref, bf_ref, cqt_ref, sqt_ref, ck_ref, sk_ref,
                      qt_ref, k_ref, vt_ref, sg_ref, carry_ref):
    tm = x_ref.shape[0]
    xb = x_ref[...].astype(BF16)

    hrow = lax.broadcasted_iota(jnp.int32, (MLA_HW, 1), 0) % (2 * LANES)
    ones_col = jnp.where((hrow == MLA_V) | (hrow == LANES), 1.0, 0.0).astype(F32)
    frow = lax.broadcasted_iota(jnp.int32, (MLA_HW, 1), 0) % LANES
    neg_col = jnp.where((frow >= C_LANE) & (frow < C_LANE + 3), -1.0, 0.0).astype(F32)

    lat = _dot(xb, wlat_ref[...])
    o = 0
    cq = lat[:, o:o + MLA_Q_RANK]
    o += MLA_Q_RANK
    ckv = lat[:, o:o + MLA_KV_RANK]
    o += MLA_KV_RANK
    kpe = lat[:, o:o + LANES] * ck_ref[...] + lat[:, o + LANES:o + 2 * LANES] * sk_ref[...]
    o += 2 * LANES
    fl = lat[:, o:o + LANES] + bf_ref[...]

    rq = _rms(cq, qn_ref[...]).astype(BF16)
    qa = _dot_nt(wuqt_ref[:MLA_HW, :], rq)
    qb = _dot_nt(wuqt_ref[MLA_HW:, :], rq)
    cqt = cqt_ref[...]
    sqt = sqt_ref[...]
    for h in range(MLA_HEADS):
        hs = slice(h * LANES, (h + 1) * LANES)
        qt_ref[hs, :] = (qa[hs, :] * cqt + qb[hs, :] * sqt).astype(BF16)
    rkv = _rms(ckv, kvn_ref[...]).astype(BF16)
    kk = _dot(rkv, wuk_ref[...])
    for h in range(MLA_HEADS):
        hs = slice(h * LANES, (h + 1) * LANES)
        k_ref[:, hs] = (kk[:, hs] + kpe).astype(BF16)
    vt_ref[:MLA_HW, :] = (_dot_nt(wuvt_ref[...], rkv) + ones_col).astype(BF16)

    lane = lax.broadcasted_iota(jnp.int32, (1, LANES), 1)
    lf = (jnp.minimum(fl, 0.0) - jnp.log1p(jnp.exp(-jnp.abs(fl)))) * LOG2E
    lf = jnp.where(lane < FOX_HEADS, lf, 0.0)
    row = lax.broadcasted_iota(jnp.int32, (tm, tm), 0)
    col = lax.broadcasted_iota(jnp.int32, (tm, tm), 1)
    tri = jnp.where(col <= row, 1.0, 0.0).astype(BF16)
    hi, mid, lo = _split3(lf)
    c_loc = _dot(tri, hi) + _dot(tri, mid) + _dot(tri, lo)

    @pl.when(pl.program_id(1) == 0)
    def _():
        carry_ref[...] = jnp.zeros_like(carry_ref)

    c = c_loc + carry_ref[0:1, :]
    carry_ref[...] = jnp.broadcast_to(c[tm - 1:tm, :], carry_ref.shape)
    src = lax.broadcasted_iota(jnp.int32, (LANES, MLA_HW), 0)
    dst = lax.broadcasted_iota(jnp.int32, (LANES, MLA_HW), 1)
    k_extra = None
    for t, part in enumerate(_split3(c)):
        place = jnp.where((dst == src * LANES + C_LANE + t) & (src < FOX_HEADS), 1.0, 0.0).astype(BF16)
        term = _dot(part, place)
        k_extra = term if k_extra is None else k_extra + term

    qt_ref[MLA_HW:, :] = (_dot_nt(wfqt_ref[...], xb) * (FOX_DIM ** -0.5 * LOG2E) + neg_col).astype(BF16)
    k_ref[:, MLA_HW:] = (_dot(xb, wfk_ref[...]) + k_extra).astype(BF16)
    vt_ref[MLA_HW:, :] = (_dot_nt(wfvt_ref[...], xb) + ones_col).astype(BF16)

    sg_ref[...] = _silu(_dot(xb, wg_ref[...])).astype(BF16)


def _even_proj(x, w, tabs):
    b, s, d = x.shape
    tm = PROJ_TM
    const = lambda bi, si: (0, 0)
    tok = lambda bi, si: (bi, si, 0)
    tok_t = lambda bi, si: (bi, 0, si)
    wspec = lambda a: pl.BlockSpec(a.shape, const)
    names = ("wlat", "wuqt", "wuk", "wuvt", "wfqt", "wfk", "wfvt", "wg", "qn", "kvn", "bf")
    in_specs = [pl.BlockSpec((None, tm, d), tok)] + [wspec(w[k]) for k in names]
    in_specs += [pl.BlockSpec((LANES, tm), lambda bi, si: (0, si))] * 2
    in_specs += [pl.BlockSpec((tm, LANES), lambda bi, si: (si, 0))] * 2
    out_specs = [pl.BlockSpec((None, EVEN_HW, tm), tok_t), pl.BlockSpec((None, tm, EVEN_HW), tok),
                 pl.BlockSpec((None, EVEN_HW, tm), tok_t), pl.BlockSpec((None, tm, d), tok)]
    out_shape = [jax.ShapeDtypeStruct((b, EVEN_HW, s), BF16), jax.ShapeDtypeStruct((b, s, EVEN_HW), BF16),
                 jax.ShapeDtypeStruct((b, EVEN_HW, s), BF16), jax.ShapeDtypeStruct((b, s, d), BF16)]
    return pl.pallas_call(
        _even_proj_kernel,
        grid=(b, s // tm),
        in_specs=in_specs,
        out_specs=out_specs,
        out_shape=out_shape,
        scratch_shapes=[pltpu.VMEM((8, LANES), F32)],
        compiler_params=_params(2),
        name="even_proj",
    )(x, *[w[k] for k in names], tabs["cqt"], tabs["sqt"], tabs["ck"], tabs["sk"])


def _causal_attn_kernel(qt_ref, k_ref, vt_ref, sg_ref, o_ref, sa_ref, sb_ref):
    t = qt_ref.shape[1]
    n_heads = qt_ref.shape[0] // LANES
    qi = pl.program_id(2)
    heads = [slice(h * LANES, (h + 1) * LANES) for h in range(n_heads)]

    def key_slice(kj):
        return pl.ds(pl.multiple_of(kj * t, t), t)

    def col_max(s):
        return jnp.max(s, axis=0, keepdims=True)

    def scores(kj, buf):
        ks = key_slice(kj)
        tile_max = []
        for h, hs in enumerate(heads):
            s = _dot(k_ref[ks, hs], qt_ref[hs, :])
            buf[h] = s
            tile_max.append(col_max(s))
        return tuple(tile_max)

    def process(kj, buf, tile_max, state):
        ks = key_slice(kj)
        out = []
        for h, hs in enumerate(heads):
            m, acc = state[h]
            m_new = jnp.maximum(m, tile_max[h])
            p = jnp.exp2(buf[h] - m_new).astype(BF16)
            acc = jnp.exp2(m - m_new) * acc + _dot(vt_ref[hs, ks], p)
            out.append((m_new, acc))
        return tuple(out)

    def stage(kj, cur, nxt, tile_max, state):
        next_max = scores(kj + 1, nxt)
        return next_max, process(kj, cur, tile_max, state)

    def two_stages(i, carry):
        tile_max, state = carry
        tile_max, state = stage(2 * i, sa_ref, sb_ref, tile_max, state)
        return stage(2 * i + 1, sb_ref, sa_ref, tile_max, state)

    def diagonal(buf, state):
        causal = (lax.broadcasted_iota(jnp.int32, (t, t), 0) <= lax.broadcasted_iota(jnp.int32, (t, t), 1))
        tile_max = []
        for h in range(n_heads):
            s = jnp.where(causal, buf[h], -jnp.inf)
            buf[h] = s
            tile_max.append(col_max(s))
        return process(qi, buf, tuple(tile_max), state)

    init = tuple((jnp.full((1, t), -jnp.inf, F32), jnp.zeros((LANES, t), F32)) for _ in range(n_heads))
    tile_max, state = lax.fori_loop(0, qi // 2, two_stages, (scores(0, sa_ref), init))

    def odd_tail(tile_max, state):
        _, state = stage(qi - 1, sa_ref, sb_ref, tile_max, state)
        return diagonal(sb_ref, state)

    def even_tail(tile_max, state):
        return diagonal(sa_ref, state)

    state = lax.cond(qi % 2 == 1, odd_tail, even_tail, tile_max, state)

    lane = lax.broadcasted_iota(jnp.int32, (t, LANES), 1)
    for hp in range(n_heads // 2):
        a0 = state[2 * hp][1]
        a1 = state[2 * hp + 1][1]
        o0 = (a0 / a0[MLA_V:MLA_V + 1, :]).T
        o1 = (a1 / a1[0:1, :]).T
        o = jnp.where(lane < MLA_V, o0, o1)
        cols = slice(hp * LANES, (hp + 1) * LANES)
        o_ref[:, cols] = (o * sg_ref[:, cols].astype(F32)).astype(BF16)


def _causal_attn(qt, k, vt, sg):
    b, s, hw = k.shape
    t = ATT_T
    gw = ATT_HG * LANES
    ow = ATT_HG * MLA_V
    return pl.pallas_call(
        _causal_attn_kernel,
        grid=(b, hw // gw, s // t),
        in_specs=[pl.BlockSpec((None, gw, t), lambda bi, g, qi: (bi, g, qi)),
                  pl.BlockSpec((None, s, gw), lambda bi, g, qi: (bi, 0, g)),
                  pl.BlockSpec((None, gw, s), lambda bi, g, qi: (bi, g, 0)),
                  pl.BlockSpec((None, t, ow), lambda bi, g, qi: (bi, qi, g))],
        out_specs=pl.BlockSpec((None, t, ow), lambda bi, g, qi: (bi, qi, g)),
        out_shape=jax.ShapeDtypeStruct((b, s, (hw // LANES) * MLA_V), BF16),
        scratch_shapes=[pltpu.VMEM((ATT_HG, t, t), F32), pltpu.VMEM((ATT_HG, t, t), F32)],
        compiler_params=_params(3),
        name="causal_attn",
    )(qt, k, vt, sg)


def _out_kernel(o_ref, x_ref, w_ref, g_ref, b_ref, y_ref):
    z = ALPHA * x_ref[...] + _dot(o_ref[...], w_ref[...])
    mu = jnp.mean(z, axis=-1, keepdims=True)
    zc = z - mu
    var = jnp.mean(zc * zc, axis=-1, keepdims=True)
    y_ref[...] = zc * lax.rsqrt(var + LN_EPS) * g_ref[...] + b_ref[...]


def _out_proj(o, x, w_out, g, bias):
    b, s, d = x.shape
    tm = PROJ_TM
    tok = lambda bi, si: (bi, si, 0)
    const = lambda bi, si: (0, 0)
    return pl.pallas_call(
        _out_kernel,
        grid=(b, s // tm),
        in_specs=[pl.BlockSpec((None, tm, d), tok), pl.BlockSpec((None, tm, d), tok),
                  pl.BlockSpec(w_out.shape, const), pl.BlockSpec((1, d), const),
                  pl.BlockSpec((1, d), const)],
        out_specs=pl.BlockSpec((None, tm, d), tok),
        out_shape=jax.ShapeDtypeStruct((b, s, d), F32),
        compiler_params=_params(2),
        name="out_proj_ln",
    )(o, x, w_out, g, bias)


def _odd_proj_kernel(x_ref, wq_ref, wkv_ref, wg_ref, c_ref, s_ref,
                     q_ref, k_ref, v_ref, sg_ref):
    d = SWA_HEADS * SWA_DIM
    xb = x_ref[...].astype(BF16)
    ct = c_ref[...]
    st = s_ref[...]
    qq = _dot(xb, wq_ref[...])
    scale = SWA_DIM ** -0.5 * LOG2E
    for p in range(d // LANES):
        ps = slice(p * LANES, (p + 1) * LANES)
        sw = slice(d + p * LANES, d + (p + 1) * LANES)
        q_ref[:, ps] = ((qq[:, ps] * ct + qq[:, sw] * st) * scale).astype(BF16)
    kv = _dot(xb, wkv_ref[...])
    k_ref[...] = (kv[:, :LANES] * ct + kv[:, LANES:2 * LANES] * st).astype(BF16)
    v_ref[...] = kv[:, 2 * LANES:].astype(BF16)
    sg_ref[...] = _silu(_dot(xb, wg_ref[...])).astype(BF16)


def _odd_proj(x, w, tabs):
    b, s, d = x.shape
    tm = PROJ_TM
    const = lambda bi, si: (0, 0)
    tok = lambda bi, si: (bi, si, 0)
    tab = lambda bi, si: (si, 0)
    wspec = lambda a: pl.BlockSpec(a.shape, const)
    wide = jax.ShapeDtypeStruct((b, s, d), BF16)
    narrow = jax.ShapeDtypeStruct((b, s, LANES), BF16)
    return pl.pallas_call(
        _odd_proj_kernel,
        grid=(b, s // tm),
        in_specs=[pl.BlockSpec((None, tm, d), tok), wspec(w["wq"]), wspec(w["wkv"]), wspec(w["wg"]),
                  pl.BlockSpec((tm, LANES), tab), pl.BlockSpec((tm, LANES), tab)],
        out_specs=[pl.BlockSpec((None, tm, d), tok), pl.BlockSpec((None, tm, LANES), tok),
                   pl.BlockSpec((None, tm, LANES), tok), pl.BlockSpec((None, tm, d), tok)],
        out_shape=[wide, narrow, narrow, wide],
        compiler_params=_params(2),
        name="odd_proj",
    )(x, w["wq"], w["wkv"], w["wg"], tabs["c"], tabs["s"])


def _swa_kernel(sink_ref, q_ref, k_ref, v_ref, sg_ref, o_ref):
    n_blk = q_ref.shape[0] // QBLK
    n_pair = q_ref.shape[1] // LANES
    qi = pl.program_id(1)
    lane = lax.broadcasted_iota(jnp.int32, (QBLK, LANES), 1)
    lo_half = lane < SWA_DIM
    r = lax.broadcasted_iota(jnp.int32, (QBLK, 2 * QBLK), 0)
    c = lax.broadcasted_iota(jnp.int32, (QBLK, 2 * QBLK), 1)

    def block(blk, _):
        gb = qi * n_blk + blk
        q0 = pl.multiple_of(blk * QBLK, QBLK)
        k0 = pl.multiple_of(jnp.maximum(gb - 1, 0) * QBLK, QBLK)
        diff = (gb * QBLK + r) - (k0 + c)
        band = (diff >= 0) & (diff < WINDOW)
        kw = k_ref[pl.ds(k0, 2 * QBLK), :]
        vw = v_ref[pl.ds(k0, 2 * QBLK), :]
        qrows = []
        for p in range(n_pair):
            qp = q_ref[pl.ds(q0, QBLK), p * LANES:(p + 1) * LANES]
            zero = jnp.zeros_like(qp)
            qrows.append(jnp.where(lo_half, qp, zero))
            qrows.append(jnp.where(lo_half, zero, qp))
        s_all = _dot_nt(jnp.concatenate(qrows, axis=0), kw)
        ps = []
        inv = []
        for rb in range(2 * n_pair):
            sink = sink_ref[rb] * LOG2E
            s = jnp.where(band, s_all[rb * QBLK:(rb + 1) * QBLK, :], -jnp.inf)
            m = jnp.maximum(jnp.max(s, axis=1, keepdims=True), sink)
            p_ = jnp.exp2(s - m)
            inv.append(1.0 / (jnp.sum(p_, axis=1, keepdims=True) + jnp.exp2(sink - m)))
            ps.append(p_.astype(BF16))
        o_all = _dot(jnp.concatenate(ps, axis=0), vw)
        for p in range(n_pair):
            oa = o_all[(2 * p) * QBLK:(2 * p + 1) * QBLK, :] * inv[2 * p]
            ob = o_all[(2 * p + 1) * QBLK:(2 * p + 2) * QBLK, :] * inv[2 * p + 1]
            o = jnp.where(lo_half, oa, ob)
            cols = slice(p * LANES, (p + 1) * LANES)
            o_ref[pl.ds(q0, QBLK), cols] = (
                o * sg_ref[pl.ds(q0, QBLK), cols].astype(F32)).astype(BF16)
        return 0

    lax.fori_loop(0, n_blk, block, 0)


def _swa_attn(sinks, q, k, v, sg):
    b, s, d = q.shape
    tq = SWA_TQ
    tok = lambda bi, qi: (bi, qi, 0)
    full = lambda bi, qi: (bi, 0, 0)
    return pl.pallas_call(
        _swa_kernel,
        grid=(b, s // tq),
        in_specs=[pl.BlockSpec(memory_space=pltpu.SMEM),
                  pl.BlockSpec((None, tq, d), tok),
                  pl.BlockSpec((None, s, LANES), full), pl.BlockSpec((None, s, LANES), full),
                  pl.BlockSpec((None, tq, d), tok)],
        out_specs=pl.BlockSpec((None, tq, d), tok),
        out_shape=jax.ShapeDtypeStruct((b, s, d), BF16),
        compiler_params=_params(2),
        name="swa_attn",
    )(sinks, q, k, v, sg)


def _pad_heads(w, n_heads, dim, offsets):
    k = w.shape[0]
    n_off = len(offsets)
    w4 = w.reshape(k, n_heads // n_off, n_off, dim)
    parts = [jnp.pad(w4[:, :, i, :], ((0, 0), (0, 0), (off, LANES - off - dim)))
             for i, off in enumerate(offsets)]
    return jnp.stack(parts, axis=2).reshape(k, n_heads * LANES)


def _swap_halves(w, n_heads, dim):
    k = w.shape[0]
    w3 = w.reshape(k, n_heads, dim)
    return jnp.concatenate([w3[..., dim // 2:], w3[..., :dim // 2]], axis=-1).reshape(k, n_heads * dim)


def _even_weights(w_in, q_norm, w_uq, kv_norm, w_ukv, b_f):
    sizes = (MLA_Q_RANK, MLA_KV_RANK, MLA_ROPE, FOX_HEADS * FOX_DIM, FOX_HEADS * FOX_DIM,
             FOX_HEADS * FOX_DIM, FOX_HEADS, MLA_HEADS * MLA_V + FOX_HEADS * FOX_DIM)
    cuts = [int(c) for c in np.cumsum(sizes)[:-1]]
    w_cq, w_ckv, w_kpe, w_fq, w_fk, w_fv, w_f, w_g = jnp.split(w_in, cuts, axis=1)
    place = lambda w: _pad_heads(w, 1, MLA_ROPE, (MLA_NOPE,))
    wlat = jnp.concatenate([w_cq, w_ckv, place(w_kpe), place(_swap_halves(w_kpe, 1, MLA_ROPE)),
                            _pad_heads(w_f, 1, FOX_HEADS, (0,))], axis=1)
    v_off = (0, MLA_V)
    dq = MLA_NOPE + MLA_ROPE
    uq3 = w_uq.reshape(MLA_Q_RANK, MLA_HEADS, dq)
    uq_pe = uq3[..., MLA_NOPE:].reshape(MLA_Q_RANK, MLA_HEADS * MLA_ROPE)
    uq_a = _pad_heads(w_uq, MLA_HEADS, dq, (0,))
    uq_b = _pad_heads(_swap_halves(uq_pe, MLA_HEADS, MLA_ROPE), MLA_HEADS, MLA_ROPE, (MLA_NOPE,))
    ukv3 = w_ukv.reshape(MLA_KV_RANK, MLA_HEADS, MLA_NOPE + MLA_V)
    uk = ukv3[..., :MLA_NOPE].reshape(MLA_KV_RANK, MLA_HEADS * MLA_NOPE)
    uv = ukv3[..., MLA_NOPE:].reshape(MLA_KV_RANK, MLA_HEADS * MLA_V)
    bf = lambda a: a.astype(BF16)
    return dict(wlat=bf(wlat),
                wuqt=bf(jnp.concatenate([uq_a, uq_b], axis=1).T),
                wuk=bf(_pad_heads(uk, MLA_HEADS, MLA_NOPE, (0,))),
                wuvt=bf(_pad_heads(uv, MLA_HEADS, MLA_V, v_off).T),
                wfqt=bf(_pad_heads(w_fq, FOX_HEADS, FOX_DIM, (0,)).T),
                wfk=bf(_pad_heads(w_fk, FOX_HEADS, FOX_DIM, (0,))),
                wfvt=bf(_pad_heads(w_fv, FOX_HEADS, FOX_DIM, v_off).T),
                wg=bf(w_g), qn=q_norm.reshape(1, -1), kvn=kv_norm.reshape(1, -1),
                bf=jnp.pad(b_f, (0, LANES - FOX_HEADS)).reshape(1, LANES))


def _pair_heads(w):
    lead = w.shape[:-1]
    w4 = w.reshape(*lead, SWA_KV_HEADS, SWA_HEADS // SWA_KV_HEADS, -1)
    return jnp.swapaxes(w4, -3, -2).reshape(*lead, -1)


def _odd_weights(w_in, sinks, w_out):
    d = SWA_HEADS * SWA_DIM
    dkv = SWA_KV_HEADS * SWA_DIM
    w_q, w_k, w_v, w_g = jnp.split(w_in, [d, d + dkv, d + 2 * dkv], axis=1)
    wq = jnp.concatenate([_pair_heads(w_q), _pair_heads(_swap_halves(w_q, SWA_HEADS, SWA_DIM))], axis=1)
    wkv = jnp.concatenate([w_k, _swap_halves(w_k, SWA_KV_HEADS, SWA_DIM), w_v], axis=1)
    return dict(wq=wq.astype(BF16), wkv=wkv.astype(BF16), wg=_pair_heads(w_g).astype(BF16),
                sinks=_pair_heads(sinks.reshape(-1, 1).T).reshape(-1),
                w_out=_pair_heads(w_out.T).T.astype(BF16))


def _rope_tables(s):
    pos = jnp.arange(s).astype(F32)

    def cs(dim):
        inv = ROPE_THETA ** (-jnp.arange(0, dim, 2, dtype=F32) / dim)
        ang = pos[:, None] * inv[None, :]
        cos, sin = jnp.cos(ang), jnp.sin(ang)
        return jnp.concatenate([cos, cos], axis=1), jnp.concatenate([-sin, sin], axis=1)

    c32, s32 = cs(MLA_ROPE)
    z = lambda n: jnp.zeros((s, n), F32)
    pad = LANES - MLA_NOPE - MLA_ROPE
    scale = (MLA_NOPE + MLA_ROPE) ** -0.5 * LOG2E
    even = dict(cqt=(jnp.concatenate([jnp.ones((s, MLA_NOPE), F32), c32, z(pad)], axis=1) * scale).T,
                sqt=(jnp.concatenate([z(MLA_NOPE), s32, z(pad)], axis=1) * scale).T,
                ck=jnp.concatenate([z(MLA_NOPE), c32, z(pad)], axis=1),
                sk=jnp.concatenate([z(MLA_NOPE), s32, z(pad)], axis=1))
    c64, s64 = cs(SWA_DIM)
    odd = dict(c=jnp.concatenate([c64, c64], axis=1), s=jnp.concatenate([s64, s64], axis=1))
    return even, odd


def kernel(x, even_w_in, even_q_norm, even_w_uq, even_kv_norm, even_w_ukv, even_b_f, even_w_out,
           even_ln_g, even_ln_b, odd_w_in, odd_sinks, odd_w_out, odd_ln_g, odd_ln_b):
    s = x.shape[1]
    even_tabs, odd_tabs = _rope_tables(s)
    for layer in range(DEPTH):
        j = layer // 2
        if layer % 2 == 0:
            w = _even_weights(even_w_in[j], even_q_norm[j], even_w_uq[j], even_kv_norm[j],
                              even_w_ukv[j], even_b_f[j])
            qt, k, vt, sg = _even_proj(x, w, even_tabs)
            o = _causal_attn(qt, k, vt, sg)
            x = _out_proj(o, x, even_w_out[j].astype(BF16),
                          even_ln_g[j].reshape(1, -1), even_ln_b[j].reshape(1, -1))
        else:
            w = _odd_weights(odd_w_in[j], odd_sinks[j], odd_w_out[j])
            q, k, v, sg = _odd_proj(x, w, odd_tabs)
            o = _swa_attn(w["sinks"], q, k, v, sg)
            x = _out_proj(o, x, w["w_out"], odd_ln_g[j].reshape(1, -1), odd_ln_b[j].reshape(1, -1))
    return x
```

```python
import functools
import math

import jax
import jax.numpy as jnp
import numpy as np
from jax import lax
from jax.experimental import pallas as pl
from jax.experimental.pallas import tpu as pltpu

D_MODEL = 1024
DEPTH = 4
ROPE_THETA = 10000.0
QBLK = 128
MLA_HEADS = 8
MLA_NOPE = 64
MLA_ROPE = 32
MLA_V = 64
MLA_Q_RANK = 256
MLA_KV_RANK = 128
FOX_HEADS = 8
FOX_DIM = 64
SWA_HEADS = 16
SWA_KV_HEADS = 2
SWA_DIM = 64
WINDOW = 128
RMS_EPS = 1e-6
LN_EPS = 1e-5
ALPHA = (2 * DEPTH) ** 0.25
LOG2E = math.log2(math.e)

LANES = 128
VMEM_LIMIT = 56 * 1024 * 1024
BF16 = jnp.bfloat16
F32 = jnp.float32

PROJ_TM = 256
ATT_T = 512
ATT_HG = 4
SWA_TQ = 512
EVEN_HEADS = MLA_HEADS + FOX_HEADS
EVEN_HW = EVEN_HEADS * LANES
MLA_HW = MLA_HEADS * LANES
C_LANE = FOX_DIM


def _dot(a, b):
    return jnp.dot(a, b, preferred_element_type=F32)


def _dot_nt(a, b):
    return lax.dot_general(a, b, (((1,), (1,)), ((), ())), preferred_element_type=F32)


def _rms(t, g):
    return t * lax.rsqrt(jnp.mean(t * t, axis=-1, keepdims=True) + RMS_EPS) * g


def _silu(g):
    return g / (1.0 + jnp.exp(-g))


def _split3(v):
    hi = v.astype(BF16)
    r1 = v - hi.astype(F32)
    mid = r1.astype(BF16)
    lo = (r1 - mid.astype(F32)).astype(BF16)
    return hi, mid, lo


def _params(n_axes, flags=None):
    return pltpu.CompilerParams(dimension_semantics=("arbitrary",) * n_axes,
                                vmem_limit_bytes=VMEM_LIMIT, flags=flags)


def _even_proj_kernel(x_ref, wlat_ref, wuqt_ref, wuk_ref, wuvt_ref, wfqt_ref, wfk_ref, wfvt_ref,
                      wg_ref, qn_ref, kvn_ref, bf_ref, cqt_ref, sqt_ref, ck_ref, sk_ref,
                      qt_ref, k_ref, vt_ref, sg_ref, carry_ref):
    tm = x_ref.shape[0]
    xb = x_ref[...].astype(BF16)

    hrow = lax.broadcasted_iota(jnp.int32, (MLA_HW, 1), 0) % (2 * LANES)
    ones_col = jnp.where((hrow == MLA_V) | (hrow == LANES), 1.0, 0.0).astype(F32)
    frow = lax.broadcasted_iota(jnp.int32, (MLA_HW, 1), 0) % LANES
    neg_col = jnp.where((frow >= C_LANE) & (frow < C_LANE + 3), -1.0, 0.0).astype(F32)

    lat = _dot(xb, wlat_ref[...])
    o = 0
    cq = lat[:, o:o + MLA_Q_RANK]
    o += MLA_Q_RANK
    ckv = lat[:, o:o + MLA_KV_RANK]
    o += MLA_KV_RANK
    kpe = lat[:, o:o + LANES] * ck_ref[...] + lat[:, o + LANES:o + 2 * LANES] * sk_ref[...]
    o += 2 * LANES
    fl = lat[:, o:o + LANES] + bf_ref[...]

    rq = _rms(cq, qn_ref[...]).astype(BF16)
    qa = _dot_nt(wuqt_ref[:MLA_HW, :], rq)
    qb = _dot_nt(wuqt_ref[MLA_HW:, :], rq)
    cqt = cqt_ref[...]
    sqt = sqt_ref[...]
    for h in range(MLA_HEADS):
        hs = slice(h * LANES, (h + 1) * LANES)
        qt_ref[hs, :] = (qa[hs, :] * cqt + qb[hs, :] * sqt).astype(BF16)
    rkv = _rms(ckv, kvn_ref[...]).astype(BF16)
    kk = _dot(rkv, wuk_ref[...])
    for h in range(MLA_HEADS):
        hs = slice(h * LANES, (h + 1) * LANES)
        k_ref[:, hs] = (kk[:, hs] + kpe).astype(BF16)
    vt_ref[:MLA_HW, :] = (_dot_nt(wuvt_ref[...], rkv) + ones_col).astype(BF16)

    lane = lax.broadcasted_iota(jnp.int32, (1, LANES), 1)
    lf = (jnp.minimum(fl, 0.0) - jnp.log1p(jnp.exp(-jnp.abs(fl)))) * LOG2E
    lf = jnp.where(lane < FOX_HEADS, lf, 0.0)
    row = lax.broadcasted_iota(jnp.int32, (tm, tm), 0)
    col = lax.broadcasted_iota(jnp.int32, (tm, tm), 1)
    tri = jnp.where(col <= row, 1.0, 0.0).astype(BF16)
    hi, mid, lo = _split3(lf)
    c_loc = _dot(tri, hi) + _dot(tri, mid) + _dot(tri, lo)

    @pl.when(pl.program_id(1) == 0)
    def _():
        carry_ref[...] = jnp.zeros_like(carry_ref)

    c = c_loc + carry_ref[0:1, :]
    carry_ref[...] = jnp.broadcast_to(c[tm - 1:tm, :], carry_ref.shape)
    src = lax.broadcasted_iota(jnp.int32, (LANES, MLA_HW), 0)
    dst = lax.broadcasted_iota(jnp.int32, (LANES, MLA_HW), 1)
    k_extra = None
    for t, part in enumerate(_split3(c)):
        place = jnp.where((dst == src * LANES + C_LANE + t) & (src < FOX_HEADS), 1.0, 0.0).astype(BF16)
        term = _dot(part, place)
        k_extra = term if k_extra is None else k_extra + term

    qt_ref[MLA_HW:, :] = (_dot_nt(wfqt_ref[...], xb) * (FOX_DIM ** -0.5 * LOG2E) + neg_col).astype(BF16)
    k_ref[:, MLA_HW:] = (_dot(xb, wfk_ref[...]) + k_extra).astype(BF16)
    vt_ref[MLA_HW:, :] = (_dot_nt(wfvt_ref[...], xb) + ones_col).astype(BF16)

    sg_ref[...] = _silu(_dot(xb, wg_ref[...])).astype(BF16)


def _even_proj(x, w, tabs):
    b, s, d = x.shape
    tm = PROJ_TM
    const = lambda bi, si: (0, 0)
    tok = lambda bi, si: (bi, si, 0)
    tok_t = lambda bi, si: (bi, 0, si)
    wspec = lambda a: pl.BlockSpec(a.shape, const)
    names = ("wlat", "wuqt", "wuk", "wuvt", "wfqt", "wfk", "wfvt", "wg", "qn", "kvn", "bf")
    in_specs = [pl.BlockSpec((None, tm, d), tok)] + [wspec(w[k]) for k in names]
    in_specs += [pl.BlockSpec((LANES, tm), lambda bi, si: (0, si))] * 2
    in_specs += [pl.BlockSpec((tm, LANES), lambda bi, si: (si, 0))] * 2
    out_specs = [pl.BlockSpec((None, EVEN_HW, tm), tok_t), pl.BlockSpec((None, tm, EVEN_HW), tok),
                 pl.BlockSpec((None, EVEN_HW, tm), tok_t), pl.BlockSpec((None, tm, d), tok)]
    out_shape = [jax.ShapeDtypeStruct((b, EVEN_HW, s), BF16), jax.ShapeDtypeStruct((b, s, EVEN_HW), BF16),
                 jax.ShapeDtypeStruct((b, EVEN_HW, s), BF16), jax.ShapeDtypeStruct((b, s, d), BF16)]
    return pl.pallas_call(
        _even_proj_kernel,
        grid=(b, s // tm),
        in_specs=in_specs,
        out_specs=out_specs,
        out_shape=out_shape,
        scratch_shapes=[pltpu.VMEM((8, LANES), F32)],
        compiler_params=_params(2),
        name="even_proj",
    )(x, *[w[k] for k in names], tabs["cqt"], tabs["sqt"], tabs["ck"], tabs["sk"])


def _causal_attn_kernel(qt_ref, qtn_ref, k_ref, vt_ref, sg_ref, o_ref,
                        sa_ref, sb_ref, sc_ref, mc_ref):
    t = qt_ref.shape[1]
    n_heads = qt_ref.shape[0] // LANES
    qi = pl.program_id(2)
    heads = [slice(h * LANES, (h + 1) * LANES) for h in range(n_heads)]

    def key_slice(kj):
        return pl.ds(pl.multiple_of(kj * t, t), t)

    def col_max(s):
        return jnp.max(s, axis=0, keepdims=True)

    def causal_mask(s):
        keep = (lax.broadcasted_iota(jnp.int32, (t, t), 0) <= lax.broadcasted_iota(jnp.int32, (t, t), 1))
        return jnp.where(keep, s, -jnp.inf)

    def scores_head(q_ref, kj, buf, h, masked=False):
        s = _dot(k_ref[key_slice(kj), heads[h]], q_ref[heads[h], :])
        if masked:
            s = causal_mask(s)
        buf[h] = s
        return col_max(s)

    def process_head(kj, s, h, tile_max, m, acc):
        m_new = jnp.maximum(m, tile_max)
        p = jnp.exp2(s - m_new).astype(BF16)
        return m_new, jnp.exp2(m - m_new) * acc + _dot(vt_ref[heads[h], key_slice(kj)], p)

    def stage(kj, cur, nxt, tile_max, state, mask_next=False):
        next_max, out = [], []
        for h in range(n_heads):
            next_max.append(scores_head(qt_ref, kj + 1, nxt, h, mask_next))
            out.append(process_head(kj, cur[h], h, tile_max[h], *state[h]))
        return tuple(next_max), tuple(out)

    def two_stages(i, carry):
        tile_max, state = carry
        tile_max, state = stage(2 * i + 1, sa_ref, sb_ref, tile_max, state)
        return stage(2 * i + 2, sb_ref, sa_ref, tile_max, state)

    def prefetch_head(h):
        mc_ref[h, 0:1, :] = scores_head(qtn_ref, 0, sc_ref, h)

    def last_tile(buf, tile_max, state, masked, prefetch_first):
        out = []
        for h in range(n_heads):
            if prefetch_first:
                prefetch_head(h)
            s = buf[h]
            if masked:
                s = causal_mask(s)
            out.append(process_head(qi, s, h, col_max(s) if masked else tile_max[h], *state[h]))
            if not prefetch_first:
                prefetch_head(h)
        return tuple(out)

    @pl.when(qi == 0)
    def _():
        for h in range(n_heads):
            mc_ref[h, 0:1, :] = scores_head(qt_ref, 0, sc_ref, h)

    def only_tile(state):
        return last_tile(sc_ref, None, state, masked=True, prefetch_first=False)

    def several_tiles(state):
        tile_max = tuple(mc_ref[h, 0:1, :] for h in range(n_heads))
        tile_max, state = stage(0, sc_ref, sa_ref, tile_max, state)
        tile_max, state = lax.fori_loop(0, (qi - 1) // 2, two_stages, (tile_max, state))

        def odd_tail(tile_max, state):
            return last_tile(sa_ref, None, state, masked=True, prefetch_first=True)

        def even_tail(tile_max, state):
            tile_max, state = stage(qi - 1, sa_ref, sb_ref, tile_max, state, mask_next=True)
            return last_tile(sb_ref, tile_max, state, masked=False, prefetch_first=True)

        return lax.cond(qi % 2 == 1, odd_tail, even_tail, tile_max, state)

    init = tuple((jnp.full((1, t), -jnp.inf, F32), jnp.zeros((LANES, t), F32)) for _ in range(n_heads))
    state = lax.cond(qi == 0, only_tile, several_tiles, init)

    row = lax.broadcasted_iota(jnp.int32, (LANES, t), 0)
    for hp in range(n_heads // 2):
        a0 = state[2 * hp][1]
        a1 = state[2 * hp + 1][1]
        pair_t = jnp.where(row < MLA_V, a0 / a0[MLA_V:MLA_V + 1, :], a1 / a1[0:1, :])
        cols = slice(hp * LANES, (hp + 1) * LANES)
        o_ref[:, cols] = (pair_t.T * sg_ref[:, cols].astype(F32)).astype(BF16)


def _causal_attn(qt, k, vt, sg):
    b, s, hw = k.shape
    t = ATT_T
    nq = s // t
    gw = ATT_HG * LANES
    ow = ATT_HG * MLA_V
    score_buf = pltpu.VMEM((ATT_HG, t, t), F32)
    return pl.pallas_call(
        _causal_attn_kernel,
        grid=(b, hw // gw, nq),
        in_specs=[pl.BlockSpec((None, gw, t), lambda bi, g, qi: (bi, g, qi)),
                  pl.BlockSpec((None, gw, t), lambda bi, g, qi: (bi, g, jnp.minimum(qi + 1, nq - 1))),
                  pl.BlockSpec((None, s, gw), lambda bi, g, qi: (bi, 0, g)),
                  pl.BlockSpec((None, gw, s), lambda bi, g, qi: (bi, g, 0)),
                  pl.BlockSpec((None, t, ow), lambda bi, g, qi: (bi, qi, g))],
        out_specs=pl.BlockSpec((None, t, ow), lambda bi, g, qi: (bi, qi, g)),
        out_shape=jax.ShapeDtypeStruct((b, s, (hw // LANES) * MLA_V), BF16),
        scratch_shapes=[score_buf, score_buf, score_buf, pltpu.VMEM((ATT_HG, 8, t), F32)],
        compiler_params=_params(3),
        name="causal_attn",
    )(qt, qt, k, vt, sg)


def _out_kernel(o_ref, x_ref, w_ref, g_ref, b_ref, y_ref):
    z = ALPHA * x_ref[...] + _dot(o_ref[...], w_ref[...])
    mu = jnp.mean(z, axis=-1, keepdims=True)
    zc = z - mu
    var = jnp.mean(zc * zc, axis=-1, keepdims=True)
    y_ref[...] = zc * lax.rsqrt(var + LN_EPS) * g_ref[...] + b_ref[...]


def _out_proj(o, x, w_out, g, bias):
    b, s, d = x.shape
    tm = PROJ_TM
    tok = lambda bi, si: (bi, si, 0)
    const = lambda bi, si: (0, 0)
    return pl.pallas_call(
        _out_kernel,
        grid=(b, s // tm),
        in_specs=[pl.BlockSpec((None, tm, d), tok), pl.BlockSpec((None, tm, d), tok),
                  pl.BlockSpec(w_out.shape, const), pl.BlockSpec((1, d), const),
                  pl.BlockSpec((1, d), const)],
        out_specs=pl.BlockSpec((None, tm, d), tok),
        out_shape=jax.ShapeDtypeStruct((b, s, d), F32),
        compiler_params=_params(2),
        name="out_proj_ln",
    )(o, x, w_out, g, bias)


def _odd_proj_kernel(x_ref, wq_ref, wkv_ref, wg_ref, c_ref, s_ref,
                     q_ref, k_ref, v_ref, sg_ref):
    d = SWA_HEADS * SWA_DIM
    xb = x_ref[...].astype(BF16)
    ct = c_ref[...]
    st = s_ref[...]
    qq = _dot(xb, wq_ref[...])
    scale = SWA_DIM ** -0.5 * LOG2E
    for p in range(d // LANES):
        ps = slice(p * LANES, (p + 1) * LANES)
        sw = slice(d + p * LANES, d + (p + 1) * LANES)
        q_ref[:, ps] = ((qq[:, ps] * ct + qq[:, sw] * st) * scale).astype(BF16)
    kv = _dot(xb, wkv_ref[...])
    k_ref[...] = (kv[:, :LANES] * ct + kv[:, LANES:2 * LANES] * st).astype(BF16)
    v_ref[...] = kv[:, 2 * LANES:].astype(BF16)
    sg_ref[...] = _silu(_dot(xb, wg_ref[...])).astype(BF16)


def _odd_proj(x, w, tabs):
    b, s, d = x.shape
    tm = PROJ_TM
    const = lambda bi, si: (0, 0)
    tok = lambda bi, si: (bi, si, 0)
    tab = lambda bi, si: (si, 0)
    wspec = lambda a: pl.BlockSpec(a.shape, const)
    wide = jax.ShapeDtypeStruct((b, s, d), BF16)
    narrow = jax.ShapeDtypeStruct((b, s, LANES), BF16)
    return pl.pallas_call(
        _odd_proj_kernel,
        grid=(b, s // tm),
        in_specs=[pl.BlockSpec((None, tm, d), tok), wspec(w["wq"]), wspec(w["wkv"]), wspec(w["wg"]),
                  pl.BlockSpec((tm, LANES), tab), pl.BlockSpec((tm, LANES), tab)],
        out_specs=[pl.BlockSpec((None, tm, d), tok), pl.BlockSpec((None, tm, LANES), tok),
                   pl.BlockSpec((None, tm, LANES), tok), pl.BlockSpec((None, tm, d), tok)],
        out_shape=[wide, narrow, narrow, wide],
        compiler_params=_params(2),
        name="odd_proj",
    )(x, w["wq"], w["wkv"], w["wg"], tabs["c"], tabs["s"])


def _swa_kernel(sink_ref, q_ref, k_ref, v_ref, sg_ref, o_ref):
    n_blk = q_ref.shape[0] // QBLK
    n_pair = q_ref.shape[1] // LANES
    qi = pl.program_id(1)
    lane = lax.broadcasted_iota(jnp.int32, (QBLK, LANES), 1)
    lo_half = lane < SWA_DIM
    r = lax.broadcasted_iota(jnp.int32, (QBLK, 2 * QBLK), 0)
    c = lax.broadcasted_iota(jnp.int32, (QBLK, 2 * QBLK), 1)

    def block(blk, _):
        gb = qi * n_blk + blk
        q0 = pl.multiple_of(blk * QBLK, QBLK)
        k0 = pl.multiple_of(jnp.maximum(gb - 1, 0) * QBLK, QBLK)
        diff = (gb * QBLK + r) - (k0 + c)
        band = (diff >= 0) & (diff < WINDOW)
        kw = k_ref[pl.ds(k0, 2 * QBLK), :]
        vw = v_ref[pl.ds(k0, 2 * QBLK), :]
        qrows = []
        for p in range(n_pair):
            qp = q_ref[pl.ds(q0, QBLK), p * LANES:(p + 1) * LANES]
            zero = jnp.zeros_like(qp)
            qrows.append(jnp.where(lo_half, qp, zero))
            qrows.append(jnp.where(lo_half, zero, qp))
        s_all = _dot_nt(jnp.concatenate(qrows, axis=0), kw)
        ps = []
        inv = []
        for rb in range(2 * n_pair):
            sink = sink_ref[rb] * LOG2E
            s = jnp.where(band, s_all[rb * QBLK:(rb + 1) * QBLK, :], -jnp.inf)
            m = jnp.maximum(jnp.max(s, axis=1, keepdims=True), sink)
            p_ = jnp.exp2(s - m)
            inv.append(1.0 / (jnp.sum(p_, axis=1, keepdims=True) + jnp.exp2(sink - m)))
            ps.append(p_.astype(BF16))
        o_all = _dot(jnp.concatenate(ps, axis=0), vw)
        for p in range(n_pair):
            oa = o_all[(2 * p) * QBLK:(2 * p + 1) * QBLK, :] * inv[2 * p]
            ob = o_all[(2 * p + 1) * QBLK:(2 * p + 2) * QBLK, :] * inv[2 * p + 1]
            o = jnp.where(lo_half, oa, ob)
            cols = slice(p * LANES, (p + 1) * LANES)
            o_ref[pl.ds(q0, QBLK), cols] = (
                o * sg_ref[pl.ds(q0, QBLK), cols].astype(F32)).astype(BF16)
        return 0

    lax.fori_loop(0, n_blk, block, 0)


def _swa_attn(sinks, q, k, v, sg):
    b, s, d = q.shape
    tq = SWA_TQ
    tok = lambda bi, qi: (bi, qi, 0)
    full = lambda bi, qi: (bi, 0, 0)
    return pl.pallas_call(
        _swa_kernel,
        grid=(b, s // tq),
        in_specs=[pl.BlockSpec(memory_space=pltpu.SMEM),
                  pl.BlockSpec((None, tq, d), tok),
                  pl.BlockSpec((None, s, LANES), full), pl.BlockSpec((None, s, LANES), full),
                  pl.BlockSpec((None, tq, d), tok)],
        out_specs=pl.BlockSpec((None, tq, d), tok),
        out_shape=jax.ShapeDtypeStruct((b, s, d), BF16),
        compiler_params=_params(2),
        name="swa_attn",
    )(sinks, q, k, v, sg)


def _pad_heads(w, n_heads, dim, offsets):
    k = w.shape[0]
    n_off = len(offsets)
    w4 = w.reshape(k, n_heads // n_off, n_off, dim)
    parts = [jnp.pad(w4[:, :, i, :], ((0, 0), (0, 0), (off, LANES - off - dim)))
             for i, off in enumerate(offsets)]
    return jnp.stack(parts, axis=2).reshape(k, n_heads * LANES)


def _swap_halves(w, n_heads, dim):
    k = w.shape[0]
    w3 = w.reshape(k, n_heads, dim)
    return jnp.concatenate([w3[..., dim // 2:], w3[..., :dim // 2]], axis=-1).reshape(k, n_heads * dim)


def _even_weights(w_in, q_norm, w_uq, kv_norm, w_ukv, b_f):
    sizes = (MLA_Q_RANK, MLA_KV_RANK, MLA_ROPE, FOX_HEADS * FOX_DIM, FOX_HEADS * FOX_DIM,
             FOX_HEADS * FOX_DIM, FOX_HEADS, MLA_HEADS * MLA_V + FOX_HEADS * FOX_DIM)
    cuts = [int(c) for c in np.cumsum(sizes)[:-1]]
    w_cq, w_ckv, w_kpe, w_fq, w_fk, w_fv, w_f, w_g = jnp.split(w_in, cuts, axis=1)
    place = lambda w: _pad_heads(w, 1, MLA_ROPE, (MLA_NOPE,))
    wlat = jnp.concatenate([w_cq, w_ckv, place(w_kpe), place(_swap_halves(w_kpe, 1, MLA_ROPE)),
                            _pad_heads(w_f, 1, FOX_HEADS, (0,))], axis=1)
    v_off = (0, MLA_V)
    dq = MLA_NOPE + MLA_ROPE
    uq3 = w_uq.reshape(MLA_Q_RANK, MLA_HEADS, dq)
    uq_pe = uq3[..., MLA_NOPE:].reshape(MLA_Q_RANK, MLA_HEADS * MLA_ROPE)
    uq_a = _pad_heads(w_uq, MLA_HEADS, dq, (0,))
    uq_b = _pad_heads(_swap_halves(uq_pe, MLA_HEADS, MLA_ROPE), MLA_HEADS, MLA_ROPE, (MLA_NOPE,))
    ukv3 = w_ukv.reshape(MLA_KV_RANK, MLA_HEADS, MLA_NOPE + MLA_V)
    uk = ukv3[..., :MLA_NOPE].reshape(MLA_KV_RANK, MLA_HEADS * MLA_NOPE)
    uv = ukv3[..., MLA_NOPE:].reshape(MLA_KV_RANK, MLA_HEADS * MLA_V)
    bf = lambda a: a.astype(BF16)
    return dict(wlat=bf(wlat),
                wuqt=bf(jnp.concatenate([uq_a, uq_b], axis=1).T),
                wuk=bf(_pad_heads(uk, MLA_HEADS, MLA_NOPE, (0,))),
                wuvt=bf(_pad_heads(uv, MLA_HEADS, MLA_V, v_off).T),
                wfqt=bf(_pad_heads(w_fq, FOX_HEADS, FOX_DIM, (0,)).T),
                wfk=bf(_pad_heads(w_fk, FOX_HEADS, FOX_DIM, (0,))),
                wfvt=bf(_pad_heads(w_fv, FOX_HEADS, FOX_DIM, v_off).T),
                wg=bf(w_g), qn=q_norm.reshape(1, -1), kvn=kv_norm.reshape(1, -1),
                bf=jnp.pad(b_f, (0, LANES - FOX_HEADS)).reshape(1, LANES))


def _pair_heads(w):
    lead = w.shape[:-1]
    w4 = w.reshape(*lead, SWA_KV_HEADS, SWA_HEADS // SWA_KV_HEADS, -1)
    return jnp.swapaxes(w4, -3, -2).reshape(*lead, -1)


def _odd_weights(w_in, sinks, w_out):
    d = SWA_HEADS * SWA_DIM
    dkv = SWA_KV_HEADS * SWA_DIM
    w_q, w_k, w_v, w_g = jnp.split(w_in, [d, d + dkv, d + 2 * dkv], axis=1)
    wq = jnp.concatenate([_pair_heads(w_q), _pair_heads(_swap_halves(w_q, SWA_HEADS, SWA_DIM))], axis=1)
    wkv = jnp.concatenate([w_k, _swap_halves(w_k, SWA_KV_HEADS, SWA_DIM), w_v], axis=1)
    return dict(wq=wq.astype(BF16), wkv=wkv.astype(BF16), wg=_pair_heads(w_g).astype(BF16),
                sinks=_pair_heads(sinks.reshape(-1, 1).T).reshape(-1),
                w_out=_pair_heads(w_out.T).T.astype(BF16))


def _rope_tables(s):
    pos = jnp.arange(s).astype(F32)

    def cs(dim):
        inv = ROPE_THETA ** (-jnp.arange(0, dim, 2, dtype=F32) / dim)
        ang = pos[:, None] * inv[None, :]
        cos, sin = jnp.cos(ang), jnp.sin(ang)
        return jnp.concatenate([cos, cos], axis=1), jnp.concatenate([-sin, sin], axis=1)

    c32, s32 = cs(MLA_ROPE)
    z = lambda n: jnp.zeros((s, n), F32)
    pad = LANES - MLA_NOPE - MLA_ROPE
    scale = (MLA_NOPE + MLA_ROPE) ** -0.5 * LOG2E
    even = dict(cqt=(jnp.concatenate([jnp.ones((s, MLA_NOPE), F32), c32, z(pad)], axis=1) * scale).T,
                sqt=(jnp.concatenate([z(MLA_NOPE), s32, z(pad)], axis=1) * scale).T,
                ck=jnp.concatenate([z(MLA_NOPE), c32, z(pad)], axis=1),
                sk=jnp.concatenate([z(MLA_NOPE), s32, z(pad)], axis=1))
    c64, s64 = cs(SWA_DIM)
    odd = dict(c=jnp.concatenate([c64, c64], axis=1), s=jnp.concatenate([s64, s64], axis=1))
    return even, odd


def kernel(x, even_w_in, even_q_norm, even_w_uq, even_kv_norm, even_w_ukv, even_b_f, even_w_out,
           even_ln_g, even_ln_b, odd_w_in, odd_sinks, odd_w_out, odd_ln_g, odd_ln_b):
    s = x.shape[1]
    even_tabs, odd_tabs = _rope_tables(s)
    for layer in range(DEPTH):
        j = layer // 2
        if layer % 2 == 0:
            w = _even_weights(even_w_in[j], even_q_norm[j], even_w_uq[j], even_kv_norm[j],
                              even_w_ukv[j], even_b_f[j])
            qt, k, vt, sg = _even_proj(x, w, even_tabs)
            o = _causal_attn(qt, k, vt, sg)
            x = _out_proj(o, x, even_w_out[j].astype(BF16),
                          even_ln_g[j].reshape(1, -1), even_ln_b[j].reshape(1, -1))
        else:
            w = _odd_weights(odd_w_in[j], odd_sinks[j], odd_w_out[j])
            q, k, v, sg = _odd_proj(x, w, odd_tabs)
            o = _swa_attn(w["sinks"], q, k, v, sg)
            x = _out_proj(o, x, w["w_out"], odd_ln_g[j].reshape(1, -1), odd_ln_b[j].reshape(1, -1))
    return x
```

```python
import functools
import math

import jax
import jax.numpy as jnp
import numpy as np
from jax import lax
from jax.experimental import pallas as pl
from jax.experimental.pallas import tpu as pltpu

D_MODEL = 1024
DEPTH = 4
ROPE_THETA = 10000.0
QBLK = 128
MLA_HEADS = 8
MLA_NOPE = 64
MLA_ROPE = 32
MLA_V = 64
MLA_Q_RANK = 256
MLA_KV_RANK = 128
FOX_HEADS = 8
FOX_DIM = 64
SWA_HEADS = 16
SWA_KV_HEADS = 2
SWA_DIM = 64
WINDOW = 128
RMS_EPS = 1e-6
LN_EPS = 1e-5
ALPHA = (2 * DEPTH) ** 0.25
LOG2E = math.log2(math.e)

LANES = 128
VMEM_LIMIT = 56 * 1024 * 1024
BF16 = jnp.bfloat16
F32 = jnp.float32

PROJ_TM = 256
ATT_T = 512
ATT_HG = 4
SWA_TQ = 512
EVEN_HEADS = MLA_HEADS + FOX_HEADS
EVEN_HW = EVEN_HEADS * LANES
MLA_HW = MLA_HEADS * LANES


def _dot(a, b):
    return jnp.dot(a, b, preferred_element_type=F32)


def _dot_nt(a, b):
    return lax.dot_general(a, b, (((1,), (1,)), ((), ())), preferred_element_type=F32)


def _rms(t, g):
    return t * lax.rsqrt(jnp.mean(t * t, axis=-1, keepdims=True) + RMS_EPS) * g


def _silu(g):
    return g / (1.0 + jnp.exp(-g))


def _split3(v):
    hi = v.astype(BF16)
    r1 = v - hi.astype(F32)
    mid = r1.astype(BF16)
    lo = (r1 - mid.astype(F32)).astype(BF16)
    return hi, mid, lo


def _params(n_axes, flags=None):
    return pltpu.CompilerParams(dimension_semantics=("arbitrary",) * n_axes,
                                vmem_limit_bytes=VMEM_LIMIT, flags=flags)


def _even_proj_kernel(x_ref, wlat_ref, wuqt_ref, wuk_ref, wuvt_ref, wfqt_ref, wfk_ref, wfvt_ref,
                      wg_ref, qn_ref, kvn_ref, bf_ref, cqt_ref, sqt_ref, ck_ref, sk_ref,
                      qt_ref, k_ref, vt_ref, sg_ref, carry_ref):
    tm = x_ref.shape[0]
    xb = x_ref[...].astype(BF16)

    hrow = lax.broadcasted_iota(jnp.int32, (MLA_HW, 1), 0) % (2 * LANES)
    ones_col = jnp.where((hrow == MLA_V) | (hrow == LANES), 1.0, 0.0).astype(F32)
    lat = _dot(xb, wlat_ref[...])
    o = 0
    cq = lat[:, o:o + MLA_Q_RANK]
    o += MLA_Q_RANK
    ckv = lat[:, o:o + MLA_KV_RANK]
    o += MLA_KV_RANK
    kpe = lat[:, o:o + LANES] * ck_ref[...] + lat[:, o + LANES:o + 2 * LANES] * sk_ref[...]
    o += 2 * LANES
    fl = lat[:, o:o + LANES] + bf_ref[...]

    rq = _rms(cq, qn_ref[...]).astype(BF16)
    qa = _dot_nt(wuqt_ref[:MLA_HW, :], rq)
    qb = _dot_nt(wuqt_ref[MLA_HW:, :], rq)
    cqt = cqt_ref[...]
    sqt = sqt_ref[...]
    for h in range(MLA_HEADS):
        hs = slice(h * LANES, (h + 1) * LANES)
        qt_ref[hs, :] = (qa[hs, :] * cqt + qb[hs, :] * sqt).astype(BF16)
    rkv = _rms(ckv, kvn_ref[...]).astype(BF16)
    kk = _dot(rkv, wuk_ref[...])
    for h in range(MLA_HEADS):
        hs = slice(h * LANES, (h + 1) * LANES)
        k_ref[:, hs] = (kk[:, hs] + kpe).astype(BF16)
    vt_ref[:MLA_HW, :] = (_dot_nt(wuvt_ref[...], rkv) + ones_col).astype(BF16)

    lane = lax.broadcasted_iota(jnp.int32, (1, LANES), 1)
    lf = (jnp.minimum(fl, 0.0) - jnp.log1p(jnp.exp(-jnp.abs(fl)))) * LOG2E
    lf = jnp.where(lane < FOX_HEADS, lf, 0.0)
    row = lax.broadcasted_iota(jnp.int32, (tm, tm), 0)
    col = lax.broadcasted_iota(jnp.int32, (tm, tm), 1)
    tri = jnp.where(col <= row, 1.0, 0.0).astype(BF16)
    hi, mid, lo = _split3(lf)
    c_loc = _dot(tri, hi) + _dot(tri, mid) + _dot(tri, lo)

    @pl.when(pl.program_id(1) == 0)
    def _():
        carry_ref[...] = jnp.zeros_like(carry_ref)

    c = c_loc + carry_ref[0:1, :]
    carry_ref[...] = jnp.broadcast_to(c[tm - 1:tm, :], carry_ref.shape)
    src = lax.broadcasted_iota(jnp.int32, (LANES, MLA_HW), 0)
    dst = lax.broadcasted_iota(jnp.int32, (LANES, MLA_HW), 1)
    c_lane0 = src * LANES + jnp.where(src % 2 == 0, FOX_DIM, 0)
    k_extra = None
    for t, part in enumerate(_split3(c)):
        place = jnp.where((dst == c_lane0 + t) & (src < FOX_HEADS), 1.0, 0.0).astype(BF16)
        term = _dot(part, place)
        k_extra = term if k_extra is None else k_extra + term

    fqt = _dot_nt(wfqt_ref[...], xb) * (FOX_DIM ** -0.5 * LOG2E)
    fk = _dot(xb, wfk_ref[...])
    fvt = _dot_nt(wfvt_ref[...], xb)
    half_row = lax.broadcasted_iota(jnp.int32, (FOX_DIM, tm), 0)
    neg_blk = jnp.where(half_row < 3, -1.0, 0.0).astype(BF16)
    ones_blk = jnp.where(half_row < 1, 1.0, 0.0).astype(BF16)
    lane_k = lax.broadcasted_iota(jnp.int32, (1, LANES), 1)
    for h in range(FOX_HEADS):
        off = (h % 2) * FOX_DIM
        base = MLA_HW + h * LANES
        data = slice(base + off, base + off + FOX_DIM)
        rest = slice(base + FOX_DIM - off, base + 2 * FOX_DIM - off)
        rows = slice(h * FOX_DIM, (h + 1) * FOX_DIM)
        qt_ref[data, :] = fqt[rows, :].astype(BF16)
        qt_ref[rest, :] = neg_blk
        vt_ref[data, :] = fvt[rows, :].astype(BF16)
        vt_ref[rest, :] = ones_blk
        pair = slice((h // 2) * LANES, (h // 2 + 1) * LANES)
        in_data = (lane_k >= off) & (lane_k < off + FOX_DIM)
        k_ref[:, base:base + LANES] = jnp.where(
            in_data, fk[:, pair], k_extra[:, h * LANES:(h + 1) * LANES]).astype(BF16)

    sg_ref[...] = _silu(_dot(xb, wg_ref[...])).astype(BF16)


def _even_proj(x, w, tabs, layer):
    b, s, d = x.shape
    tm = PROJ_TM
    tok = lambda bi, si: (bi, si, 0)
    tok_t = lambda bi, si: (bi, 0, si)
    names = ("wlat", "wuqt", "wuk", "wuvt", "wfqt", "wfk", "wfvt", "wg", "qn", "kvn", "bf")
    in_specs = [pl.BlockSpec((None, tm, d), tok)] + [_layer_spec(w[k], layer) for k in names]
    in_specs += [pl.BlockSpec((LANES, tm), lambda bi, si: (0, si))] * 2
    in_specs += [pl.BlockSpec((tm, LANES), lambda bi, si: (si, 0))] * 2
    out_specs = [pl.BlockSpec((None, EVEN_HW, tm), tok_t), pl.BlockSpec((None, tm, EVEN_HW), tok),
                 pl.BlockSpec((None, EVEN_HW, tm), tok_t), pl.BlockSpec((None, tm, d), tok)]
    out_shape = [jax.ShapeDtypeStruct((b, EVEN_HW, s), BF16), jax.ShapeDtypeStruct((b, s, EVEN_HW), BF16),
                 jax.ShapeDtypeStruct((b, EVEN_HW, s), BF16), jax.ShapeDtypeStruct((b, s, d), BF16)]
    return pl.pallas_call(
        _even_proj_kernel,
        grid=(b, s // tm),
        in_specs=in_specs,
        out_specs=out_specs,
        out_shape=out_shape,
        scratch_shapes=[pltpu.VMEM((8, LANES), F32)],
        compiler_params=_params(2),
        name="even_proj",
    )(x, *[w[k] for k in names], tabs["cqt"], tabs["sqt"], tabs["ck"], tabs["sk"])


def _causal_attn_kernel(qt_ref, qtn_ref, k_ref, vt_ref, sg_ref, o_ref,
                        sa_ref, sb_ref, sc_ref, mc_ref):
    t = qt_ref.shape[1]
    n_heads = qt_ref.shape[0] // LANES
    qi = pl.program_id(2)
    heads = [slice(h * LANES, (h + 1) * LANES) for h in range(n_heads)]

    def key_slice(kj):
        return pl.ds(pl.multiple_of(kj * t, t), t)

    def col_max(s):
        return jnp.max(s, axis=0, keepdims=True)

    def causal_mask(s):
        keep = (lax.broadcasted_iota(jnp.int32, (t, t), 0) <= lax.broadcasted_iota(jnp.int32, (t, t), 1))
        return jnp.where(keep, s, -jnp.inf)

    def scores_head(q_ref, kj, buf, h, masked=False):
        s = _dot(k_ref[key_slice(kj), heads[h]], q_ref[heads[h], :])
        if masked:
            s = causal_mask(s)
        buf[h] = s
        return col_max(s)

    def process_head(kj, s, h, tile_max, m, acc):
        m_new = jnp.maximum(m, tile_max)
        p = jnp.exp2(s - m_new).astype(BF16)
        return m_new, jnp.exp2(m - m_new) * acc + _dot(vt_ref[heads[h], key_slice(kj)], p)

    def stage(kj, cur, nxt, tile_max, state, mask_next=False):
        next_max, out = [], []
        for h in range(n_heads):
            next_max.append(scores_head(qt_ref, kj + 1, nxt, h, mask_next))
            out.append(process_head(kj, cur[h], h, tile_max[h], *state[h]))
        return tuple(next_max), tuple(out)

    def two_stages(i, carry):
        tile_max, state = carry
        tile_max, state = stage(2 * i + 1, sa_ref, sb_ref, tile_max, state)
        return stage(2 * i + 2, sb_ref, sa_ref, tile_max, state)

    def prefetch_head(h):
        mc_ref[h, 0:1, :] = scores_head(qtn_ref, 0, sc_ref, h)

    def last_tile(buf, tile_max, state, masked, prefetch_first):
        out = []
        for h in range(n_heads):
            if prefetch_first:
                prefetch_head(h)
            s = buf[h]
            if masked:
                s = causal_mask(s)
            out.append(process_head(qi, s, h, col_max(s) if masked else tile_max[h], *state[h]))
            if not prefetch_first:
                prefetch_head(h)
        return tuple(out)

    @pl.when(qi == 0)
    def _():
        for h in range(n_heads):
            mc_ref[h, 0:1, :] = scores_head(qt_ref, 0, sc_ref, h)

    def only_tile(state):
        return last_tile(sc_ref, None, state, masked=True, prefetch_first=False)

    def several_tiles(state):
        tile_max = tuple(mc_ref[h, 0:1, :] for h in range(n_heads))
        tile_max, state = stage(0, sc_ref, sa_ref, tile_max, state)
        tile_max, state = lax.fori_loop(0, (qi - 1) // 2, two_stages, (tile_max, state))

        def odd_tail(tile_max, state):
            return last_tile(sa_ref, None, state, masked=True, prefetch_first=True)

        def even_tail(tile_max, state):
            tile_max, state = stage(qi - 1, sa_ref, sb_ref, tile_max, state, mask_next=True)
            return last_tile(sb_ref, tile_max, state, masked=False, prefetch_first=True)

        return lax.cond(qi % 2 == 1, odd_tail, even_tail, tile_max, state)

    init = tuple((jnp.full((1, t), -jnp.inf, F32), jnp.zeros((LANES, t), F32)) for _ in range(n_heads))
    state = lax.cond(qi == 0, only_tile, several_tiles, init)

    row = lax.broadcasted_iota(jnp.int32, (LANES, t), 0)
    for hp in range(n_heads // 2):
        a0 = state[2 * hp][1]
        a1 = state[2 * hp + 1][1]
        pair_t = jnp.where(row < MLA_V, a0 / a0[MLA_V:MLA_V + 1, :], a1 / a1[0:1, :])
        cols = slice(hp * LANES, (hp + 1) * LANES)
        o_ref[:, cols] = (pair_t.T * sg_ref[:, cols].astype(F32)).astype(BF16)


def _causal_attn(qt, k, vt, sg):
    b, s, hw = k.shape
    t = ATT_T
    nq = s // t
    gw = ATT_HG * LANES
    ow = ATT_HG * MLA_V
    score_buf = pltpu.VMEM((ATT_HG, t, t), F32)
    return pl.pallas_call(
        _causal_attn_kernel,
        grid=(b, hw // gw, nq),
        in_specs=[pl.BlockSpec((None, gw, t), lambda bi, g, qi: (bi, g, qi)),
                  pl.BlockSpec((None, gw, t), lambda bi, g, qi: (bi, g, jnp.minimum(qi + 1, nq - 1))),
                  pl.BlockSpec((None, s, gw), lambda bi, g, qi: (bi, 0, g)),
                  pl.BlockSpec((None, gw, s), lambda bi, g, qi: (bi, g, 0)),
                  pl.BlockSpec((None, t, ow), lambda bi, g, qi: (bi, qi, g))],
        out_specs=pl.BlockSpec((None, t, ow), lambda bi, g, qi: (bi, qi, g)),
        out_shape=jax.ShapeDtypeStruct((b, s, (hw // LANES) * MLA_V), BF16),
        scratch_shapes=[score_buf, score_buf, score_buf, pltpu.VMEM((ATT_HG, 8, t), F32)],
        compiler_params=_params(3),
        name="causal_attn",
    )(qt, qt, k, vt, sg)


def _out_kernel(o_ref, x_ref, w_ref, g_ref, b_ref, y_ref):
    z = ALPHA * x_ref[...] + _dot(o_ref[...], w_ref[...])
    mu = jnp.mean(z, axis=-1, keepdims=True)
    zc = z - mu
    var = jnp.mean(zc * zc, axis=-1, keepdims=True)
    y_ref[...] = zc * lax.rsqrt(var + LN_EPS) * g_ref[...] + b_ref[...]


def _out_proj(o, x, w_out, g, bias, layer):
    b, s, d = x.shape
    tm = PROJ_TM
    tok = lambda bi, si: (bi, si, 0)
    return pl.pallas_call(
        _out_kernel,
        grid=(b, s // tm),
        in_specs=[pl.BlockSpec((None, tm, d), tok), pl.BlockSpec((None, tm, d), tok),
                  _layer_spec(w_out, layer), _layer_spec(g, layer), _layer_spec(bias, layer)],
        out_specs=pl.BlockSpec((None, tm, d), tok),
        out_shape=jax.ShapeDtypeStruct((b, s, d), F32),
        compiler_params=_params(2),
        name="out_proj_ln",
    )(o, x, w_out, g, bias)


def _odd_proj_kernel(x_ref, wq_ref, wkv_ref, wg_ref, c_ref, s_ref,
                     q_ref, k_ref, v_ref, sg_ref):
    d = SWA_HEADS * SWA_DIM
    xb = x_ref[...].astype(BF16)
    ct = c_ref[...]
    st = s_ref[...]
    lane = lax.broadcasted_iota(jnp.int32, (1, LANES), 1)
    first_half = (lane % SWA_DIM) < SWA_DIM // 2

    def rope(t):
        half = SWA_DIM // 2
        swapped = jnp.where(first_half, pltpu.roll(t, LANES - half, 1), pltpu.roll(t, half, 1))
        return t * ct + swapped * st

    qq = _dot(xb, wq_ref[...])
    scale = SWA_DIM ** -0.5 * LOG2E
    for p in range(d // LANES):
        ps = slice(p * LANES, (p + 1) * LANES)
        q_ref[:, ps] = (rope(qq[:, ps]) * scale).astype(BF16)
    kv = _dot(xb, wkv_ref[...])
    k_ref[...] = rope(kv[:, :LANES]).astype(BF16)
    v_ref[...] = kv[:, LANES:].astype(BF16)
    sg_ref[...] = _silu(_dot(xb, wg_ref[...])).astype(BF16)


def _layer_spec(a, layer):
    return pl.BlockSpec((None,) + a.shape[1:], lambda bi, si: (layer,) + (0,) * (a.ndim - 1))


def _odd_proj(x, w, tabs, layer):
    b, s, d = x.shape
    tm = PROJ_TM
    tok = lambda bi, si: (bi, si, 0)
    tab = lambda bi, si: (si, 0)
    wide = jax.ShapeDtypeStruct((b, s, d), BF16)
    narrow = jax.ShapeDtypeStruct((b, s, LANES), BF16)
    names = ("wq", "wkv", "wg")
    return pl.pallas_call(
        _odd_proj_kernel,
        grid=(b, s // tm),
        in_specs=[pl.BlockSpec((None, tm, d), tok)] + [_layer_spec(w[k], layer) for k in names]
        + [pl.BlockSpec((tm, LANES), tab), pl.BlockSpec((tm, LANES), tab)],
        out_specs=[pl.BlockSpec((None, tm, d), tok), pl.BlockSpec((None, tm, LANES), tok),
                   pl.BlockSpec((None, tm, LANES), tok), pl.BlockSpec((None, tm, d), tok)],
        out_shape=[wide, narrow, narrow, wide],
        compiler_params=_params(2),
        name="odd_proj",
    )(x, *[w[k] for k in names], tabs["c"], tabs["s"])


def _swa_kernel(sink_ref, q_ref, k_ref, v_ref, sg_ref, o_ref):
    n_blk = q_ref.shape[0] // QBLK
    n_pair = q_ref.shape[1] // LANES
    qi = pl.program_id(1)
    lane = lax.broadcasted_iota(jnp.int32, (QBLK, LANES), 1)
    lo_half = lane < SWA_DIM
    r = lax.broadcasted_iota(jnp.int32, (QBLK, 2 * QBLK), 0)
    c = lax.broadcasted_iota(jnp.int32, (QBLK, 2 * QBLK), 1)

    def block(blk, _):
        gb = qi * n_blk + blk
        q0 = pl.multiple_of(blk * QBLK, QBLK)
        k0 = pl.multiple_of(jnp.maximum(gb - 1, 0) * QBLK, QBLK)
        diff = (gb * QBLK + r) - (k0 + c)
        band = (diff >= 0) & (diff < WINDOW)
        kw = k_ref[pl.ds(k0, 2 * QBLK), :]
        vw = v_ref[pl.ds(k0, 2 * QBLK), :]
        qrows = []
        for p in range(n_pair):
            qp = q_ref[pl.ds(q0, QBLK), p * LANES:(p + 1) * LANES]
            zero = jnp.zeros_like(qp)
            qrows.append(jnp.where(lo_half, qp, zero))
            qrows.append(jnp.where(lo_half, zero, qp))
        s_all = _dot_nt(jnp.concatenate(qrows, axis=0), kw)
        ps = []
        inv = []
        for rb in range(2 * n_pair):
            sink = sink_ref[rb] * LOG2E
            s = jnp.where(band, s_all[rb * QBLK:(rb + 1) * QBLK, :], -jnp.inf)
            m = jnp.maximum(jnp.max(s, axis=1, keepdims=True), sink)
            p_ = jnp.exp2(s - m)
            inv.append(1.0 / (jnp.sum(p_, axis=1, keepdims=True) + jnp.exp2(sink - m)))
            ps.append(p_.astype(BF16))
        o_all = _dot(jnp.concatenate(ps, axis=0), vw)
        for p in range(n_pair):
            oa = o_all[(2 * p) * QBLK:(2 * p + 1) * QBLK, :] * inv[2 * p]
            ob = o_all[(2 * p + 1) * QBLK:(2 * p + 2) * QBLK, :] * inv[2 * p + 1]
            o = jnp.where(lo_half, oa, ob)
            cols = slice(p * LANES, (p + 1) * LANES)
            o_ref[pl.ds(q0, QBLK), cols] = (
                o * sg_ref[pl.ds(q0, QBLK), cols].astype(F32)).astype(BF16)
        return 0

    lax.fori_loop(0, n_blk, block, 0)


def _swa_attn(sinks, q, k, v, sg):
    b, s, d = q.shape
    tq = SWA_TQ
    tok = lambda bi, qi: (bi, qi, 0)
    full = lambda bi, qi: (bi, 0, 0)
    return pl.pallas_call(
        _swa_kernel,
        grid=(b, s // tq),
        in_specs=[pl.BlockSpec(memory_space=pltpu.SMEM),
                  pl.BlockSpec((None, tq, d), tok),
                  pl.BlockSpec((None, s, LANES), full), pl.BlockSpec((None, s, LANES), full),
                  pl.BlockSpec((None, tq, d), tok)],
        out_specs=pl.BlockSpec((None, tq, d), tok),
        out_shape=jax.ShapeDtypeStruct((b, s, d), BF16),
        compiler_params=_params(2),
        name="swa_attn",
    )(sinks, q, k, v, sg)


def _pad_heads(w, n_heads, dim, offsets):
    k = w.shape[0]
    n_off = len(offsets)
    w4 = w.reshape(k, n_heads // n_off, n_off, dim)
    parts = [jnp.pad(w4[:, :, i, :], ((0, 0), (0, 0), (off, LANES - off - dim)))
             for i, off in enumerate(offsets)]
    return jnp.stack(parts, axis=2).reshape(k, n_heads * LANES)


def _swap_halves(w, n_heads, dim):
    k = w.shape[0]
    w3 = w.reshape(k, n_heads, dim)
    return jnp.concatenate([w3[..., dim // 2:], w3[..., :dim // 2]], axis=-1).reshape(k, n_heads * dim)


def _even_weights(w_in, q_norm, w_uq, kv_norm, w_ukv, b_f):
    sizes = (MLA_Q_RANK, MLA_KV_RANK, MLA_ROPE, FOX_HEADS * FOX_DIM, FOX_HEADS * FOX_DIM,
             FOX_HEADS * FOX_DIM, FOX_HEADS, MLA_HEADS * MLA_V + FOX_HEADS * FOX_DIM)
    cuts = [int(c) for c in np.cumsum(sizes)[:-1]]
    w_cq, w_ckv, w_kpe, w_fq, w_fk, w_fv, w_f, w_g = jnp.split(w_in, cuts, axis=1)
    place = lambda w: _pad_heads(w, 1, MLA_ROPE, (MLA_NOPE,))
    wlat = jnp.concatenate([w_cq, w_ckv, place(w_kpe), place(_swap_halves(w_kpe, 1, MLA_ROPE)),
                            _pad_heads(w_f, 1, FOX_HEADS, (0,))], axis=1)
    v_off = (0, MLA_V)
    dq = MLA_NOPE + MLA_ROPE
    uq3 = w_uq.reshape(MLA_Q_RANK, MLA_HEADS, dq)
    uq_pe = uq3[..., MLA_NOPE:].reshape(MLA_Q_RANK, MLA_HEADS * MLA_ROPE)
    uq_a = _pad_heads(w_uq, MLA_HEADS, dq, (0,))
    uq_b = _pad_heads(_swap_halves(uq_pe, MLA_HEADS, MLA_ROPE), MLA_HEADS, MLA_ROPE, (MLA_NOPE,))
    ukv3 = w_ukv.reshape(MLA_KV_RANK, MLA_HEADS, MLA_NOPE + MLA_V)
    uk = ukv3[..., :MLA_NOPE].reshape(MLA_KV_RANK, MLA_HEADS * MLA_NOPE)
    uv = ukv3[..., MLA_NOPE:].reshape(MLA_KV_RANK, MLA_HEADS * MLA_V)
    bf = lambda a: a.astype(BF16)
    return dict(wlat=bf(wlat),
                wuqt=bf(jnp.concatenate([uq_a, uq_b], axis=1).T),
                wuk=bf(_pad_heads(uk, MLA_HEADS, MLA_NOPE, (0,))),
                wuvt=bf(_pad_heads(uv, MLA_HEADS, MLA_V, v_off).T),
                wfqt=bf(w_fq.T), wfk=bf(w_fk), wfvt=bf(w_fv.T),
                wg=bf(w_g), qn=q_norm.reshape(1, -1), kvn=kv_norm.reshape(1, -1),
                bf=jnp.pad(b_f, (0, LANES - FOX_HEADS)).reshape(1, LANES))


def _pair_heads(w):
    lead = w.shape[:-1]
    w4 = w.reshape(*lead, SWA_KV_HEADS, SWA_HEADS // SWA_KV_HEADS, -1)
    return jnp.swapaxes(w4, -3, -2).reshape(*lead, -1)


def _odd_weights(w_in, sinks, w_out):
    d = SWA_HEADS * SWA_DIM
    dkv = SWA_KV_HEADS * SWA_DIM
    w_q, w_k, w_v, w_g = jnp.split(w_in, [d, d + dkv, d + 2 * dkv], axis=1)
    wkv = jnp.concatenate([w_k, w_v], axis=1)
    return dict(wq=_pair_heads(w_q).astype(BF16), wkv=wkv.astype(BF16), wg=_pair_heads(w_g).astype(BF16),
                sinks=_pair_heads(sinks.reshape(-1, 1).T).reshape(-1),
                w_out=_pair_heads(w_out.T).T.astype(BF16))


def _rope_tables(s):
    pos = jnp.arange(s).astype(F32)

    def cs(dim):
        inv = ROPE_THETA ** (-jnp.arange(0, dim, 2, dtype=F32) / dim)
        ang = pos[:, None] * inv[None, :]
        cos, sin = jnp.cos(ang), jnp.sin(ang)
        return jnp.concatenate([cos, cos], axis=1), jnp.concatenate([-sin, sin], axis=1)

    c32, s32 = cs(MLA_ROPE)
    z = lambda n: jnp.zeros((s, n), F32)
    pad = LANES - MLA_NOPE - MLA_ROPE
    scale = (MLA_NOPE + MLA_ROPE) ** -0.5 * LOG2E
    even = dict(cqt=(jnp.concatenate([jnp.ones((s, MLA_NOPE), F32), c32, z(pad)], axis=1) * scale).T,
                sqt=(jnp.concatenate([z(MLA_NOPE), s32, z(pad)], axis=1) * scale).T,
                ck=jnp.concatenate([z(MLA_NOPE), c32, z(pad)], axis=1),
                sk=jnp.concatenate([z(MLA_NOPE), s32, z(pad)], axis=1))
    c64, s64 = cs(SWA_DIM)
    odd = dict(c=jnp.concatenate([c64, c64], axis=1), s=jnp.concatenate([s64, s64], axis=1))
    return even, odd


def kernel(x, even_w_in, even_q_norm, even_w_uq, even_kv_norm, even_w_ukv, even_b_f, even_w_out,
           even_ln_g, even_ln_b, odd_w_in, odd_sinks, odd_w_out, odd_ln_g, odd_ln_b):
    s = x.shape[1]
    even_tabs, odd_tabs = _rope_tables(s)
    we = jax.vmap(_even_weights)(even_w_in, even_q_norm, even_w_uq, even_kv_norm, even_w_ukv, even_b_f)
    wo = jax.vmap(_odd_weights)(odd_w_in, odd_sinks, odd_w_out)
    even_w_out = even_w_out.astype(BF16)
    row = lambda a: a[:, None, :]
    for layer in range(DEPTH):
        j = layer // 2
        if layer % 2 == 0:
            qt, k, vt, sg = _even_proj(x, we, even_tabs, j)
            o = _causal_attn(qt, k, vt, sg)
            x = _out_proj(o, x, even_w_out, row(even_ln_g), row(even_ln_b), j)
        else:
            q, k, v, sg = _odd_proj(x, wo, odd_tabs, j)
            o = _swa_attn(wo["sinks"][j], q, k, v, sg)
            x = _out_proj(o, x, wo["w_out"], row(odd_ln_g), row(odd_ln_b), j)
    return x
```

```python
import functools
import math

import jax
import jax.numpy as jnp
import numpy as np
from jax import lax
from jax.experimental import pallas as pl
from jax.experimental.pallas import tpu as pltpu

D_MODEL = 1024
DEPTH = 4
ROPE_THETA = 10000.0
QBLK = 128
MLA_HEADS = 8
MLA_NOPE = 64
MLA_ROPE = 32
MLA_V = 64
MLA_Q_RANK = 256
MLA_KV_RANK = 128
FOX_HEADS = 8
FOX_DIM = 64
SWA_HEADS = 16
SWA_KV_HEADS = 2
SWA_DIM = 64
WINDOW = 128
RMS_EPS = 1e-6
LN_EPS = 1e-5
ALPHA = (2 * DEPTH) ** 0.25
LOG2E = math.log2(math.e)

LANES = 128
VMEM_LIMIT = 56 * 1024 * 1024
BF16 = jnp.bfloat16
F32 = jnp.float32

PROJ_TM = 256
ATT_T = 512
ATT_HG = 4
SWA_TQ = 512
EVEN_HEADS = MLA_HEADS + FOX_HEADS
EVEN_HW = EVEN_HEADS * LANES
MLA_HW = MLA_HEADS * LANES


def _dot(a, b):
    return jnp.dot(a, b, preferred_element_type=F32)


def _dot_nt(a, b):
    return lax.dot_general(a, b, (((1,), (1,)), ((), ())), preferred_element_type=F32)


def _rms(t, g):
    return t * lax.rsqrt(jnp.mean(t * t, axis=-1, keepdims=True) + RMS_EPS) * g


def _silu(g):
    return g / (1.0 + jnp.exp(-g))


def _split3(v):
    hi = v.astype(BF16)
    r1 = v - hi.astype(F32)
    mid = r1.astype(BF16)
    lo = (r1 - mid.astype(F32)).astype(BF16)
    return hi, mid, lo


def _params(n_axes, flags=None):
    return pltpu.CompilerParams(dimension_semantics=("arbitrary",) * n_axes,
                                vmem_limit_bytes=VMEM_LIMIT, flags=flags)


def _residual_norm(o_ref, x_ref, w_ref, g_ref, b_ref):
    z = ALPHA * x_ref[...] + _dot(o_ref[...], w_ref[...])
    mu = jnp.mean(z, axis=-1, keepdims=True)
    zc = z - mu
    var = jnp.mean(zc * zc, axis=-1, keepdims=True)
    return zc * lax.rsqrt(var + LN_EPS) * g_ref[...] + b_ref[...]


def _layer_input(refs, fused):
    if not fused:
        return refs[0][...], refs[1:]
    x = _residual_norm(*refs[:5])
    return x, refs[5:]


def _even_proj_kernel(*refs, fused):
    x, refs = _layer_input(refs, fused)
    (wlat_ref, wuqt_ref, wuk_ref, wuvt_ref, wfqt_ref, wfk_ref, wfvt_ref, wg_ref, qn_ref, kvn_ref,
     bf_ref, cqt_ref, sqt_ref, ck_ref, sk_ref, qt_ref, k_ref, vt_ref, sg_ref) = refs[:19]
    carry_ref = refs[-1]
    if fused:
        refs[19][...] = x
    tm = x.shape[0]
    xb = x.astype(BF16)

    hrow = lax.broadcasted_iota(jnp.int32, (MLA_HW, 1), 0) % (2 * LANES)
    ones_col = jnp.where((hrow == MLA_V) | (hrow == LANES), 1.0, 0.0).astype(F32)
    lat = _dot(xb, wlat_ref[...])
    o = 0
    cq = lat[:, o:o + MLA_Q_RANK]
    o += MLA_Q_RANK
    ckv = lat[:, o:o + MLA_KV_RANK]
    o += MLA_KV_RANK
    kpe = lat[:, o:o + LANES] * ck_ref[...] + lat[:, o + LANES:o + 2 * LANES] * sk_ref[...]
    o += 2 * LANES
    fl = lat[:, o:o + LANES] + bf_ref[...]

    rq = _rms(cq, qn_ref[...]).astype(BF16)
    qa = _dot_nt(wuqt_ref[:MLA_HW, :], rq)
    qb = _dot_nt(wuqt_ref[MLA_HW:, :], rq)
    cqt = cqt_ref[...]
    sqt = sqt_ref[...]
    for h in range(MLA_HEADS):
        hs = slice(h * LANES, (h + 1) * LANES)
        qt_ref[hs, :] = (qa[hs, :] * cqt + qb[hs, :] * sqt).astype(BF16)
    rkv = _rms(ckv, kvn_ref[...]).astype(BF16)
    kk = _dot(rkv, wuk_ref[...])
    for h in range(MLA_HEADS):
        hs = slice(h * LANES, (h + 1) * LANES)
        k_ref[:, hs] = (kk[:, hs] + kpe).astype(BF16)
    vt_ref[:MLA_HW, :] = (_dot_nt(wuvt_ref[...], rkv) + ones_col).astype(BF16)

    lane = lax.broadcasted_iota(jnp.int32, (1, LANES), 1)
    lf = (jnp.minimum(fl, 0.0) - jnp.log1p(jnp.exp(-jnp.abs(fl)))) * LOG2E
    lf = jnp.where(lane < FOX_HEADS, lf, 0.0)
    row = lax.broadcasted_iota(jnp.int32, (tm, tm), 0)
    col = lax.broadcasted_iota(jnp.int32, (tm, tm), 1)
    tri = jnp.where(col <= row, 1.0, 0.0).astype(BF16)
    hi, mid, lo = _split3(lf)
    c_loc = _dot(tri, hi) + _dot(tri, mid) + _dot(tri, lo)

    @pl.when(pl.program_id(1) == 0)
    def _():
        carry_ref[...] = jnp.zeros_like(carry_ref)

    c = c_loc + carry_ref[0:1, :]
    carry_ref[...] = jnp.broadcast_to(c[tm - 1:tm, :], carry_ref.shape)
    src = lax.broadcasted_iota(jnp.int32, (LANES, MLA_HW), 0)
    dst = lax.broadcasted_iota(jnp.int32, (LANES, MLA_HW), 1)
    c_lane0 = src * LANES + jnp.where(src % 2 == 0, FOX_DIM, 0)
    k_extra = None
    for t, part in enumerate(_split3(c)):
        place = jnp.where((dst == c_lane0 + t) & (src < FOX_HEADS), 1.0, 0.0).astype(BF16)
        term = _dot(part, place)
        k_extra = term if k_extra is None else k_extra + term

    fqt = _dot_nt(wfqt_ref[...], xb) * (FOX_DIM ** -0.5 * LOG2E)
    fk = _dot(xb, wfk_ref[...])
    fvt = _dot_nt(wfvt_ref[...], xb)
    half_row = lax.broadcasted_iota(jnp.int32, (FOX_DIM, tm), 0)
    neg_blk = jnp.where(half_row < 3, -1.0, 0.0).astype(BF16)
    ones_blk = jnp.where(half_row < 1, 1.0, 0.0).astype(BF16)
    lane_k = lax.broadcasted_iota(jnp.int32, (1, LANES), 1)
    for h in range(FOX_HEADS):
        off = (h % 2) * FOX_DIM
        base = MLA_HW + h * LANES
        data = slice(base + off, base + off + FOX_DIM)
        rest = slice(base + FOX_DIM - off, base + 2 * FOX_DIM - off)
        rows = slice(h * FOX_DIM, (h + 1) * FOX_DIM)
        qt_ref[data, :] = fqt[rows, :].astype(BF16)
        qt_ref[rest, :] = neg_blk
        vt_ref[data, :] = fvt[rows, :].astype(BF16)
        vt_ref[rest, :] = ones_blk
        pair = slice((h // 2) * LANES, (h // 2 + 1) * LANES)
        in_data = (lane_k >= off) & (lane_k < off + FOX_DIM)
        k_ref[:, base:base + LANES] = jnp.where(
            in_data, fk[:, pair], k_extra[:, h * LANES:(h + 1) * LANES]).astype(BF16)

    sg_ref[...] = _silu(_dot(xb, wg_ref[...])).astype(BF16)


def _input_specs(x, prev):
    b, s, d = x.shape
    tok = pl.BlockSpec((None, PROJ_TM, d), lambda bi, si: (bi, si, 0))
    if prev is None:
        return [x], [tok], [], []
    o, w_out, g, bias, layer = prev
    specs = [tok, tok] + [_layer_spec(a, layer) for a in (w_out, g, bias)]
    return [o, x, w_out, g, bias], specs, [tok], [jax.ShapeDtypeStruct((b, s, d), F32)]


def _even_proj(x, w, tabs, layer, prev=None):
    b, s, d = x.shape
    tm = PROJ_TM
    tok = lambda bi, si: (bi, si, 0)
    tok_t = lambda bi, si: (bi, 0, si)
    names = ("wlat", "wuqt", "wuk", "wuvt", "wfqt", "wfk", "wfvt", "wg", "qn", "kvn", "bf")
    operands, in_specs, x_spec, x_shape = _input_specs(x, prev)
    in_specs += [_layer_spec(w[k], layer) for k in names]
    in_specs += [pl.BlockSpec((LANES, tm), lambda bi, si: (0, si))] * 2
    in_specs += [pl.BlockSpec((tm, LANES), lambda bi, si: (si, 0))] * 2
    out_specs = [pl.BlockSpec((None, EVEN_HW, tm), tok_t), pl.BlockSpec((None, tm, EVEN_HW), tok),
                 pl.BlockSpec((None, EVEN_HW, tm), tok_t), pl.BlockSpec((None, tm, d), tok)]
    out_shape = [jax.ShapeDtypeStruct((b, EVEN_HW, s), BF16), jax.ShapeDtypeStruct((b, s, EVEN_HW), BF16),
                 jax.ShapeDtypeStruct((b, EVEN_HW, s), BF16), jax.ShapeDtypeStruct((b, s, d), BF16)]
    return pl.pallas_call(
        functools.partial(_even_proj_kernel, fused=prev is not None),
        grid=(b, s // tm),
        in_specs=in_specs,
        out_specs=out_specs + x_spec,
        out_shape=out_shape + x_shape,
        scratch_shapes=[pltpu.VMEM((8, LANES), F32)],
        compiler_params=_params(2),
        name="even_proj",
    )(*operands, *[w[k] for k in names], tabs["cqt"], tabs["sqt"], tabs["ck"], tabs["sk"])


def _causal_attn_kernel(qt_ref, qtn_ref, k_ref, vt_ref, sg_ref, o_ref,
                        sa_ref, sb_ref, sc_ref, mc_ref):
    t = qt_ref.shape[1]
    n_heads = qt_ref.shape[0] // LANES
    qi = pl.program_id(2)
    heads = [slice(h * LANES, (h + 1) * LANES) for h in range(n_heads)]

    def key_slice(kj):
        return pl.ds(pl.multiple_of(kj * t, t), t)

    def col_max(s):
        return jnp.max(s, axis=0, keepdims=True)

    def causal_mask(s):
        keep = (lax.broadcasted_iota(jnp.int32, (t, t), 0) <= lax.broadcasted_iota(jnp.int32, (t, t), 1))
        return jnp.where(keep, s, -jnp.inf)

    def scores_head(q_ref, kj, buf, h, masked=False):
        s = _dot(k_ref[key_slice(kj), heads[h]], q_ref[heads[h], :])
        if masked:
            s = causal_mask(s)
        buf[h] = s
        return col_max(s)

    def process_head(kj, s, h, tile_max, m, acc):
        m_new = jnp.maximum(m, tile_max)
        p = jnp.exp2(s - m_new).astype(BF16)
        return m_new, jnp.exp2(m - m_new) * acc + _dot(vt_ref[heads[h], key_slice(kj)], p)

    def stage(kj, cur, nxt, tile_max, state, mask_next=False):
        next_max, out = [], []
        for h in range(n_heads):
            next_max.append(scores_head(qt_ref, kj + 1, nxt, h, mask_next))
            out.append(process_head(kj, cur[h], h, tile_max[h], *state[h]))
        return tuple(next_max), tuple(out)

    def two_stages(i, carry):
        tile_max, state = carry
        tile_max, state = stage(2 * i + 1, sa_ref, sb_ref, tile_max, state)
        return stage(2 * i + 2, sb_ref, sa_ref, tile_max, state)

    def prefetch_head(h):
        mc_ref[h, 0:1, :] = scores_head(qtn_ref, 0, sc_ref, h)

    def last_tile(buf, tile_max, state, masked, prefetch_first):
        out = []
        for h in range(n_heads):
            if prefetch_first:
                prefetch_head(h)
            s = buf[h]
            if masked:
                s = causal_mask(s)
            out.append(process_head(qi, s, h, col_max(s) if masked else tile_max[h], *state[h]))
            if not prefetch_first:
                prefetch_head(h)
        return tuple(out)

    @pl.when(qi == 0)
    def _():
        for h in range(n_heads):
            mc_ref[h, 0:1, :] = scores_head(qt_ref, 0, sc_ref, h)

    def only_tile(state):
        return last_tile(sc_ref, None, state, masked=True, prefetch_first=False)

    def several_tiles(state):
        tile_max = tuple(mc_ref[h, 0:1, :] for h in range(n_heads))
        tile_max, state = stage(0, sc_ref, sa_ref, tile_max, state)
        tile_max, state = lax.fori_loop(0, (qi - 1) // 2, two_stages, (tile_max, state))

        def odd_tail(tile_max, state):
            return last_tile(sa_ref, None, state, masked=True, prefetch_first=True)

        def even_tail(tile_max, state):
            tile_max, state = stage(qi - 1, sa_ref, sb_ref, tile_max, state, mask_next=True)
            return last_tile(sb_ref, tile_max, state, masked=False, prefetch_first=True)

        return lax.cond(qi % 2 == 1, odd_tail, even_tail, tile_max, state)

    init = tuple((jnp.full((1, t), -jnp.inf, F32), jnp.zeros((LANES, t), F32)) for _ in range(n_heads))
    state = lax.cond(qi == 0, only_tile, several_tiles, init)

    row = lax.broadcasted_iota(jnp.int32, (LANES, t), 0)
    for hp in range(n_heads // 2):
        a0 = state[2 * hp][1]
        a1 = state[2 * hp + 1][1]
        pair_t = jnp.where(row < MLA_V, a0 / a0[MLA_V:MLA_V + 1, :], a1 / a1[0:1, :])
        cols = slice(hp * LANES, (hp + 1) * LANES)
        o_ref[:, cols] = (pair_t.T * sg_ref[:, cols].astype(F32)).astype(BF16)


def _causal_attn(qt, k, vt, sg):
    b, s, hw = k.shape
    t = ATT_T
    nq = s // t
    gw = ATT_HG * LANES
    ow = ATT_HG * MLA_V
    score_buf = pltpu.VMEM((ATT_HG, t, t), F32)
    return pl.pallas_call(
        _causal_attn_kernel,
        grid=(b, hw // gw, nq),
        in_specs=[pl.BlockSpec((None, gw, t), lambda bi, g, qi: (bi, g, qi)),
                  pl.BlockSpec((None, gw, t), lambda bi, g, qi: (bi, g, jnp.minimum(qi + 1, nq - 1))),
                  pl.BlockSpec((None, s, gw), lambda bi, g, qi: (bi, 0, g)),
                  pl.BlockSpec((None, gw, s), lambda bi, g, qi: (bi, g, 0)),
                  pl.BlockSpec((None, t, ow), lambda bi, g, qi: (bi, qi, g))],
        out_specs=pl.BlockSpec((None, t, ow), lambda bi, g, qi: (bi, qi, g)),
        out_shape=jax.ShapeDtypeStruct((b, s, (hw // LANES) * MLA_V), BF16),
        scratch_shapes=[score_buf, score_buf, score_buf, pltpu.VMEM((ATT_HG, 8, t), F32)],
        compiler_params=_params(3),
        name="causal_attn",
    )(qt, qt, k, vt, sg)


def _out_kernel(o_ref, x_ref, w_ref, g_ref, b_ref, y_ref):
    y_ref[...] = _residual_norm(o_ref, x_ref, w_ref, g_ref, b_ref)


def _out_proj(o, x, w_out, g, bias, layer):
    b, s, d = x.shape
    tm = PROJ_TM
    tok = lambda bi, si: (bi, si, 0)
    return pl.pallas_call(
        _out_kernel,
        grid=(b, s // tm),
        in_specs=[pl.BlockSpec((None, tm, d), tok), pl.BlockSpec((None, tm, d), tok),
                  _layer_spec(w_out, layer), _layer_spec(g, layer), _layer_spec(bias, layer)],
        out_specs=pl.BlockSpec((None, tm, d), tok),
        out_shape=jax.ShapeDtypeStruct((b, s, d), F32),
        compiler_params=_params(2),
        name="out_proj_ln",
    )(o, x, w_out, g, bias)


def _odd_proj_kernel(*refs, fused):
    x, refs = _layer_input(refs, fused)
    wq_ref, wkv_ref, wg_ref, c_ref, s_ref, q_ref, k_ref, v_ref, sg_ref = refs[:9]
    if fused:
        refs[9][...] = x
    d = SWA_HEADS * SWA_DIM
    xb = x.astype(BF16)
    ct = c_ref[...]
    st = s_ref[...]
    lane = lax.broadcasted_iota(jnp.int32, (1, LANES), 1)
    first_half = (lane % SWA_DIM) < SWA_DIM // 2

    def rope(t):
        half = SWA_DIM // 2
        swapped = jnp.where(first_half, pltpu.roll(t, LANES - half, 1), pltpu.roll(t, half, 1))
        return t * ct + swapped * st

    qq = _dot(xb, wq_ref[...])
    scale = SWA_DIM ** -0.5 * LOG2E
    for p in range(d // LANES):
        ps = slice(p * LANES, (p + 1) * LANES)
        q_ref[:, ps] = (rope(qq[:, ps]) * scale).astype(BF16)
    kv = _dot(xb, wkv_ref[...])
    k_ref[...] = rope(kv[:, :LANES]).astype(BF16)
    v_ref[...] = kv[:, LANES:].astype(BF16)
    sg_ref[...] = _silu(_dot(xb, wg_ref[...])).astype(BF16)


def _layer_spec(a, layer):
    return pl.BlockSpec((None,) + a.shape[1:], lambda bi, si: (layer,) + (0,) * (a.ndim - 1))


def _odd_proj(x, w, tabs, layer, prev=None):
    b, s, d = x.shape
    tm = PROJ_TM
    tok = lambda bi, si: (bi, si, 0)
    tab = lambda bi, si: (si, 0)
    wide = jax.ShapeDtypeStruct((b, s, d), BF16)
    narrow = jax.ShapeDtypeStruct((b, s, LANES), BF16)
    names = ("wq", "wkv", "wg")
    operands, in_specs, x_spec, x_shape = _input_specs(x, prev)
    return pl.pallas_call(
        functools.partial(_odd_proj_kernel, fused=prev is not None),
        grid=(b, s // tm),
        in_specs=in_specs + [_layer_spec(w[k], layer) for k in names]
        + [pl.BlockSpec((tm, LANES), tab), pl.BlockSpec((tm, LANES), tab)],
        out_specs=[pl.BlockSpec((None, tm, d), tok), pl.BlockSpec((None, tm, LANES), tok),
                   pl.BlockSpec((None, tm, LANES), tok), pl.BlockSpec((None, tm, d), tok)] + x_spec,
        out_shape=[wide, narrow, narrow, wide] + x_shape,
        compiler_params=_params(2),
        name="odd_proj",
    )(*operands, *[w[k] for k in names], tabs["c"], tabs["s"])


def _swa_kernel(sink_ref, q_ref, k_ref, v_ref, sg_ref, o_ref):
    n_blk = q_ref.shape[0] // QBLK
    n_pair = q_ref.shape[1] // LANES
    qi = pl.program_id(1)
    lane = lax.broadcasted_iota(jnp.int32, (QBLK, LANES), 1)
    lo_half = lane < SWA_DIM
    r = lax.broadcasted_iota(jnp.int32, (QBLK, 2 * QBLK), 0)
    c = lax.broadcasted_iota(jnp.int32, (QBLK, 2 * QBLK), 1)

    def block(blk, _):
        gb = qi * n_blk + blk
        q0 = pl.multiple_of(blk * QBLK, QBLK)
        k0 = pl.multiple_of(jnp.maximum(gb - 1, 0) * QBLK, QBLK)
        diff = (gb * QBLK + r) - (k0 + c)
        band = (diff >= 0) & (diff < WINDOW)
        kw = k_ref[pl.ds(k0, 2 * QBLK), :]
        vw = v_ref[pl.ds(k0, 2 * QBLK), :]
        qrows = []
        for p in range(n_pair):
            qp = q_ref[pl.ds(q0, QBLK), p * LANES:(p + 1) * LANES]
            zero = jnp.zeros_like(qp)
            qrows.append(jnp.where(lo_half, qp, zero))
            qrows.append(jnp.where(lo_half, zero, qp))
        s_all = _dot_nt(jnp.concatenate(qrows, axis=0), kw)
        ps = []
        inv = []
        for rb in range(2 * n_pair):
            sink = sink_ref[rb] * LOG2E
            s = jnp.where(band, s_all[rb * QBLK:(rb + 1) * QBLK, :], -jnp.inf)
            m = jnp.maximum(jnp.max(s, axis=1, keepdims=True), sink)
            p_ = jnp.exp2(s - m)
            inv.append(1.0 / (jnp.sum(p_, axis=1, keepdims=True) + jnp.exp2(sink - m)))
            ps.append(p_.astype(BF16))
        o_all = _dot(jnp.concatenate(ps, axis=0), vw)
        for p in range(n_pair):
            oa = o_all[(2 * p) * QBLK:(2 * p + 1) * QBLK, :] * inv[2 * p]
            ob = o_all[(2 * p + 1) * QBLK:(2 * p + 2) * QBLK, :] * inv[2 * p + 1]
            o = jnp.where(lo_half, oa, ob)
            cols = slice(p * LANES, (p + 1) * LANES)
            o_ref[pl.ds(q0, QBLK), cols] = (
                o * sg_ref[pl.ds(q0, QBLK), cols].astype(F32)).astype(BF16)
        return 0

    lax.fori_loop(0, n_blk, block, 0)


def _swa_attn(sinks, q, k, v, sg):
    b, s, d = q.shape
    tq = SWA_TQ
    tok = lambda bi, qi: (bi, qi, 0)
    full = lambda bi, qi: (bi, 0, 0)
    return pl.pallas_call(
        _swa_kernel,
        grid=(b, s // tq),
        in_specs=[pl.BlockSpec(memory_space=pltpu.SMEM),
                  pl.BlockSpec((None, tq, d), tok),
                  pl.BlockSpec((None, s, LANES), full), pl.BlockSpec((None, s, LANES), full),
                  pl.BlockSpec((None, tq, d), tok)],
        out_specs=pl.BlockSpec((None, tq, d), tok),
        out_shape=jax.ShapeDtypeStruct((b, s, d), BF16),
        compiler_params=_params(2),
        name="swa_attn",
    )(sinks, q, k, v, sg)


def _pad_heads(w, n_heads, dim, offsets):
    k = w.shape[0]
    n_off = len(offsets)
    w4 = w.reshape(k, n_heads // n_off, n_off, dim)
    parts = [jnp.pad(w4[:, :, i, :], ((0, 0), (0, 0), (off, LANES - off - dim)))
             for i, off in enumerate(offsets)]
    return jnp.stack(parts, axis=2).reshape(k, n_heads * LANES)


def _swap_halves(w, n_heads, dim):
    k = w.shape[0]
    w3 = w.reshape(k, n_heads, dim)
    return jnp.concatenate([w3[..., dim // 2:], w3[..., :dim // 2]], axis=-1).reshape(k, n_heads * dim)


def _even_weights(w_in, q_norm, w_uq, kv_norm, w_ukv, b_f):
    sizes = (MLA_Q_RANK, MLA_KV_RANK, MLA_ROPE, FOX_HEADS * FOX_DIM, FOX_HEADS * FOX_DIM,
             FOX_HEADS * FOX_DIM, FOX_HEADS, MLA_HEADS * MLA_V + FOX_HEADS * FOX_DIM)
    cuts = [int(c) for c in np.cumsum(sizes)[:-1]]
    w_cq, w_ckv, w_kpe, w_fq, w_fk, w_fv, w_f, w_g = jnp.split(w_in, cuts, axis=1)
    place = lambda w: _pad_heads(w, 1, MLA_ROPE, (MLA_NOPE,))
    wlat = jnp.concatenate([w_cq, w_ckv, place(w_kpe), place(_swap_halves(w_kpe, 1, MLA_ROPE)),
                            _pad_heads(w_f, 1, FOX_HEADS, (0,))], axis=1)
    v_off = (0, MLA_V)
    dq = MLA_NOPE + MLA_ROPE
    uq3 = w_uq.reshape(MLA_Q_RANK, MLA_HEADS, dq)
    uq_pe = uq3[..., MLA_NOPE:].reshape(MLA_Q_RANK, MLA_HEADS * MLA_ROPE)
    uq_a = _pad_heads(w_uq, MLA_HEADS, dq, (0,))
    uq_b = _pad_heads(_swap_halves(uq_pe, MLA_HEADS, MLA_ROPE), MLA_HEADS, MLA_ROPE, (MLA_NOPE,))
    ukv3 = w_ukv.reshape(MLA_KV_RANK, MLA_HEADS, MLA_NOPE + MLA_V)
    uk = ukv3[..., :MLA_NOPE].reshape(MLA_KV_RANK, MLA_HEADS * MLA_NOPE)
    uv = ukv3[..., MLA_NOPE:].reshape(MLA_KV_RANK, MLA_HEADS * MLA_V)
    bf = lambda a: a.astype(BF16)
    return dict(wlat=bf(wlat),
                wuqt=bf(jnp.concatenate([uq_a, uq_b], axis=1).T),
                wuk=bf(_pad_heads(uk, MLA_HEADS, MLA_NOPE, (0,))),
                wuvt=bf(_pad_heads(uv, MLA_HEADS, MLA_V, v_off).T),
                wfqt=bf(w_fq.T), wfk=bf(w_fk), wfvt=bf(w_fv.T),
                wg=bf(w_g), qn=q_norm.reshape(1, -1), kvn=kv_norm.reshape(1, -1),
                bf=jnp.pad(b_f, (0, LANES - FOX_HEADS)).reshape(1, LANES))


def _pair_heads(w):
    lead = w.shape[:-1]
    w4 = w.reshape(*lead, SWA_KV_HEADS, SWA_HEADS // SWA_KV_HEADS, -1)
    return jnp.swapaxes(w4, -3, -2).reshape(*lead, -1)


def _odd_weights(w_in, sinks, w_out):
    d = SWA_HEADS * SWA_DIM
    dkv = SWA_KV_HEADS * SWA_DIM
    w_q, w_k, w_v, w_g = jnp.split(w_in, [d, d + dkv, d + 2 * dkv], axis=1)
    wkv = jnp.concatenate([w_k, w_v], axis=1)
    return dict(wq=_pair_heads(w_q).astype(BF16), wkv=wkv.astype(BF16), wg=_pair_heads(w_g).astype(BF16),
                sinks=_pair_heads(sinks.reshape(-1, 1).T).reshape(-1),
                w_out=_pair_heads(w_out.T).T.astype(BF16))


def _rope_tables(s):
    pos = jnp.arange(s).astype(F32)

    def cs(dim):
        inv = ROPE_THETA ** (-jnp.arange(0, dim, 2, dtype=F32) / dim)
        ang = pos[:, None] * inv[None, :]
        cos, sin = jnp.cos(ang), jnp.sin(ang)
        return jnp.concatenate([cos, cos], axis=1), jnp.concatenate([-sin, sin], axis=1)

    c32, s32 = cs(MLA_ROPE)
    z = lambda n: jnp.zeros((s, n), F32)
    pad = LANES - MLA_NOPE - MLA_ROPE
    scale = (MLA_NOPE + MLA_ROPE) ** -0.5 * LOG2E
    even = dict(cqt=(jnp.concatenate([jnp.ones((s, MLA_NOPE), F32), c32, z(pad)], axis=1) * scale).T,
                sqt=(jnp.concatenate([z(MLA_NOPE), s32, z(pad)], axis=1) * scale).T,
                ck=jnp.concatenate([z(MLA_NOPE), c32, z(pad)], axis=1),
                sk=jnp.concatenate([z(MLA_NOPE), s32, z(pad)], axis=1))
    c64, s64 = cs(SWA_DIM)
    odd = dict(c=jnp.concatenate([c64, c64], axis=1), s=jnp.concatenate([s64, s64], axis=1))
    return even, odd


def kernel(x, even_w_in, even_q_norm, even_w_uq, even_kv_norm, even_w_ukv, even_b_f, even_w_out,
           even_ln_g, even_ln_b, odd_w_in, odd_sinks, odd_w_out, odd_ln_g, odd_ln_b):
    s = x.shape[1]
    even_tabs, odd_tabs = _rope_tables(s)
    we = jax.vmap(_even_weights)(even_w_in, even_q_norm, even_w_uq, even_kv_norm, even_w_ukv, even_b_f)
    wo = jax.vmap(_odd_weights)(odd_w_in, odd_sinks, odd_w_out)
    even_w_out = even_w_out.astype(BF16)
    row = lambda a: a[:, None, :]
    closing = {0: (even_w_out, row(even_ln_g), row(even_ln_b)),
               1: (wo["w_out"], row(odd_ln_g), row(odd_ln_b))}
    prev = None
    for layer in range(DEPTH):
        j = layer // 2
        if layer % 2 == 0:
            qt, k, vt, sg, *x_new = _even_proj(x, we, even_tabs, j, prev)
            x = x_new[0] if x_new else x
            o = _causal_attn(qt, k, vt, sg)
        else:
            q, k, v, sg, *x_new = _odd_proj(x, wo, odd_tabs, j, prev)
            x = x_new[0] if x_new else x
            o = _swa_attn(wo["sinks"][j], q, k, v, sg)
        prev = (o, *closing[layer % 2], j)
    o, w_out, g, bias, j = prev
    return _out_proj(o, x, w_out, g, bias, j)
```

```python
import functools
import math

import jax
import jax.numpy as jnp
import numpy as np
from jax import lax
from jax.experimental import pallas as pl
from jax.experimental.pallas import tpu as pltpu

D_MODEL = 1024
DEPTH = 4
ROPE_THETA = 10000.0
QBLK = 128
MLA_HEADS = 8
MLA_NOPE = 64
MLA_ROPE = 32
MLA_V = 64
MLA_Q_RANK = 256
MLA_KV_RANK = 128
FOX_HEADS = 8
FOX_DIM = 64
SWA_HEADS = 16
SWA_KV_HEADS = 2
SWA_DIM = 64
WINDOW = 128
RMS_EPS = 1e-6
LN_EPS = 1e-5
ALPHA = (2 * DEPTH) ** 0.25
LOG2E = math.log2(math.e)

LANES = 128
VMEM_LIMIT = 56 * 1024 * 1024
BF16 = jnp.bfloat16
F32 = jnp.float32

PROJ_TM = 256
ATT_T = 512
ATT_HG = 4
SWA_TQ = 512
EVEN_HEADS = MLA_HEADS + FOX_HEADS
EVEN_HW = EVEN_HEADS * LANES
MLA_HW = MLA_HEADS * LANES


def _dot(a, b):
    return jnp.dot(a, b, preferred_element_type=F32)


def _dot_nt(a, b):
    return lax.dot_general(a, b, (((1,), (1,)), ((), ())), preferred_element_type=F32)


def _rms(t, g):
    return t * lax.rsqrt(jnp.mean(t * t, axis=-1, keepdims=True) + RMS_EPS) * g


def _silu(g):
    return g / (1.0 + jnp.exp(-g))


def _split3(v):
    hi = v.astype(BF16)
    r1 = v - hi.astype(F32)
    mid = r1.astype(BF16)
    lo = (r1 - mid.astype(F32)).astype(BF16)
    return hi, mid, lo


def _params(n_axes, flags=None):
    return pltpu.CompilerParams(dimension_semantics=("arbitrary",) * n_axes,
                                vmem_limit_bytes=VMEM_LIMIT, flags=flags)


def _residual_norm(o_ref, x_ref, w_ref, g_ref, b_ref):
    z = ALPHA * x_ref[...] + _dot(o_ref[...], w_ref[...])
    mu = jnp.mean(z, axis=-1, keepdims=True)
    zc = z - mu
    var = jnp.mean(zc * zc, axis=-1, keepdims=True)
    return zc * lax.rsqrt(var + LN_EPS) * g_ref[...] + b_ref[...]


def _layer_input(refs, fused):
    if not fused:
        return refs[0][...], refs[1:]
    x = _residual_norm(*refs[:5])
    return x, refs[5:]


def _even_proj_kernel(*refs, fused):
    x, refs = _layer_input(refs, fused)
    (wlat_ref, wuqt_ref, wuk_ref, wuvt_ref, wfqt_ref, wfk_ref, wfvt_ref, wg_ref, qn_ref, kvn_ref,
     bf_ref, cqt_ref, sqt_ref, ck_ref, sk_ref, qt_ref, k_ref, vt_ref, sg_ref) = refs[:19]
    carry_ref = refs[-1]
    if fused:
        refs[19][...] = x
    tm = x.shape[0]
    xb = x.astype(BF16)

    hrow = lax.broadcasted_iota(jnp.int32, (MLA_HW, 1), 0) % (2 * LANES)
    ones_col = jnp.where((hrow == MLA_V) | (hrow == LANES), 1.0, 0.0).astype(F32)
    lat = _dot(xb, wlat_ref[...])
    o = 0
    cq = lat[:, o:o + MLA_Q_RANK]
    o += MLA_Q_RANK
    ckv = lat[:, o:o + MLA_KV_RANK]
    o += MLA_KV_RANK
    kpe = lat[:, o:o + LANES] * ck_ref[...] + lat[:, o + LANES:o + 2 * LANES] * sk_ref[...]
    o += 2 * LANES
    fl = lat[:, o:o + LANES] + bf_ref[...]

    rq = _rms(cq, qn_ref[...]).astype(BF16)
    qa = _dot_nt(wuqt_ref[:MLA_HW, :], rq)
    qb = _dot_nt(wuqt_ref[MLA_HW:, :], rq)
    cqt = cqt_ref[...]
    sqt = sqt_ref[...]
    for h in range(MLA_HEADS):
        hs = slice(h * LANES, (h + 1) * LANES)
        qt_ref[hs, :] = (qa[hs, :] * cqt + qb[hs, :] * sqt).astype(BF16)
    rkv = _rms(ckv, kvn_ref[...]).astype(BF16)
    kk = _dot(rkv, wuk_ref[...])
    for h in range(MLA_HEADS):
        hs = slice(h * LANES, (h + 1) * LANES)
        k_ref[:, hs] = (kk[:, hs] + kpe).astype(BF16)
    vt_ref[:MLA_HW, :] = (_dot_nt(wuvt_ref[...], rkv) + ones_col).astype(BF16)

    lane = lax.broadcasted_iota(jnp.int32, (1, LANES), 1)
    lf = (jnp.minimum(fl, 0.0) - jnp.log1p(jnp.exp(-jnp.abs(fl)))) * LOG2E
    lf = jnp.where(lane < FOX_HEADS, lf, 0.0)
    row = lax.broadcasted_iota(jnp.int32, (tm, tm), 0)
    col = lax.broadcasted_iota(jnp.int32, (tm, tm), 1)
    tri = jnp.where(col <= row, 1.0, 0.0).astype(BF16)
    hi, mid, lo = _split3(lf)
    c_loc = _dot(tri, hi) + _dot(tri, mid) + _dot(tri, lo)

    @pl.when(pl.program_id(1) == 0)
    def _():
        carry_ref[...] = jnp.zeros_like(carry_ref)

    c = c_loc + carry_ref[0:1, :]
    carry_ref[...] = jnp.broadcast_to(c[tm - 1:tm, :], carry_ref.shape)
    src = lax.broadcasted_iota(jnp.int32, (LANES, MLA_HW), 0)
    dst = lax.broadcasted_iota(jnp.int32, (LANES, MLA_HW), 1)
    c_lane0 = src * LANES + jnp.where(src % 2 == 0, FOX_DIM, 0)
    k_extra = None
    for t, part in enumerate(_split3(c)):
        place = jnp.where((dst == c_lane0 + t) & (src < FOX_HEADS), 1.0, 0.0).astype(BF16)
        term = _dot(part, place)
        k_extra = term if k_extra is None else k_extra + term

    fqt = _dot_nt(wfqt_ref[...], xb) * (FOX_DIM ** -0.5 * LOG2E)
    fk = _dot(xb, wfk_ref[...])
    fvt = _dot_nt(wfvt_ref[...], xb)
    half_row = lax.broadcasted_iota(jnp.int32, (FOX_DIM, tm), 0)
    neg_blk = jnp.where(half_row < 3, -1.0, 0.0).astype(BF16)
    ones_blk = jnp.where(half_row < 1, 1.0, 0.0).astype(BF16)
    lane_k = lax.broadcasted_iota(jnp.int32, (1, LANES), 1)
    for h in range(FOX_HEADS):
        off = (h % 2) * FOX_DIM
        base = MLA_HW + h * LANES
        data = slice(base + off, base + off + FOX_DIM)
        rest = slice(base + FOX_DIM - off, base + 2 * FOX_DIM - off)
        rows = slice(h * FOX_DIM, (h + 1) * FOX_DIM)
        qt_ref[data, :] = fqt[rows, :].astype(BF16)
        qt_ref[rest, :] = neg_blk
        vt_ref[data, :] = fvt[rows, :].astype(BF16)
        vt_ref[rest, :] = ones_blk
        pair = slice((h // 2) * LANES, (h // 2 + 1) * LANES)
        in_data = (lane_k >= off) & (lane_k < off + FOX_DIM)
        k_ref[:, base:base + LANES] = jnp.where(
            in_data, fk[:, pair], k_extra[:, h * LANES:(h + 1) * LANES]).astype(BF16)

    sg_ref[...] = _silu(_dot(xb, wg_ref[...])).astype(BF16)


def _input_specs(x, prev):
    b, s, d = x.shape
    tok = pl.BlockSpec((None, PROJ_TM, d), lambda bi, si: (bi, si, 0))
    if prev is None:
        return [x], [tok], [], []
    o, w_out, g, bias, layer = prev
    specs = [tok, tok] + [_layer_spec(a, layer) for a in (w_out, g, bias)]
    return [o, x, w_out, g, bias], specs, [tok], [jax.ShapeDtypeStruct((b, s, d), F32)]


def _even_proj(x, w, tabs, layer, prev=None):
    b, s, d = x.shape
    tm = PROJ_TM
    tok = lambda bi, si: (bi, si, 0)
    tok_t = lambda bi, si: (bi, 0, si)
    names = ("wlat", "wuqt", "wuk", "wuvt", "wfqt", "wfk", "wfvt", "wg", "qn", "kvn", "bf")
    operands, in_specs, x_spec, x_shape = _input_specs(x, prev)
    in_specs += [_layer_spec(w[k], layer) for k in names]
    in_specs += [pl.BlockSpec((LANES, tm), lambda bi, si: (0, si))] * 2
    in_specs += [pl.BlockSpec((tm, LANES), lambda bi, si: (si, 0))] * 2
    out_specs = [pl.BlockSpec((None, EVEN_HW, tm), tok_t), pl.BlockSpec((None, tm, EVEN_HW), tok),
                 pl.BlockSpec((None, EVEN_HW, tm), tok_t), pl.BlockSpec((None, tm, d), tok)]
    out_shape = [jax.ShapeDtypeStruct((b, EVEN_HW, s), BF16), jax.ShapeDtypeStruct((b, s, EVEN_HW), BF16),
                 jax.ShapeDtypeStruct((b, EVEN_HW, s), BF16), jax.ShapeDtypeStruct((b, s, d), BF16)]
    return pl.pallas_call(
        functools.partial(_even_proj_kernel, fused=prev is not None),
        grid=(b, s // tm),
        in_specs=in_specs,
        out_specs=out_specs + x_spec,
        out_shape=out_shape + x_shape,
        scratch_shapes=[pltpu.VMEM((8, LANES), F32)],
        compiler_params=_params(2),
        name="even_proj",
    )(*operands, *[w[k] for k in names], tabs["cqt"], tabs["sqt"], tabs["ck"], tabs["sk"])


def _causal_attn_kernel(qt_ref, qtn_ref, k_ref, vt_ref, sg_ref, o_ref,
                        sa_ref, sb_ref, sc_ref, mc_ref):
    t = qt_ref.shape[1]
    n_heads = qt_ref.shape[0] // LANES
    qi = pl.program_id(2)
    heads = [slice(h * LANES, (h + 1) * LANES) for h in range(n_heads)]

    def key_slice(kj):
        return pl.ds(pl.multiple_of(kj * t, t), t)

    def col_max(s):
        return jnp.max(s, axis=0, keepdims=True)

    def causal_mask(s):
        keep = (lax.broadcasted_iota(jnp.int32, (t, t), 0) <= lax.broadcasted_iota(jnp.int32, (t, t), 1))
        return jnp.where(keep, s, -jnp.inf)

    def scores_head(q_ref, kj, buf, h, masked=False):
        s = _dot(k_ref[key_slice(kj), heads[h]], q_ref[heads[h], :])
        if masked:
            s = causal_mask(s)
        buf[h] = s
        return col_max(s)

    def process_head(kj, s, h, tile_max, m, acc):
        m_new = jnp.maximum(m, tile_max)
        p = jnp.exp2(s - m_new).astype(BF16)
        return m_new, jnp.exp2(m - m_new) * acc + _dot(vt_ref[heads[h], key_slice(kj)], p)

    def stage(kj, cur, nxt, tile_max, state, mask_next=False):
        next_max, out = [], []
        for h in range(n_heads):
            next_max.append(scores_head(qt_ref, kj + 1, nxt, h, mask_next))
            out.append(process_head(kj, cur[h], h, tile_max[h], *state[h]))
        return tuple(next_max), tuple(out)

    def two_stages(i, carry):
        tile_max, state = carry
        tile_max, state = stage(2 * i + 1, sa_ref, sb_ref, tile_max, state)
        return stage(2 * i + 2, sb_ref, sa_ref, tile_max, state)

    def prefetch_head(h):
        mc_ref[h, 0:1, :] = scores_head(qtn_ref, 0, sc_ref, h)

    def last_tile(buf, tile_max, state, masked, prefetch_first):
        out = []
        for h in range(n_heads):
            if prefetch_first:
                prefetch_head(h)
            s = buf[h]
            if masked:
                s = causal_mask(s)
            out.append(process_head(qi, s, h, col_max(s) if masked else tile_max[h], *state[h]))
            if not prefetch_first:
                prefetch_head(h)
        return tuple(out)

    @pl.when(qi == 0)
    def _():
        for h in range(n_heads):
            mc_ref[h, 0:1, :] = scores_head(qt_ref, 0, sc_ref, h)

    def only_tile(state):
        return last_tile(sc_ref, None, state, masked=True, prefetch_first=False)

    def several_tiles(state):
        tile_max = tuple(mc_ref[h, 0:1, :] for h in range(n_heads))
        tile_max, state = stage(0, sc_ref, sa_ref, tile_max, state)
        tile_max, state = lax.fori_loop(0, (qi - 1) // 2, two_stages, (tile_max, state))

        def odd_tail(tile_max, state):
            return last_tile(sa_ref, None, state, masked=True, prefetch_first=True)

        def even_tail(tile_max, state):
            tile_max, state = stage(qi - 1, sa_ref, sb_ref, tile_max, state, mask_next=True)
            return last_tile(sb_ref, tile_max, state, masked=False, prefetch_first=True)

        return lax.cond(qi % 2 == 1, odd_tail, even_tail, tile_max, state)

    init = tuple((jnp.full((1, t), -jnp.inf, F32), jnp.zeros((LANES, t), F32)) for _ in range(n_heads))
    state = lax.cond(qi == 0, only_tile, several_tiles, init)

    row = lax.broadcasted_iota(jnp.int32, (LANES, t), 0)
    for hp in range(n_heads // 2):
        a0 = state[2 * hp][1]
        a1 = state[2 * hp + 1][1]
        pair_t = jnp.where(row < MLA_V, a0 / a0[MLA_V:MLA_V + 1, :], a1 / a1[0:1, :])
        cols = slice(hp * LANES, (hp + 1) * LANES)
        o_ref[:, cols] = (pair_t.T * sg_ref[:, cols].astype(F32)).astype(BF16)


def _causal_attn(qt, k, vt, sg):
    b, s, hw = k.shape
    t = ATT_T
    nq = s // t
    gw = ATT_HG * LANES
    ow = ATT_HG * MLA_V
    score_buf = pltpu.VMEM((ATT_HG, t, t), F32)
    return pl.pallas_call(
        _causal_attn_kernel,
        grid=(b, hw // gw, nq),
        in_specs=[pl.BlockSpec((None, gw, t), lambda bi, g, qi: (bi, g, qi)),
                  pl.BlockSpec((None, gw, t), lambda bi, g, qi: (bi, g, jnp.minimum(qi + 1, nq - 1))),
                  pl.BlockSpec((None, s, gw), lambda bi, g, qi: (bi, 0, g)),
                  pl.BlockSpec((None, gw, s), lambda bi, g, qi: (bi, g, 0)),
                  pl.BlockSpec((None, t, ow), lambda bi, g, qi: (bi, qi, g))],
        out_specs=pl.BlockSpec((None, t, ow), lambda bi, g, qi: (bi, qi, g)),
        out_shape=jax.ShapeDtypeStruct((b, s, (hw // LANES) * MLA_V), BF16),
        scratch_shapes=[score_buf, score_buf, score_buf, pltpu.VMEM((ATT_HG, 8, t), F32)],
        compiler_params=_params(3),
        name="causal_attn",
    )(qt, qt, k, vt, sg)


def _out_kernel(o_ref, x_ref, w_ref, g_ref, b_ref, y_ref):
    y_ref[...] = _residual_norm(o_ref, x_ref, w_ref, g_ref, b_ref)


def _out_proj(o, x, w_out, g, bias, layer):
    b, s, d = x.shape
    tm = PROJ_TM
    tok = lambda bi, si: (bi, si, 0)
    return pl.pallas_call(
        _out_kernel,
        grid=(b, s // tm),
        in_specs=[pl.BlockSpec((None, tm, d), tok), pl.BlockSpec((None, tm, d), tok),
                  _layer_spec(w_out, layer), _layer_spec(g, layer), _layer_spec(bias, layer)],
        out_specs=pl.BlockSpec((None, tm, d), tok),
        out_shape=jax.ShapeDtypeStruct((b, s, d), F32),
        compiler_params=_params(2),
        name="out_proj_ln",
    )(o, x, w_out, g, bias)


def _odd_proj_kernel(*refs, fused):
    x, refs = _layer_input(refs, fused)
    (wqt_ref, wk_ref, wvt_ref, wg_ref, c_ref, s_ref, ct_ref, st_ref,
     qt_ref, k_ref, vt_ref, sg_ref) = refs[:12]
    if fused:
        refs[12][...] = x
    d = SWA_HEADS * SWA_DIM
    half = SWA_DIM // 2
    xb = x.astype(BF16)

    kk = _dot(xb, wk_ref[...])
    lane = lax.broadcasted_iota(jnp.int32, (1, LANES), 1)
    first_half = (lane % SWA_DIM) < half
    k_sw = jnp.where(first_half, pltpu.roll(kk, LANES - half, 1), pltpu.roll(kk, half, 1))
    k_ref[...] = (kk * c_ref[...] + k_sw * s_ref[...]).astype(BF16)

    qq = _dot_nt(wqt_ref[...], xb)
    ctt = ct_ref[...]
    stt = st_ref[...]
    scale = SWA_DIM ** -0.5 * LOG2E
    for p in range(d // LANES):
        blk = qq[p * LANES:(p + 1) * LANES, :]
        q_sw = jnp.concatenate([blk[half:2 * half], blk[:half], blk[3 * half:], blk[2 * half:3 * half]], axis=0)
        qt_ref[p * LANES:(p + 1) * LANES, :] = ((blk * ctt + q_sw * stt) * scale).astype(BF16)

    vt_ref[...] = _dot_nt(wvt_ref[...], xb).astype(BF16)
    sg_ref[...] = _silu(_dot(xb, wg_ref[...])).astype(BF16)


def _layer_spec(a, layer):
    return pl.BlockSpec((None,) + a.shape[1:], lambda bi, si: (layer,) + (0,) * (a.ndim - 1))


def _odd_proj(x, w, tabs, layer, prev=None):
    b, s, d = x.shape
    tm = PROJ_TM
    tok = lambda bi, si: (bi, si, 0)
    tok_t = lambda bi, si: (bi, 0, si)
    tab = pl.BlockSpec((tm, LANES), lambda bi, si: (si, 0))
    tab_t = pl.BlockSpec((LANES, tm), lambda bi, si: (0, si))
    names = ("wqt", "wk", "wvt", "wg")
    operands, in_specs, x_spec, x_shape = _input_specs(x, prev)
    return pl.pallas_call(
        functools.partial(_odd_proj_kernel, fused=prev is not None),
        grid=(b, s // tm),
        in_specs=in_specs + [_layer_spec(w[k], layer) for k in names] + [tab, tab, tab_t, tab_t],
        out_specs=[pl.BlockSpec((None, d, tm), tok_t), pl.BlockSpec((None, tm, LANES), tok),
                   pl.BlockSpec((None, LANES, tm), tok_t), pl.BlockSpec((None, tm, d), tok)] + x_spec,
        out_shape=[jax.ShapeDtypeStruct((b, d, s), BF16), jax.ShapeDtypeStruct((b, s, LANES), BF16),
                   jax.ShapeDtypeStruct((b, LANES, s), BF16), jax.ShapeDtypeStruct((b, s, d), BF16)] + x_shape,
        compiler_params=_params(2),
        name="odd_proj",
    )(*operands, *[w[k] for k in names], tabs["c"], tabs["s"], tabs["ct"], tabs["st"])


def _swa_kernel(sink_ref, qt_ref, k_ref, vt_ref, sg_ref, o_ref):
    n_blk = qt_ref.shape[1] // QBLK
    n_pair = qt_ref.shape[0] // LANES
    win = 2 * QBLK
    qi = pl.program_id(1)
    row = lax.broadcasted_iota(jnp.int32, (LANES, QBLK), 0)
    lo_rows = row < SWA_DIM
    rel = (lax.broadcasted_iota(jnp.int32, (win, QBLK), 1) - lax.broadcasted_iota(jnp.int32, (win, QBLK), 0))
    sink_row = jnp.concatenate([jnp.full((1, QBLK), sink_ref[c] * LOG2E, F32) for c in range(2 * n_pair)],
                               axis=1)
    ones_blk = jnp.ones((16, win), BF16)

    def window_start(blk):
        return pl.multiple_of(jnp.maximum(qi * n_blk + blk - 1, 0) * QBLK, QBLK)

    def scores(blk):
        q0 = blk * QBLK
        cols = []
        for p in range(n_pair):
            qp = qt_ref[p * LANES:(p + 1) * LANES, q0:q0 + QBLK]
            zero = jnp.zeros_like(qp)
            cols.append(jnp.where(lo_rows, qp, zero))
            cols.append(jnp.where(lo_rows, zero, qp))
        s = _dot(k_ref[pl.ds(window_start(blk), win), :], jnp.concatenate(cols, axis=1))
        shift = jnp.where(qi * n_blk + blk == 0, 0, QBLK)
        diff = rel + shift
        band = (diff >= 0) & (diff < WINDOW)
        return jnp.concatenate([jnp.where(band, s[:, c * QBLK:(c + 1) * QBLK], -jnp.inf)
                                for c in range(2 * n_pair)], axis=1)

    def process(blk, s):
        q0 = blk * QBLK
        m = jnp.maximum(jnp.max(s, axis=0, keepdims=True), sink_row)
        p = jnp.exp2(s - m).astype(BF16)
        vt = jnp.concatenate([vt_ref[:, pl.ds(window_start(blk), win)], ones_blk], axis=0)
        ot = _dot(vt, p)
        inv = 1.0 / (ot[LANES:LANES + 1, :] + jnp.exp2(sink_row - m))
        for pr in range(n_pair):
            ca = slice((2 * pr) * QBLK, (2 * pr + 1) * QBLK)
            cb = slice((2 * pr + 1) * QBLK, (2 * pr + 2) * QBLK)
            pair = jnp.concatenate([ot[:SWA_DIM, ca] * inv[:, ca], ot[SWA_DIM:LANES, cb] * inv[:, cb]],
                                   axis=0).T
            cols = slice(pr * LANES, (pr + 1) * LANES)
            o_ref[q0:q0 + QBLK, cols] = (pair * sg_ref[q0:q0 + QBLK, cols].astype(F32)).astype(BF16)

    s_next = scores(0)
    for blk in range(n_blk):
        s_cur = s_next
        if blk + 1 < n_blk:
            s_next = scores(blk + 1)
        process(blk, s_cur)


def _swa_attn(sinks, qt, k, vt, sg):
    b, d, s = qt.shape
    tq = SWA_TQ
    tok = lambda bi, qi: (bi, qi, 0)
    return pl.pallas_call(
        _swa_kernel,
        grid=(b, s // tq),
        in_specs=[pl.BlockSpec(memory_space=pltpu.SMEM),
                  pl.BlockSpec((None, d, tq), lambda bi, qi: (bi, 0, qi)),
                  pl.BlockSpec((None, s, LANES), lambda bi, qi: (bi, 0, 0)),
                  pl.BlockSpec((None, LANES, s), lambda bi, qi: (bi, 0, 0)),
                  pl.BlockSpec((None, tq, d), tok)],
        out_specs=pl.BlockSpec((None, tq, d), tok),
        out_shape=jax.ShapeDtypeStruct((b, s, d), BF16),
        compiler_params=_params(2),
        name="swa_attn",
    )(sinks, qt, k, vt, sg)


def _pad_heads(w, n_heads, dim, offsets):
    k = w.shape[0]
    n_off = len(offsets)
    w4 = w.reshape(k, n_heads // n_off, n_off, dim)
    parts = [jnp.pad(w4[:, :, i, :], ((0, 0), (0, 0), (off, LANES - off - dim)))
             for i, off in enumerate(offsets)]
    return jnp.stack(parts, axis=2).reshape(k, n_heads * LANES)


def _swap_halves(w, n_heads, dim):
    k = w.shape[0]
    w3 = w.reshape(k, n_heads, dim)
    return jnp.concatenate([w3[..., dim // 2:], w3[..., :dim // 2]], axis=-1).reshape(k, n_heads * dim)


def _even_weights(w_in, q_norm, w_uq, kv_norm, w_ukv, b_f):
    sizes = (MLA_Q_RANK, MLA_KV_RANK, MLA_ROPE, FOX_HEADS * FOX_DIM, FOX_HEADS * FOX_DIM,
             FOX_HEADS * FOX_DIM, FOX_HEADS, MLA_HEADS * MLA_V + FOX_HEADS * FOX_DIM)
    cuts = [int(c) for c in np.cumsum(sizes)[:-1]]
    w_cq, w_ckv, w_kpe, w_fq, w_fk, w_fv, w_f, w_g = jnp.split(w_in, cuts, axis=1)
    place = lambda w: _pad_heads(w, 1, MLA_ROPE, (MLA_NOPE,))
    wlat = jnp.concatenate([w_cq, w_ckv, place(w_kpe), place(_swap_halves(w_kpe, 1, MLA_ROPE)),
                            _pad_heads(w_f, 1, FOX_HEADS, (0,))], axis=1)
    v_off = (0, MLA_V)
    dq = MLA_NOPE + MLA_ROPE
    uq3 = w_uq.reshape(MLA_Q_RANK, MLA_HEADS, dq)
    uq_pe = uq3[..., MLA_NOPE:].reshape(MLA_Q_RANK, MLA_HEADS * MLA_ROPE)
    uq_a = _pad_heads(w_uq, MLA_HEADS, dq, (0,))
    uq_b = _pad_heads(_swap_halves(uq_pe, MLA_HEADS, MLA_ROPE), MLA_HEADS, MLA_ROPE, (MLA_NOPE,))
    ukv3 = w_ukv.reshape(MLA_KV_RANK, MLA_HEADS, MLA_NOPE + MLA_V)
    uk = ukv3[..., :MLA_NOPE].reshape(MLA_KV_RANK, MLA_HEADS * MLA_NOPE)
    uv = ukv3[..., MLA_NOPE:].reshape(MLA_KV_RANK, MLA_HEADS * MLA_V)
    bf = lambda a: a.astype(BF16)
    return dict(wlat=bf(wlat),
                wuqt=bf(jnp.concatenate([uq_a, uq_b], axis=1).T),
                wuk=bf(_pad_heads(uk, MLA_HEADS, MLA_NOPE, (0,))),
                wuvt=bf(_pad_heads(uv, MLA_HEADS, MLA_V, v_off).T),
                wfqt=bf(w_fq.T), wfk=bf(w_fk), wfvt=bf(w_fv.T),
                wg=bf(w_g), qn=q_norm.reshape(1, -1), kvn=kv_norm.reshape(1, -1),
                bf=jnp.pad(b_f, (0, LANES - FOX_HEADS)).reshape(1, LANES))


def _pair_heads(w):
    lead = w.shape[:-1]
    w4 = w.reshape(*lead, SWA_KV_HEADS, SWA_HEADS // SWA_KV_HEADS, -1)
    return jnp.swapaxes(w4, -3, -2).reshape(*lead, -1)


def _odd_weights(w_in, sinks, w_out):
    d = SWA_HEADS * SWA_DIM
    dkv = SWA_KV_HEADS * SWA_DIM
    w_q, w_k, w_v, w_g = jnp.split(w_in, [d, d + dkv, d + 2 * dkv], axis=1)
    pair_rows = lambda a: jnp.swapaxes(
        a.reshape(SWA_KV_HEADS, SWA_HEADS // SWA_KV_HEADS, SWA_DIM, -1), 0, 1).reshape(a.shape)
    return dict(wqt=pair_rows(w_q.T).astype(BF16), wk=w_k.astype(BF16), wvt=w_v.T.astype(BF16),
                wg=_pair_heads(w_g).astype(BF16),
                sinks=_pair_heads(sinks.reshape(-1, 1).T).reshape(-1),
                w_out=pair_rows(w_out).astype(BF16))


def _rope_tables(s):
    pos = jnp.arange(s).astype(F32)

    def cs(dim):
        inv = ROPE_THETA ** (-jnp.arange(0, dim, 2, dtype=F32) / dim)
        ang = pos[:, None] * inv[None, :]
        cos, sin = jnp.cos(ang), jnp.sin(ang)
        return jnp.concatenate([cos, cos], axis=1), jnp.concatenate([-sin, sin], axis=1)

    c32, s32 = cs(MLA_ROPE)
    z = lambda n: jnp.zeros((s, n), F32)
    pad = LANES - MLA_NOPE - MLA_ROPE
    scale = (MLA_NOPE + MLA_ROPE) ** -0.5 * LOG2E
    even = dict(cqt=(jnp.concatenate([jnp.ones((s, MLA_NOPE), F32), c32, z(pad)], axis=1) * scale).T,
                sqt=(jnp.concatenate([z(MLA_NOPE), s32, z(pad)], axis=1) * scale).T,
                ck=jnp.concatenate([z(MLA_NOPE), c32, z(pad)], axis=1),
                sk=jnp.concatenate([z(MLA_NOPE), s32, z(pad)], axis=1))
    c64, s64 = cs(SWA_DIM)
    odd = dict(c=jnp.concatenate([c64, c64], axis=1), s=jnp.concatenate([s64, s64], axis=1))
    odd.update(ct=odd["c"].T, st=odd["s"].T)
    return even, odd


def kernel(x, even_w_in, even_q_norm, even_w_uq, even_kv_norm, even_w_ukv, even_b_f, even_w_out,
           even_ln_g, even_ln_b, odd_w_in, odd_sinks, odd_w_out, odd_ln_g, odd_ln_b):
    s = x.shape[1]
    even_tabs, odd_tabs = _rope_tables(s)
    we = jax.vmap(_even_weights)(even_w_in, even_q_norm, even_w_uq, even_kv_norm, even_w_ukv, even_b_f)
    wo = jax.vmap(_odd_weights)(odd_w_in, odd_sinks, odd_w_out)
    even_w_out = even_w_out.astype(BF16)
    row = lambda a: a[:, None, :]
    closing = {0: (even_w_out, row(even_ln_g), row(even_ln_b)),
               1: (wo["w_out"], row(odd_ln_g), row(odd_ln_b))}
    prev = None
    for layer in range(DEPTH):
        j = layer // 2
        if layer % 2 == 0:
            qt, k, vt, sg, *x_new = _even_proj(x, we, even_tabs, j, prev)
            x = x_new[0] if x_new else x
            o = _causal_attn(qt, k, vt, sg)
        else:
            qt, k, vt, sg, *x_new = _odd_proj(x, wo, odd_tabs, j, prev)
            x = x_new[0] if x_new else x
            o = _swa_attn(wo["sinks"][j], qt, k, vt, sg)
        prev = (o, *closing[layer % 2], j)
    o, w_out, g, bias, j = prev
    return _out_proj(o, x, w_out, g, bias, j)
```

```python
import functools
import math

import jax
import jax.numpy as jnp
import numpy as np
from jax import lax
from jax.experimental import pallas as pl
from jax.experimental.pallas import tpu as pltpu

D_MODEL = 1024
DEPTH = 4
ROPE_THETA = 10000.0
QBLK = 128
MLA_HEADS = 8
MLA_NOPE = 64
MLA_ROPE = 32
MLA_V = 64
MLA_Q_RANK = 256
MLA_KV_RANK = 128
FOX_HEADS = 8
FOX_DIM = 64
SWA_HEADS = 16
SWA_KV_HEADS = 2
SWA_DIM = 64
WINDOW = 128
RMS_EPS = 1e-6
LN_EPS = 1e-5
ALPHA = (2 * DEPTH) ** 0.25
LOG2E = math.log2(math.e)

LANES = 128
VMEM_LIMIT = 56 * 1024 * 1024
BF16 = jnp.bfloat16
F32 = jnp.float32

PROJ_TM = 512
ATT_T = 512
ATT_HG = 4
SWA_TQ = 512
EVEN_HEADS = MLA_HEADS + FOX_HEADS
EVEN_HW = EVEN_HEADS * LANES
MLA_HW = MLA_HEADS * LANES


def _dot(a, b):
    return jnp.dot(a, b, preferred_element_type=F32)


def _dot_nt(a, b):
    return lax.dot_general(a, b, (((1,), (1,)), ((), ())), preferred_element_type=F32)


def _rms(t, g):
    return t * lax.rsqrt(jnp.mean(t * t, axis=-1, keepdims=True) + RMS_EPS) * g


def _silu(g):
    return g / (1.0 + jnp.exp(-g))


def _pack3(v):
    hi = v.astype(BF16).astype(F32)
    r1 = v - hi
    mid = r1.astype(BF16).astype(F32)
    lo = r1 - mid
    return (hi + pltpu.roll(mid, FOX_HEADS, 1) + pltpu.roll(lo, 2 * FOX_HEADS, 1)).astype(BF16)


def _params(n_axes, flags=None):
    return pltpu.CompilerParams(dimension_semantics=("arbitrary",) * n_axes,
                                vmem_limit_bytes=VMEM_LIMIT, flags=flags)


def _residual_norm(o_ref, x_ref, w_ref, g_ref, b_ref):
    z = ALPHA * x_ref[...] + _dot(o_ref[...], w_ref[...])
    mu = jnp.mean(z, axis=-1, keepdims=True)
    zc = z - mu
    var = jnp.mean(zc * zc, axis=-1, keepdims=True)
    return zc * lax.rsqrt(var + LN_EPS) * g_ref[...] + b_ref[...]


def _layer_input(refs, fused):
    if not fused:
        return refs[0][...], refs[1:]
    x = _residual_norm(*refs[:5])
    return x, refs[5:]


def _even_proj_kernel(*refs, fused):
    x, refs = _layer_input(refs, fused)
    (wlat_ref, wuqt_ref, wuk_ref, wuvt_ref, wfqt_ref, wfk_ref, wfvt_ref, wg_ref, qn_ref, kvn_ref,
     bf_ref, cqt_ref, sqt_ref, ck_ref, sk_ref, qt_ref, k_ref, vt_ref, sg_ref) = refs[:19]
    carry_ref = refs[-1]
    if fused:
        refs[19][...] = x
    tm = x.shape[0]
    xb = x.astype(BF16)

    hrow = lax.broadcasted_iota(jnp.int32, (MLA_HW, 1), 0) % (2 * LANES)
    ones_col = jnp.where((hrow == MLA_V) | (hrow == LANES), 1.0, 0.0).astype(F32)
    lat = _dot(xb, wlat_ref[...])
    o = 0
    cq = lat[:, o:o + MLA_Q_RANK]
    o += MLA_Q_RANK
    ckv = lat[:, o:o + MLA_KV_RANK]
    o += MLA_KV_RANK
    kpe = lat[:, o:o + LANES] * ck_ref[...] + lat[:, o + LANES:o + 2 * LANES] * sk_ref[...]
    o += 2 * LANES
    fl = lat[:, o:o + LANES] + bf_ref[...]

    rq = _rms(cq, qn_ref[...]).astype(BF16)
    qa = _dot_nt(wuqt_ref[:MLA_HW, :], rq)
    qb = _dot_nt(wuqt_ref[MLA_HW:, :], rq)
    cqt = cqt_ref[...]
    sqt = sqt_ref[...]
    for h in range(MLA_HEADS):
        hs = slice(h * LANES, (h + 1) * LANES)
        qt_ref[hs, :] = (qa[hs, :] * cqt + qb[hs, :] * sqt).astype(BF16)
    rkv = _rms(ckv, kvn_ref[...]).astype(BF16)
    kk = _dot(rkv, wuk_ref[...])
    for h in range(MLA_HEADS):
        hs = slice(h * LANES, (h + 1) * LANES)
        k_ref[:, hs] = (kk[:, hs] + kpe).astype(BF16)
    vt_ref[:MLA_HW, :] = (_dot_nt(wuvt_ref[...], rkv) + ones_col).astype(BF16)

    lane = lax.broadcasted_iota(jnp.int32, (1, LANES), 1)
    lf = (jnp.minimum(fl, 0.0) - jnp.log1p(jnp.exp(-jnp.abs(fl)))) * LOG2E
    lf = jnp.where(lane < FOX_HEADS, lf, 0.0)
    row = lax.broadcasted_iota(jnp.int32, (tm, tm), 0)
    col = lax.broadcasted_iota(jnp.int32, (tm, tm), 1)
    tri = jnp.where(col <= row, 1.0, 0.0).astype(BF16)
    sums = _dot(tri, _pack3(lf))
    c_loc = sums + pltpu.roll(sums, LANES - FOX_HEADS, 1) + pltpu.roll(sums, LANES - 2 * FOX_HEADS, 1)
    c_loc = jnp.where(lane < FOX_HEADS, c_loc, 0.0)

    @pl.when(pl.program_id(1) == 0)
    def _():
        carry_ref[...] = jnp.zeros_like(carry_ref)

    c = c_loc + carry_ref[0:1, :]
    carry_ref[...] = jnp.broadcast_to(c[tm - 1:tm, :], carry_ref.shape)
    src = lax.broadcasted_iota(jnp.int32, (LANES, MLA_HW), 0)
    dst = lax.broadcasted_iota(jnp.int32, (LANES, MLA_HW), 1)
    src_h = src % FOX_HEADS
    c_lane = src_h * LANES + jnp.where(src_h % 2 == 0, FOX_DIM, 0) + src // FOX_HEADS
    place = jnp.where((dst == c_lane) & (src < 3 * FOX_HEADS), 1.0, 0.0).astype(BF16)
    k_extra = _dot(_pack3(c), place)

    fqt = _dot_nt(wfqt_ref[...], xb) * (FOX_DIM ** -0.5 * LOG2E)
    fk = _dot(xb, wfk_ref[...])
    fvt = _dot_nt(wfvt_ref[...], xb)
    half_row = lax.broadcasted_iota(jnp.int32, (FOX_DIM, tm), 0)
    neg_blk = jnp.where(half_row < 3, -1.0, 0.0).astype(BF16)
    ones_blk = jnp.where(half_row < 1, 1.0, 0.0).astype(BF16)
    lane_k = lax.broadcasted_iota(jnp.int32, (1, LANES), 1)
    for h in range(FOX_HEADS):
        off = (h % 2) * FOX_DIM
        base = MLA_HW + h * LANES
        data = slice(base + off, base + off + FOX_DIM)
        rest = slice(base + FOX_DIM - off, base + 2 * FOX_DIM - off)
        rows = slice(h * FOX_DIM, (h + 1) * FOX_DIM)
        qt_ref[data, :] = fqt[rows, :].astype(BF16)
        qt_ref[rest, :] = neg_blk
        vt_ref[data, :] = fvt[rows, :].astype(BF16)
        vt_ref[rest, :] = ones_blk
        pair = slice((h // 2) * LANES, (h // 2 + 1) * LANES)
        in_data = (lane_k >= off) & (lane_k < off + FOX_DIM)
        k_ref[:, base:base + LANES] = jnp.where(
            in_data, fk[:, pair], k_extra[:, h * LANES:(h + 1) * LANES]).astype(BF16)

    sg_ref[...] = _silu(_dot(xb, wg_ref[...])).astype(BF16)


def _input_specs(x, prev):
    b, s, d = x.shape
    tok = pl.BlockSpec((None, PROJ_TM, d), lambda bi, si: (bi, si, 0))
    if prev is None:
        return [x], [tok], [], []
    o, w_out, g, bias, layer = prev
    specs = [tok, tok] + [_layer_spec(a, layer) for a in (w_out, g, bias)]
    return [o, x, w_out, g, bias], specs, [tok], [jax.ShapeDtypeStruct((b, s, d), F32)]


def _even_proj(x, w, tabs, layer, prev=None):
    b, s, d = x.shape
    tm = PROJ_TM
    tok = lambda bi, si: (bi, si, 0)
    tok_t = lambda bi, si: (bi, 0, si)
    names = ("wlat", "wuqt", "wuk", "wuvt", "wfqt", "wfk", "wfvt", "wg", "qn", "kvn", "bf")
    operands, in_specs, x_spec, x_shape = _input_specs(x, prev)
    in_specs += [_layer_spec(w[k], layer) for k in names]
    in_specs += [pl.BlockSpec((LANES, tm), lambda bi, si: (0, si))] * 2
    in_specs += [pl.BlockSpec((tm, LANES), lambda bi, si: (si, 0))] * 2
    out_specs = [pl.BlockSpec((None, EVEN_HW, tm), tok_t), pl.BlockSpec((None, tm, EVEN_HW), tok),
                 pl.BlockSpec((None, EVEN_HW, tm), tok_t), pl.BlockSpec((None, tm, d), tok)]
    out_shape = [jax.ShapeDtypeStruct((b, EVEN_HW, s), BF16), jax.ShapeDtypeStruct((b, s, EVEN_HW), BF16),
                 jax.ShapeDtypeStruct((b, EVEN_HW, s), BF16), jax.ShapeDtypeStruct((b, s, d), BF16)]
    return pl.pallas_call(
        functools.partial(_even_proj_kernel, fused=prev is not None),
        grid=(b, s // tm),
        in_specs=in_specs,
        out_specs=out_specs + x_spec,
        out_shape=out_shape + x_shape,
        scratch_shapes=[pltpu.VMEM((8, LANES), F32)],
        compiler_params=_params(2),
        name="even_proj",
    )(*operands, *[w[k] for k in names], tabs["cqt"], tabs["sqt"], tabs["ck"], tabs["sk"])


def _causal_attn_kernel(qt_ref, qtn_ref, k_ref, vt_ref, sg_ref, o_ref,
                        sa_ref, sb_ref, sc_ref, mc_ref):
    t = qt_ref.shape[1]
    n_heads = qt_ref.shape[0] // LANES
    qi = pl.program_id(2)
    heads = [slice(h * LANES, (h + 1) * LANES) for h in range(n_heads)]

    def key_slice(kj):
        return pl.ds(pl.multiple_of(kj * t, t), t)

    def col_max(s):
        return jnp.max(s, axis=0, keepdims=True)

    def causal_mask(s):
        keep = (lax.broadcasted_iota(jnp.int32, (t, t), 0) <= lax.broadcasted_iota(jnp.int32, (t, t), 1))
        return jnp.where(keep, s, -jnp.inf)

    def scores_head(q_ref, kj, buf, h, masked=False):
        s = _dot(k_ref[key_slice(kj), heads[h]], q_ref[heads[h], :])
        if masked:
            s = causal_mask(s)
        buf[h] = s
        return col_max(s)

    def process_head(kj, s, h, tile_max, m, acc):
        m_new = jnp.maximum(m, tile_max)
        p = jnp.exp2(s - m_new).astype(BF16)
        return m_new, jnp.exp2(m - m_new) * acc + _dot(vt_ref[heads[h], key_slice(kj)], p)

    def stage(kj, cur, nxt, tile_max, state, mask_next=False):
        next_max, out = [], []
        for h in range(n_heads):
            next_max.append(scores_head(qt_ref, kj + 1, nxt, h, mask_next))
            out.append(process_head(kj, cur[h], h, tile_max[h], *state[h]))
        return tuple(next_max), tuple(out)

    def two_stages(i, carry):
        tile_max, state = carry
        tile_max, state = stage(2 * i + 1, sa_ref, sb_ref, tile_max, state)
        return stage(2 * i + 2, sb_ref, sa_ref, tile_max, state)

    def prefetch_head(h):
        mc_ref[h, 0:1, :] = scores_head(qtn_ref, 0, sc_ref, h)

    def last_tile(buf, tile_max, state, masked, prefetch_first):
        out = []
        for h in range(n_heads):
            if prefetch_first:
                prefetch_head(h)
            s = buf[h]
            if masked:
                s = causal_mask(s)
            out.append(process_head(qi, s, h, col_max(s) if masked else tile_max[h], *state[h]))
            if not prefetch_first:
                prefetch_head(h)
        return tuple(out)

    @pl.when(qi == 0)
    def _():
        for h in range(n_heads):
            mc_ref[h, 0:1, :] = scores_head(qt_ref, 0, sc_ref, h)

    def only_tile(state):
        return last_tile(sc_ref, None, state, masked=True, prefetch_first=False)

    def several_tiles(state):
        tile_max = tuple(mc_ref[h, 0:1, :] for h in range(n_heads))
        tile_max, state = stage(0, sc_ref, sa_ref, tile_max, state)
        tile_max, state = lax.fori_loop(0, (qi - 1) // 2, two_stages, (tile_max, state))

        def odd_tail(tile_max, state):
            return last_tile(sa_ref, None, state, masked=True, prefetch_first=True)

        def even_tail(tile_max, state):
            tile_max, state = stage(qi - 1, sa_ref, sb_ref, tile_max, state, mask_next=True)
            return last_tile(sb_ref, tile_max, state, masked=False, prefetch_first=True)

        return lax.cond(qi % 2 == 1, odd_tail, even_tail, tile_max, state)

    init = tuple((jnp.full((1, t), -jnp.inf, F32), jnp.zeros((LANES, t), F32)) for _ in range(n_heads))
    state = lax.cond(qi == 0, only_tile, several_tiles, init)

    row = lax.broadcasted_iota(jnp.int32, (LANES, t), 0)
    for hp in range(n_heads // 2):
        a0 = state[2 * hp][1]
        a1 = state[2 * hp + 1][1]
        pair_t = jnp.where(row < MLA_V, a0 / a0[MLA_V:MLA_V + 1, :], a1 / a1[0:1, :])
        cols = slice(hp * LANES, (hp + 1) * LANES)
        o_ref[:, cols] = (pair_t.T * sg_ref[:, cols].astype(F32)).astype(BF16)


def _causal_attn(qt, k, vt, sg):
    b, s, hw = k.shape
    t = ATT_T
    nq = s // t
    gw = ATT_HG * LANES
    ow = ATT_HG * MLA_V
    score_buf = pltpu.VMEM((ATT_HG, t, t), F32)
    return pl.pallas_call(
        _causal_attn_kernel,
        grid=(b, hw // gw, nq),
        in_specs=[pl.BlockSpec((None, gw, t), lambda bi, g, qi: (bi, g, qi)),
                  pl.BlockSpec((None, gw, t), lambda bi, g, qi: (bi, g, jnp.minimum(qi + 1, nq - 1))),
                  pl.BlockSpec((None, s, gw), lambda bi, g, qi: (bi, 0, g)),
                  pl.BlockSpec((None, gw, s), lambda bi, g, qi: (bi, g, 0)),
                  pl.BlockSpec((None, t, ow), lambda bi, g, qi: (bi, qi, g))],
        out_specs=pl.BlockSpec((None, t, ow), lambda bi, g, qi: (bi, qi, g)),
        out_shape=jax.ShapeDtypeStruct((b, s, (hw // LANES) * MLA_V), BF16),
        scratch_shapes=[score_buf, score_buf, score_buf, pltpu.VMEM((ATT_HG, 8, t), F32)],
        compiler_params=_params(3),
        name="causal_attn",
    )(qt, qt, k, vt, sg)


def _out_kernel(o_ref, x_ref, w_ref, g_ref, b_ref, y_ref):
    y_ref[...] = _residual_norm(o_ref, x_ref, w_ref, g_ref, b_ref)


def _out_proj(o, x, w_out, g, bias, layer):
    b, s, d = x.shape
    tm = PROJ_TM
    tok = lambda bi, si: (bi, si, 0)
    return pl.pallas_call(
        _out_kernel,
        grid=(b, s // tm),
        in_specs=[pl.BlockSpec((None, tm, d), tok), pl.BlockSpec((None, tm, d), tok),
                  _layer_spec(w_out, layer), _layer_spec(g, layer), _layer_spec(bias, layer)],
        out_specs=pl.BlockSpec((None, tm, d), tok),
        out_shape=jax.ShapeDtypeStruct((b, s, d), F32),
        compiler_params=_params(2),
        name="out_proj_ln",
    )(o, x, w_out, g, bias)


def _odd_proj_kernel(*refs, fused):
    x, refs = _layer_input(refs, fused)
    (wqt_ref, wk_ref, wvt_ref, wg_ref, c_ref, s_ref, ct_ref, st_ref,
     qt_ref, k_ref, vt_ref, sg_ref) = refs[:12]
    if fused:
        refs[12][...] = x
    d = SWA_HEADS * SWA_DIM
    half = SWA_DIM // 2
    xb = x.astype(BF16)

    kk = _dot(xb, wk_ref[...])
    lane = lax.broadcasted_iota(jnp.int32, (1, LANES), 1)
    first_half = (lane % SWA_DIM) < half
    k_sw = jnp.where(first_half, pltpu.roll(kk, LANES - half, 1), pltpu.roll(kk, half, 1))
    k_ref[...] = (kk * c_ref[...] + k_sw * s_ref[...]).astype(BF16)

    qq = _dot_nt(wqt_ref[...], xb)
    ctt = ct_ref[...]
    stt = st_ref[...]
    scale = SWA_DIM ** -0.5 * LOG2E
    for p in range(d // LANES):
        blk = qq[p * LANES:(p + 1) * LANES, :]
        q_sw = jnp.concatenate([blk[half:2 * half], blk[:half], blk[3 * half:], blk[2 * half:3 * half]], axis=0)
        qt_ref[p * LANES:(p + 1) * LANES, :] = ((blk * ctt + q_sw * stt) * scale).astype(BF16)

    vt_ref[...] = _dot_nt(wvt_ref[...], xb).astype(BF16)
    sg_ref[...] = _silu(_dot(xb, wg_ref[...])).astype(BF16)


def _layer_spec(a, layer):
    return pl.BlockSpec((None,) + a.shape[1:], lambda bi, si: (layer,) + (0,) * (a.ndim - 1))


def _odd_proj(x, w, tabs, layer, prev=None):
    b, s, d = x.shape
    tm = PROJ_TM
    tok = lambda bi, si: (bi, si, 0)
    tok_t = lambda bi, si: (bi, 0, si)
    tab = pl.BlockSpec((tm, LANES), lambda bi, si: (si, 0))
    tab_t = pl.BlockSpec((LANES, tm), lambda bi, si: (0, si))
    names = ("wqt", "wk", "wvt", "wg")
    operands, in_specs, x_spec, x_shape = _input_specs(x, prev)
    return pl.pallas_call(
        functools.partial(_odd_proj_kernel, fused=prev is not None),
        grid=(b, s // tm),
        in_specs=in_specs + [_layer_spec(w[k], layer) for k in names] + [tab, tab, tab_t, tab_t],
        out_specs=[pl.BlockSpec((None, d, tm), tok_t), pl.BlockSpec((None, tm, LANES), tok),
                   pl.BlockSpec((None, LANES, tm), tok_t), pl.BlockSpec((None, tm, d), tok)] + x_spec,
        out_shape=[jax.ShapeDtypeStruct((b, d, s), BF16), jax.ShapeDtypeStruct((b, s, LANES), BF16),
                   jax.ShapeDtypeStruct((b, LANES, s), BF16), jax.ShapeDtypeStruct((b, s, d), BF16)] + x_shape,
        compiler_params=_params(2),
        name="odd_proj",
    )(*operands, *[w[k] for k in names], tabs["c"], tabs["s"], tabs["ct"], tabs["st"])


def _swa_kernel(sink_ref, qt_ref, k_ref, vt_ref, sg_ref, o_ref):
    n_blk = qt_ref.shape[1] // QBLK
    n_pair = qt_ref.shape[0] // LANES
    win = 2 * QBLK
    qi = pl.program_id(1)
    row = lax.broadcasted_iota(jnp.int32, (LANES, QBLK), 0)
    lo_rows = row < SWA_DIM
    rel = (lax.broadcasted_iota(jnp.int32, (win, QBLK), 1) - lax.broadcasted_iota(jnp.int32, (win, QBLK), 0))
    sink_row = jnp.concatenate([jnp.full((1, QBLK), sink_ref[c] * LOG2E, F32) for c in range(2 * n_pair)],
                               axis=1)
    ones_blk = jnp.ones((16, win), BF16)

    def window_start(blk):
        return pl.multiple_of(jnp.maximum(qi * n_blk + blk - 1, 0) * QBLK, QBLK)

    def scores(blk):
        q0 = blk * QBLK
        cols = []
        for p in range(n_pair):
            qp = qt_ref[p * LANES:(p + 1) * LANES, q0:q0 + QBLK]
            zero = jnp.zeros_like(qp)
            cols.append(jnp.where(lo_rows, qp, zero))
            cols.append(jnp.where(lo_rows, zero, qp))
        s = _dot(k_ref[pl.ds(window_start(blk), win), :], jnp.concatenate(cols, axis=1))
        shift = jnp.where(qi * n_blk + blk == 0, 0, QBLK)
        diff = rel + shift
        band = (diff >= 0) & (diff < WINDOW)
        return jnp.concatenate([jnp.where(band, s[:, c * QBLK:(c + 1) * QBLK], -jnp.inf)
                                for c in range(2 * n_pair)], axis=1)

    def process(blk, s):
        q0 = blk * QBLK
        m = jnp.maximum(jnp.max(s, axis=0, keepdims=True), sink_row)
        p = jnp.exp2(s - m).astype(BF16)
        vt = jnp.concatenate([vt_ref[:, pl.ds(window_start(blk), win)], ones_blk], axis=0)
        ot = _dot(vt, p)
        inv = 1.0 / (ot[LANES:LANES + 1, :] + jnp.exp2(sink_row - m))
        for pr in range(n_pair):
            ca = slice((2 * pr) * QBLK, (2 * pr + 1) * QBLK)
            cb = slice((2 * pr + 1) * QBLK, (2 * pr + 2) * QBLK)
            pair = jnp.concatenate([ot[:SWA_DIM, ca] * inv[:, ca], ot[SWA_DIM:LANES, cb] * inv[:, cb]],
                                   axis=0).T
            cols = slice(pr * LANES, (pr + 1) * LANES)
            o_ref[q0:q0 + QBLK, cols] = (pair * sg_ref[q0:q0 + QBLK, cols].astype(F32)).astype(BF16)

    s_next = scores(0)
    for blk in range(n_blk):
        s_cur = s_next
        if blk + 1 < n_blk:
            s_next = scores(blk + 1)
        process(blk, s_cur)


def _swa_attn(sinks, qt, k, vt, sg):
    b, d, s = qt.shape
    tq = SWA_TQ
    tok = lambda bi, qi: (bi, qi, 0)
    return pl.pallas_call(
        _swa_kernel,
        grid=(b, s // tq),
        in_specs=[pl.BlockSpec(memory_space=pltpu.SMEM),
                  pl.BlockSpec((None, d, tq), lambda bi, qi: (bi, 0, qi)),
                  pl.BlockSpec((None, s, LANES), lambda bi, qi: (bi, 0, 0)),
                  pl.BlockSpec((None, LANES, s), lambda bi, qi: (bi, 0, 0)),
                  pl.BlockSpec((None, tq, d), tok)],
        out_specs=pl.BlockSpec((None, tq, d), tok),
        out_shape=jax.ShapeDtypeStruct((b, s, d), BF16),
        compiler_params=_params(2),
        name="swa_attn",
    )(sinks, qt, k, vt, sg)


def _pad_heads(w, n_heads, dim, offsets):
    k = w.shape[0]
    n_off = len(offsets)
    w4 = w.reshape(k, n_heads // n_off, n_off, dim)
    parts = [jnp.pad(w4[:, :, i, :], ((0, 0), (0, 0), (off, LANES - off - dim)))
             for i, off in enumerate(offsets)]
    return jnp.stack(parts, axis=2).reshape(k, n_heads * LANES)


def _swap_halves(w, n_heads, dim):
    k = w.shape[0]
    w3 = w.reshape(k, n_heads, dim)
    return jnp.concatenate([w3[..., dim // 2:], w3[..., :dim // 2]], axis=-1).reshape(k, n_heads * dim)


def _even_weights(w_in, q_norm, w_uq, kv_norm, w_ukv, b_f):
    sizes = (MLA_Q_RANK, MLA_KV_RANK, MLA_ROPE, FOX_HEADS * FOX_DIM, FOX_HEADS * FOX_DIM,
             FOX_HEADS * FOX_DIM, FOX_HEADS, MLA_HEADS * MLA_V + FOX_HEADS * FOX_DIM)
    cuts = [int(c) for c in np.cumsum(sizes)[:-1]]
    w_cq, w_ckv, w_kpe, w_fq, w_fk, w_fv, w_f, w_g = jnp.split(w_in, cuts, axis=1)
    place = lambda w: _pad_heads(w, 1, MLA_ROPE, (MLA_NOPE,))
    wlat = jnp.concatenate([w_cq, w_ckv, place(w_kpe), place(_swap_halves(w_kpe, 1, MLA_ROPE)),
                            _pad_heads(w_f, 1, FOX_HEADS, (0,))], axis=1)
    v_off = (0, MLA_V)
    dq = MLA_NOPE + MLA_ROPE
    uq3 = w_uq.reshape(MLA_Q_RANK, MLA_HEADS, dq)
    uq_pe = uq3[..., MLA_NOPE:].reshape(MLA_Q_RANK, MLA_HEADS * MLA_ROPE)
    uq_a = _pad_heads(w_uq, MLA_HEADS, dq, (0,))
    uq_b = _pad_heads(_swap_halves(uq_pe, MLA_HEADS, MLA_ROPE), MLA_HEADS, MLA_ROPE, (MLA_NOPE,))
    ukv3 = w_ukv.reshape(MLA_KV_RANK, MLA_HEADS, MLA_NOPE + MLA_V)
    uk = ukv3[..., :MLA_NOPE].reshape(MLA_KV_RANK, MLA_HEADS * MLA_NOPE)
    uv = ukv3[..., MLA_NOPE:].reshape(MLA_KV_RANK, MLA_HEADS * MLA_V)
    bf = lambda a: a.astype(BF16)
    return dict(wlat=bf(wlat),
                wuqt=bf(jnp.concatenate([uq_a, uq_b], axis=1).T),
                wuk=bf(_pad_heads(uk, MLA_HEADS, MLA_NOPE, (0,))),
                wuvt=bf(_pad_heads(uv, MLA_HEADS, MLA_V, v_off).T),
                wfqt=bf(w_fq.T), wfk=bf(w_fk), wfvt=bf(w_fv.T),
                wg=bf(w_g), qn=q_norm.reshape(1, -1), kvn=kv_norm.reshape(1, -1),
                bf=jnp.pad(b_f, (0, LANES - FOX_HEADS)).reshape(1, LANES))


def _pair_heads(w):
    lead = w.shape[:-1]
    w4 = w.reshape(*lead, SWA_KV_HEADS, SWA_HEADS // SWA_KV_HEADS, -1)
    return jnp.swapaxes(w4, -3, -2).reshape(*lead, -1)


def _odd_weights(w_in, sinks, w_out):
    d = SWA_HEADS * SWA_DIM
    dkv = SWA_KV_HEADS * SWA_DIM
    w_q, w_k, w_v, w_g = jnp.split(w_in, [d, d + dkv, d + 2 * dkv], axis=1)
    pair_rows = lambda a: jnp.swapaxes(
        a.reshape(SWA_KV_HEADS, SWA_HEADS // SWA_KV_HEADS, SWA_DIM, -1), 0, 1).reshape(a.shape)
    return dict(wqt=pair_rows(w_q.T).astype(BF16), wk=w_k.astype(BF16), wvt=w_v.T.astype(BF16),
                wg=_pair_heads(w_g).astype(BF16),
                sinks=_pair_heads(sinks.reshape(-1, 1).T).reshape(-1),
                w_out=pair_rows(w_out).astype(BF16))


def _rope_tables(s):
    pos = jnp.arange(s).astype(F32)

    def cs(dim):
        inv = ROPE_THETA ** (-jnp.arange(0, dim, 2, dtype=F32) / dim)
        ang = pos[:, None] * inv[None, :]
        cos, sin = jnp.cos(ang), jnp.sin(ang)
        return jnp.concatenate([cos, cos], axis=1), jnp.concatenate([-sin, sin], axis=1)

    c32, s32 = cs(MLA_ROPE)
    z = lambda n: jnp.zeros((s, n), F32)
    pad = LANES - MLA_NOPE - MLA_ROPE
    scale = (MLA_NOPE + MLA_ROPE) ** -0.5 * LOG2E
    even = dict(cqt=(jnp.concatenate([jnp.ones((s, MLA_NOPE), F32), c32, z(pad)], axis=1) * scale).T,
                sqt=(jnp.concatenate([z(MLA_NOPE), s32, z(pad)], axis=1) * scale).T,
                ck=jnp.concatenate([z(MLA_NOPE), c32, z(pad)], axis=1),
                sk=jnp.concatenate([z(MLA_NOPE), s32, z(pad)], axis=1))
    c64, s64 = cs(SWA_DIM)
    odd = dict(c=jnp.concatenate([c64, c64], axis=1), s=jnp.concatenate([s64, s64], axis=1))
    odd.update(ct=odd["c"].T, st=odd["s"].T)
    return even, odd


def kernel(x, even_w_in, even_q_norm, even_w_uq, even_kv_norm, even_w_ukv, even_b_f, even_w_out,
           even_ln_g, even_ln_b, odd_w_in, odd_sinks, odd_w_out, odd_ln_g, odd_ln_b):
    s = x.shape[1]
    even_tabs, odd_tabs = _rope_tables(s)
    we = jax.vmap(_even_weights)(even_w_in, even_q_norm, even_w_uq, even_kv_norm, even_w_ukv, even_b_f)
    wo = jax.vmap(_odd_weights)(odd_w_in, odd_sinks, odd_w_out)
    even_w_out = even_w_out.astype(BF16)
    row = lambda a: a[:, None, :]
    closing = {0: (even_w_out, row(even_ln_g), row(even_ln_b)),
               1: (wo["w_out"], row(odd_ln_g), row(odd_ln_b))}
    prev = None
    for layer in range(DEPTH):
        j = layer // 2
        if layer % 2 == 0:
            qt, k, vt, sg, *x_new = _even_proj(x, we, even_tabs, j, prev)
            x = x_new[0] if x_new else x
            o = _causal_attn(qt, k, vt, sg)
        else:
            qt, k, vt, sg, *x_new = _odd_proj(x, wo, odd_tabs, j, prev)
            x = x_new[0] if x_new else x
            o = _swa_attn(wo["sinks"][j], qt, k, vt, sg)
        prev = (o, *closing[layer % 2], j)
    o, w_out, g, bias, j = prev
    return _out_proj(o, x, w_out, g, bias, j)
```

```python
import functools
import math

import jax
import jax.numpy as jnp
import numpy as np
from jax import lax
from jax.experimental import pallas as pl
from jax.experimental.pallas import tpu as pltpu

D_MODEL = 1024
DEPTH = 4
ROPE_THETA = 10000.0
QBLK = 128
MLA_HEADS = 8
MLA_NOPE = 64
MLA_ROPE = 32
MLA_V = 64
MLA_Q_RANK = 256
MLA_KV_RANK = 128
FOX_HEADS = 8
FOX_DIM = 64
SWA_HEADS = 16
SWA_KV_HEADS = 2
SWA_DIM = 64
WINDOW = 128
RMS_EPS = 1e-6
LN_EPS = 1e-5
ALPHA = (2 * DEPTH) ** 0.25
LOG2E = math.log2(math.e)

LANES = 128
VMEM_LIMIT = 56 * 1024 * 1024
BF16 = jnp.bfloat16
F32 = jnp.float32

PROJ_TM = 512
ATT_T = 512
ATT_HG = 4
SWA_TQ = 512
EVEN_HEADS = MLA_HEADS + FOX_HEADS
EVEN_HW = EVEN_HEADS * LANES
MLA_HW = MLA_HEADS * LANES


def _dot(a, b):
    return jnp.dot(a, b, preferred_element_type=F32)


def _dot_nt(a, b):
    return lax.dot_general(a, b, (((1,), (1,)), ((), ())), preferred_element_type=F32)


def _rms(t, g):
    return t * lax.rsqrt(jnp.mean(t * t, axis=-1, keepdims=True) + RMS_EPS) * g


def _silu(g):
    return g / (1.0 + jnp.exp(-g))


def _pack3(v):
    hi = v.astype(BF16).astype(F32)
    r1 = v - hi
    mid = r1.astype(BF16).astype(F32)
    lo = r1 - mid
    return (hi + pltpu.roll(mid, FOX_HEADS, 1) + pltpu.roll(lo, 2 * FOX_HEADS, 1)).astype(BF16)


def _params(n_axes, flags=None):
    return pltpu.CompilerParams(dimension_semantics=("arbitrary",) * n_axes,
                                vmem_limit_bytes=VMEM_LIMIT, flags=flags)


def _residual_norm(o_ref, x_ref, w_ref, g_ref, b_ref):
    z = ALPHA * x_ref[...] + _dot(o_ref[...], w_ref[...])
    mu = jnp.mean(z, axis=-1, keepdims=True)
    zc = z - mu
    var = jnp.mean(zc * zc, axis=-1, keepdims=True)
    return zc * lax.rsqrt(var + LN_EPS) * g_ref[...] + b_ref[...]


def _layer_input(refs, fused):
    if not fused:
        return refs[0][...], refs[1:]
    x = _residual_norm(*refs[:5])
    return x, refs[5:]


def _even_proj_kernel(*refs, fused):
    x, refs = _layer_input(refs, fused)
    (wlat_ref, wuqt_ref, wuk_ref, wuvt_ref, wfqt_ref, wfk_ref, wfvt_ref, wg_ref, qn_ref, kvn_ref,
     bf_ref, cqt_ref, sqt_ref, ck_ref, sk_ref, qt_ref, k_ref, vt_ref, sg_ref) = refs[:19]
    carry_ref = refs[-1]
    if fused:
        refs[19][...] = x
    tm = x.shape[0]
    xb = x.astype(BF16)

    hrow = lax.broadcasted_iota(jnp.int32, (MLA_HW, 1), 0) % (2 * LANES)
    ones_col = jnp.where((hrow == MLA_V) | (hrow == LANES), 1.0, 0.0).astype(F32)
    lat = _dot(xb, wlat_ref[...])
    o = 0
    cq = lat[:, o:o + MLA_Q_RANK]
    o += MLA_Q_RANK
    ckv = lat[:, o:o + MLA_KV_RANK]
    o += MLA_KV_RANK
    kpe = lat[:, o:o + LANES] * ck_ref[...] + lat[:, o + LANES:o + 2 * LANES] * sk_ref[...]
    o += 2 * LANES
    fl = lat[:, o:o + LANES] + bf_ref[...]

    rq = _rms(cq, qn_ref[...]).astype(BF16)
    qa = _dot_nt(wuqt_ref[:MLA_HW, :], rq)
    qb = _dot_nt(wuqt_ref[MLA_HW:, :], rq)
    cqt = cqt_ref[...]
    sqt = sqt_ref[...]
    for h in range(MLA_HEADS):
        hs = slice(h * LANES, (h + 1) * LANES)
        qt_ref[hs, :] = (qa[hs, :] * cqt + qb[hs, :] * sqt).astype(BF16)
    rkv = _rms(ckv, kvn_ref[...]).astype(BF16)
    kk = _dot(rkv, wuk_ref[...])
    for h in range(MLA_HEADS):
        hs = slice(h * LANES, (h + 1) * LANES)
        k_ref[:, hs] = (kk[:, hs] + kpe).astype(BF16)
    vt_ref[:MLA_HW, :] = (_dot_nt(wuvt_ref[...], rkv) + ones_col).astype(BF16)

    lane = lax.broadcasted_iota(jnp.int32, (1, LANES), 1)
    lf = (jnp.minimum(fl, 0.0) - jnp.log1p(jnp.exp(-jnp.abs(fl)))) * LOG2E
    lf = jnp.where(lane < FOX_HEADS, lf, 0.0)
    row = lax.broadcasted_iota(jnp.int32, (tm, tm), 0)
    col = lax.broadcasted_iota(jnp.int32, (tm, tm), 1)
    tri = jnp.where(col <= row, 1.0, 0.0).astype(BF16)
    sums = _dot(tri, _pack3(lf))
    c_loc = sums + pltpu.roll(sums, LANES - FOX_HEADS, 1) + pltpu.roll(sums, LANES - 2 * FOX_HEADS, 1)
    c_loc = jnp.where(lane < FOX_HEADS, c_loc, 0.0)

    @pl.when(pl.program_id(1) == 0)
    def _():
        carry_ref[...] = jnp.zeros_like(carry_ref)

    c = c_loc + carry_ref[0:1, :]
    carry_ref[...] = jnp.broadcast_to(c[tm - 1:tm, :], carry_ref.shape)
    src = lax.broadcasted_iota(jnp.int32, (LANES, MLA_HW), 0)
    dst = lax.broadcasted_iota(jnp.int32, (LANES, MLA_HW), 1)
    src_h = src % FOX_HEADS
    c_lane = src_h * LANES + jnp.where(src_h % 2 == 0, FOX_DIM, 0) + src // FOX_HEADS
    place = jnp.where((dst == c_lane) & (src < 3 * FOX_HEADS), 1.0, 0.0).astype(BF16)
    k_extra = _dot(_pack3(c), place)

    fqt = _dot_nt(wfqt_ref[...], xb) * (FOX_DIM ** -0.5 * LOG2E)
    fk = _dot(xb, wfk_ref[...])
    fvt = _dot_nt(wfvt_ref[...], xb)
    half_row = lax.broadcasted_iota(jnp.int32, (FOX_DIM, tm), 0)
    neg_blk = jnp.where(half_row < 3, -1.0, 0.0).astype(BF16)
    ones_blk = jnp.where(half_row < 1, 1.0, 0.0).astype(BF16)
    lane_k = lax.broadcasted_iota(jnp.int32, (1, LANES), 1)
    for h in range(FOX_HEADS):
        off = (h % 2) * FOX_DIM
        base = MLA_HW + h * LANES
        data = slice(base + off, base + off + FOX_DIM)
        rest = slice(base + FOX_DIM - off, base + 2 * FOX_DIM - off)
        rows = slice(h * FOX_DIM, (h + 1) * FOX_DIM)
        qt_ref[data, :] = fqt[rows, :].astype(BF16)
        qt_ref[rest, :] = neg_blk
        vt_ref[data, :] = fvt[rows, :].astype(BF16)
        vt_ref[rest, :] = ones_blk
        pair = slice((h // 2) * LANES, (h // 2 + 1) * LANES)
        in_data = (lane_k >= off) & (lane_k < off + FOX_DIM)
        k_ref[:, base:base + LANES] = jnp.where(
            in_data, fk[:, pair], k_extra[:, h * LANES:(h + 1) * LANES]).astype(BF16)

    sg_ref[...] = _silu(_dot(xb, wg_ref[...])).astype(BF16)


def _input_specs(x, prev):
    b, s, d = x.shape
    tok = pl.BlockSpec((None, PROJ_TM, d), lambda bi, si: (bi, si, 0))
    if prev is None:
        return [x], [tok], [], []
    o, w_out, g, bias, layer = prev
    specs = [tok, tok] + [_layer_spec(a, layer) for a in (w_out, g, bias)]
    return [o, x, w_out, g, bias], specs, [tok], [jax.ShapeDtypeStruct((b, s, d), F32)]


def _even_proj(x, w, tabs, layer, prev=None):
    b, s, d = x.shape
    tm = PROJ_TM
    tok = lambda bi, si: (bi, si, 0)
    tok_t = lambda bi, si: (bi, 0, si)
    names = ("wlat", "wuqt", "wuk", "wuvt", "wfqt", "wfk", "wfvt", "wg", "qn", "kvn", "bf")
    operands, in_specs, x_spec, x_shape = _input_specs(x, prev)
    in_specs += [_layer_spec(w[k], layer) for k in names]
    in_specs += [pl.BlockSpec((LANES, tm), lambda bi, si: (0, si))] * 2
    in_specs += [pl.BlockSpec((tm, LANES), lambda bi, si: (si, 0))] * 2
    out_specs = [pl.BlockSpec((None, EVEN_HW, tm), tok_t), pl.BlockSpec((None, tm, EVEN_HW), tok),
                 pl.BlockSpec((None, EVEN_HW, tm), tok_t), pl.BlockSpec((None, tm, d), tok)]
    out_shape = [jax.ShapeDtypeStruct((b, EVEN_HW, s), BF16), jax.ShapeDtypeStruct((b, s, EVEN_HW), BF16),
                 jax.ShapeDtypeStruct((b, EVEN_HW, s), BF16), jax.ShapeDtypeStruct((b, s, d), BF16)]
    return pl.pallas_call(
        functools.partial(_even_proj_kernel, fused=prev is not None),
        grid=(b, s // tm),
        in_specs=in_specs,
        out_specs=out_specs + x_spec,
        out_shape=out_shape + x_shape,
        scratch_shapes=[pltpu.VMEM((8, LANES), F32)],
        compiler_params=_params(2),
        name="even_proj",
    )(*operands, *[w[k] for k in names], tabs["cqt"], tabs["sqt"], tabs["ck"], tabs["sk"])


def _causal_attn_kernel(qt_ref, qtn_ref, k_ref, vt_ref, sg_ref, o_ref,
                        sa_ref, sb_ref, sc_ref, mc_ref):
    t = qt_ref.shape[1]
    hf = t // 2
    n_heads = qt_ref.shape[0] // LANES
    qi = pl.program_id(2)
    heads = [slice(h * LANES, (h + 1) * LANES) for h in range(n_heads)]

    def key_slice(kj, start=0, size=None):
        size = t if size is None else size
        return pl.ds(pl.multiple_of(kj * t + start, size), size)

    def col_max(s):
        return jnp.max(s, axis=0, keepdims=True)

    def causal_mask(s):
        n = s.shape[0]
        keep = (lax.broadcasted_iota(jnp.int32, (n, n), 0) <= lax.broadcasted_iota(jnp.int32, (n, n), 1))
        return jnp.where(keep, s, -jnp.inf)

    def scores_head(q_ref, kj, buf, h):
        s = _dot(k_ref[key_slice(kj), heads[h]], q_ref[heads[h], :])
        buf[h] = s
        return col_max(s)

    def diag_scores_head(buf, h):
        early = _dot(k_ref[key_slice(qi, 0, hf), heads[h]], qt_ref[heads[h], :])
        late = _dot(k_ref[key_slice(qi, hf, hf), heads[h]], qt_ref[heads[h], hf:])
        early = jnp.concatenate([causal_mask(early[:, :hf]), early[:, hf:]], axis=1)
        late = causal_mask(late)
        buf[h, :hf, :] = early
        buf[h, hf:, hf:] = late
        return jnp.concatenate([col_max(early[:, :hf]),
                                jnp.maximum(col_max(early[:, hf:]), col_max(late))], axis=1)

    def process_head(kj, s, h, tile_max, m, acc):
        m_new = jnp.maximum(m, tile_max)
        p = jnp.exp2(s - m_new).astype(BF16)
        return m_new, jnp.exp2(m - m_new) * acc + _dot(vt_ref[heads[h], key_slice(kj)], p)

    def diag_process_head(buf, h, tile_max, m, acc):
        m_new = jnp.maximum(m, tile_max)
        p_early = jnp.exp2(buf[h, :hf, :] - m_new).astype(BF16)
        p_late = jnp.exp2(buf[h, hf:, hf:] - m_new[:, hf:]).astype(BF16)
        pv_early = _dot(vt_ref[heads[h], key_slice(qi, 0, hf)], p_early)
        pv_late = _dot(vt_ref[heads[h], key_slice(qi, hf, hf)], p_late)
        scaled = jnp.exp2(m - m_new) * acc + pv_early
        return m_new, jnp.concatenate([scaled[:, :hf], scaled[:, hf:] + pv_late], axis=1)

    def stage(kj, cur, nxt, tile_max, state, diag_next=False):
        next_max, out = [], []
        for h in range(n_heads):
            next_max.append(diag_scores_head(nxt, h) if diag_next else scores_head(qt_ref, kj + 1, nxt, h))
            out.append(process_head(kj, cur[h], h, tile_max[h], *state[h]))
        return tuple(next_max), tuple(out)

    def two_stages(i, carry):
        tile_max, state = carry
        tile_max, state = stage(2 * i + 1, sa_ref, sb_ref, tile_max, state)
        return stage(2 * i + 2, sb_ref, sa_ref, tile_max, state)

    def last_tile(buf, tile_max, state):
        out = []
        for h in range(n_heads):
            mc_ref[h, 0:1, :] = scores_head(qtn_ref, 0, sc_ref, h)
            out.append(diag_process_head(buf, h, tile_max[h], *state[h]))
        return tuple(out)

    def first_stage(state, diag_next=False):
        tile_max = tuple(mc_ref[h, 0:1, :] for h in range(n_heads))
        return stage(0, sc_ref, sa_ref, tile_max, state, diag_next)

    def case_first(state):
        tile_max = tuple(diag_scores_head(sa_ref, h) for h in range(n_heads))
        return last_tile(sa_ref, tile_max, state)

    def case_second(state):
        tile_max, state = first_stage(state, diag_next=True)
        return last_tile(sa_ref, tile_max, state)

    def case_even(state):
        tile_max, state = first_stage(state)
        tile_max, state = lax.fori_loop(0, (qi - 2) // 2, two_stages, (tile_max, state))
        tile_max, state = stage(qi - 1, sa_ref, sb_ref, tile_max, state, diag_next=True)
        return last_tile(sb_ref, tile_max, state)

    def case_odd(state):
        tile_max, state = first_stage(state)
        tile_max, state = lax.fori_loop(0, (qi - 3) // 2, two_stages, (tile_max, state))
        tile_max, state = stage(qi - 2, sa_ref, sb_ref, tile_max, state)
        tile_max, state = stage(qi - 1, sb_ref, sa_ref, tile_max, state, diag_next=True)
        return last_tile(sa_ref, tile_max, state)

    def case_later(state):
        return lax.cond(qi % 2 == 0, case_even, case_odd, state)

    def case_not_first(state):
        return lax.cond(qi == 1, case_second, case_later, state)

    init = tuple((jnp.full((1, t), -jnp.inf, F32), jnp.zeros((LANES, t), F32)) for _ in range(n_heads))
    state = lax.cond(qi == 0, case_first, case_not_first, init)

    row = lax.broadcasted_iota(jnp.int32, (LANES, t), 0)
    for hp in range(n_heads // 2):
        a0 = state[2 * hp][1]
        a1 = state[2 * hp + 1][1]
        pair_t = jnp.where(row < MLA_V, a0 / a0[MLA_V:MLA_V + 1, :], a1 / a1[0:1, :])
        cols = slice(hp * LANES, (hp + 1) * LANES)
        o_ref[:, cols] = (pair_t.T * sg_ref[:, cols].astype(F32)).astype(BF16)


def _causal_attn(qt, k, vt, sg):
    b, s, hw = k.shape
    t = ATT_T
    nq = s // t
    gw = ATT_HG * LANES
    ow = ATT_HG * MLA_V
    score_buf = pltpu.VMEM((ATT_HG, t, t), F32)
    return pl.pallas_call(
        _causal_attn_kernel,
        grid=(b, hw // gw, nq),
        in_specs=[pl.BlockSpec((None, gw, t), lambda bi, g, qi: (bi, g, qi)),
                  pl.BlockSpec((None, gw, t), lambda bi, g, qi: (bi, g, jnp.minimum(qi + 1, nq - 1))),
                  pl.BlockSpec((None, s, gw), lambda bi, g, qi: (bi, 0, g)),
                  pl.BlockSpec((None, gw, s), lambda bi, g, qi: (bi, g, 0)),
                  pl.BlockSpec((None, t, ow), lambda bi, g, qi: (bi, qi, g))],
        out_specs=pl.BlockSpec((None, t, ow), lambda bi, g, qi: (bi, qi, g)),
        out_shape=jax.ShapeDtypeStruct((b, s, (hw // LANES) * MLA_V), BF16),
        scratch_shapes=[score_buf, score_buf, score_buf, pltpu.VMEM((ATT_HG, 8, t), F32)],
        compiler_params=_params(3),
        name="causal_attn",
    )(qt, qt, k, vt, sg)


def _out_kernel(o_ref, x_ref, w_ref, g_ref, b_ref, y_ref):
    y_ref[...] = _residual_norm(o_ref, x_ref, w_ref, g_ref, b_ref)


def _out_proj(o, x, w_out, g, bias, layer):
    b, s, d = x.shape
    tm = PROJ_TM
    tok = lambda bi, si: (bi, si, 0)
    return pl.pallas_call(
        _out_kernel,
        grid=(b, s // tm),
        in_specs=[pl.BlockSpec((None, tm, d), tok), pl.BlockSpec((None, tm, d), tok),
                  _layer_spec(w_out, layer), _layer_spec(g, layer), _layer_spec(bias, layer)],
        out_specs=pl.BlockSpec((None, tm, d), tok),
        out_shape=jax.ShapeDtypeStruct((b, s, d), F32),
        compiler_params=_params(2),
        name="out_proj_ln",
    )(o, x, w_out, g, bias)


def _odd_proj_kernel(*refs, fused):
    x, refs = _layer_input(refs, fused)
    (wqt_ref, wk_ref, wvt_ref, wg_ref, c_ref, s_ref, ct_ref, st_ref,
     qt_ref, k_ref, vt_ref, sg_ref) = refs[:12]
    if fused:
        refs[12][...] = x
    d = SWA_HEADS * SWA_DIM
    half = SWA_DIM // 2
    xb = x.astype(BF16)

    kk = _dot(xb, wk_ref[...])
    lane = lax.broadcasted_iota(jnp.int32, (1, LANES), 1)
    first_half = (lane % SWA_DIM) < half
    k_sw = jnp.where(first_half, pltpu.roll(kk, LANES - half, 1), pltpu.roll(kk, half, 1))
    k_ref[...] = (kk * c_ref[...] + k_sw * s_ref[...]).astype(BF16)

    qq = _dot_nt(wqt_ref[...], xb)
    ctt = ct_ref[...]
    stt = st_ref[...]
    scale = SWA_DIM ** -0.5 * LOG2E
    for p in range(d // LANES):
        blk = qq[p * LANES:(p + 1) * LANES, :]
        q_sw = jnp.concatenate([blk[half:2 * half], blk[:half], blk[3 * half:], blk[2 * half:3 * half]], axis=0)
        qt_ref[p * LANES:(p + 1) * LANES, :] = ((blk * ctt + q_sw * stt) * scale).astype(BF16)

    vt_ref[...] = _dot_nt(wvt_ref[...], xb).astype(BF16)
    sg_ref[...] = _silu(_dot(xb, wg_ref[...])).astype(BF16)


def _layer_spec(a, layer):
    return pl.BlockSpec((None,) + a.shape[1:], lambda bi, si: (layer,) + (0,) * (a.ndim - 1))


def _odd_proj(x, w, tabs, layer, prev=None):
    b, s, d = x.shape
    tm = PROJ_TM
    tok = lambda bi, si: (bi, si, 0)
    tok_t = lambda bi, si: (bi, 0, si)
    tab = pl.BlockSpec((tm, LANES), lambda bi, si: (si, 0))
    tab_t = pl.BlockSpec((LANES, tm), lambda bi, si: (0, si))
    names = ("wqt", "wk", "wvt", "wg")
    operands, in_specs, x_spec, x_shape = _input_specs(x, prev)
    return pl.pallas_call(
        functools.partial(_odd_proj_kernel, fused=prev is not None),
        grid=(b, s // tm),
        in_specs=in_specs + [_layer_spec(w[k], layer) for k in names] + [tab, tab, tab_t, tab_t],
        out_specs=[pl.BlockSpec((None, d, tm), tok_t), pl.BlockSpec((None, tm, LANES), tok),
                   pl.BlockSpec((None, LANES, tm), tok_t), pl.BlockSpec((None, tm, d), tok)] + x_spec,
        out_shape=[jax.ShapeDtypeStruct((b, d, s), BF16), jax.ShapeDtypeStruct((b, s, LANES), BF16),
                   jax.ShapeDtypeStruct((b, LANES, s), BF16), jax.ShapeDtypeStruct((b, s, d), BF16)] + x_shape,
        compiler_params=_params(2),
        name="odd_proj",
    )(*operands, *[w[k] for k in names], tabs["c"], tabs["s"], tabs["ct"], tabs["st"])


def _swa_kernel(sink_ref, qt_ref, k_ref, vt_ref, sg_ref, o_ref):
    n_blk = qt_ref.shape[1] // QBLK
    n_pair = qt_ref.shape[0] // LANES
    win = 2 * QBLK
    qi = pl.program_id(1)
    row = lax.broadcasted_iota(jnp.int32, (LANES, QBLK), 0)
    lo_rows = row < SWA_DIM
    rel = (lax.broadcasted_iota(jnp.int32, (win, QBLK), 1) - lax.broadcasted_iota(jnp.int32, (win, QBLK), 0))
    sink_row = jnp.concatenate([jnp.full((1, QBLK), sink_ref[c] * LOG2E, F32) for c in range(2 * n_pair)],
                               axis=1)
    ones_blk = jnp.ones((16, win), BF16)

    def window_start(blk):
        return pl.multiple_of(jnp.maximum(qi * n_blk + blk - 1, 0) * QBLK, QBLK)

    def scores(blk):
        q0 = blk * QBLK
        cols = []
        for p in range(n_pair):
            qp = qt_ref[p * LANES:(p + 1) * LANES, q0:q0 + QBLK]
            zero = jnp.zeros_like(qp)
            cols.append(jnp.where(lo_rows, qp, zero))
            cols.append(jnp.where(lo_rows, zero, qp))
        s = _dot(k_ref[pl.ds(window_start(blk), win), :], jnp.concatenate(cols, axis=1))
        shift = jnp.where(qi * n_blk + blk == 0, 0, QBLK)
        diff = rel + shift
        band = (diff >= 0) & (diff < WINDOW)
        return jnp.concatenate([jnp.where(band, s[:, c * QBLK:(c + 1) * QBLK], -jnp.inf)
                                for c in range(2 * n_pair)], axis=1)

    def process(blk, s):
        q0 = blk * QBLK
        m = jnp.maximum(jnp.max(s, axis=0, keepdims=True), sink_row)
        p = jnp.exp2(s - m).astype(BF16)
        vt = jnp.concatenate([vt_ref[:, pl.ds(window_start(blk), win)], ones_blk], axis=0)
        ot = _dot(vt, p)
        inv = 1.0 / (ot[LANES:LANES + 1, :] + jnp.exp2(sink_row - m))
        for pr in range(n_pair):
            ca = slice((2 * pr) * QBLK, (2 * pr + 1) * QBLK)
            cb = slice((2 * pr + 1) * QBLK, (2 * pr + 2) * QBLK)
            pair = jnp.concatenate([ot[:SWA_DIM, ca] * inv[:, ca], ot[SWA_DIM:LANES, cb] * inv[:, cb]],
                                   axis=0).T
            cols = slice(pr * LANES, (pr + 1) * LANES)
            o_ref[q0:q0 + QBLK, cols] = (pair * sg_ref[q0:q0 + QBLK, cols].astype(F32)).astype(BF16)

    s_next = scores(0)
    for blk in range(n_blk):
        s_cur = s_next
        if blk + 1 < n_blk:
            s_next = scores(blk + 1)
        process(blk, s_cur)


def _swa_attn(sinks, qt, k, vt, sg):
    b, d, s = qt.shape
    tq = SWA_TQ
    tok = lambda bi, qi: (bi, qi, 0)
    return pl.pallas_call(
        _swa_kernel,
        grid=(b, s // tq),
        in_specs=[pl.BlockSpec(memory_space=pltpu.SMEM),
                  pl.BlockSpec((None, d, tq), lambda bi, qi: (bi, 0, qi)),
                  pl.BlockSpec((None, s, LANES), lambda bi, qi: (bi, 0, 0)),
                  pl.BlockSpec((None, LANES, s), lambda bi, qi: (bi, 0, 0)),
                  pl.BlockSpec((None, tq, d), tok)],
        out_specs=pl.BlockSpec((None, tq, d), tok),
        out_shape=jax.ShapeDtypeStruct((b, s, d), BF16),
        compiler_params=_params(2),
        name="swa_attn",
    )(sinks, qt, k, vt, sg)


def _pad_heads(w, n_heads, dim, offsets):
    k = w.shape[0]
    n_off = len(offsets)
    w4 = w.reshape(k, n_heads // n_off, n_off, dim)
    parts = [jnp.pad(w4[:, :, i, :], ((0, 0), (0, 0), (off, LANES - off - dim)))
             for i, off in enumerate(offsets)]
    return jnp.stack(parts, axis=2).reshape(k, n_heads * LANES)


def _swap_halves(w, n_heads, dim):
    k = w.shape[0]
    w3 = w.reshape(k, n_heads, dim)
    return jnp.concatenate([w3[..., dim // 2:], w3[..., :dim // 2]], axis=-1).reshape(k, n_heads * dim)


def _even_weights(w_in, q_norm, w_uq, kv_norm, w_ukv, b_f):
    sizes = (MLA_Q_RANK, MLA_KV_RANK, MLA_ROPE, FOX_HEADS * FOX_DIM, FOX_HEADS * FOX_DIM,
             FOX_HEADS * FOX_DIM, FOX_HEADS, MLA_HEADS * MLA_V + FOX_HEADS * FOX_DIM)
    cuts = [int(c) for c in np.cumsum(sizes)[:-1]]
    w_cq, w_ckv, w_kpe, w_fq, w_fk, w_fv, w_f, w_g = jnp.split(w_in, cuts, axis=1)
    place = lambda w: _pad_heads(w, 1, MLA_ROPE, (MLA_NOPE,))
    wlat = jnp.concatenate([w_cq, w_ckv, place(w_kpe), place(_swap_halves(w_kpe, 1, MLA_ROPE)),
                            _pad_heads(w_f, 1, FOX_HEADS, (0,))], axis=1)
    v_off = (0, MLA_V)
    dq = MLA_NOPE + MLA_ROPE
    uq3 = w_uq.reshape(MLA_Q_RANK, MLA_HEADS, dq)
    uq_pe = uq3[..., MLA_NOPE:].reshape(MLA_Q_RANK, MLA_HEADS * MLA_ROPE)
    uq_a = _pad_heads(w_uq, MLA_HEADS, dq, (0,))
    uq_b = _pad_heads(_swap_halves(uq_pe, MLA_HEADS, MLA_ROPE), MLA_HEADS, MLA_ROPE, (MLA_NOPE,))
    ukv3 = w_ukv.reshape(MLA_KV_RANK, MLA_HEADS, MLA_NOPE + MLA_V)
    uk = ukv3[..., :MLA_NOPE].reshape(MLA_KV_RANK, MLA_HEADS * MLA_NOPE)
    uv = ukv3[..., MLA_NOPE:].reshape(MLA_KV_RANK, MLA_HEADS * MLA_V)
    bf = lambda a: a.astype(BF16)
    return dict(wlat=bf(wlat),
                wuqt=bf(jnp.concatenate([uq_a, uq_b], axis=1).T),
                wuk=bf(_pad_heads(uk, MLA_HEADS, MLA_NOPE, (0,))),
                wuvt=bf(_pad_heads(uv, MLA_HEADS, MLA_V, v_off).T),
                wfqt=bf(w_fq.T), wfk=bf(w_fk), wfvt=bf(w_fv.T),
                wg=bf(w_g), qn=q_norm.reshape(1, -1), kvn=kv_norm.reshape(1, -1),
                bf=jnp.pad(b_f, (0, LANES - FOX_HEADS)).reshape(1, LANES))


def _pair_heads(w):
    lead = w.shape[:-1]
    w4 = w.reshape(*lead, SWA_KV_HEADS, SWA_HEADS // SWA_KV_HEADS, -1)
    return jnp.swapaxes(w4, -3, -2).reshape(*lead, -1)


def _odd_weights(w_in, sinks, w_out):
    d = SWA_HEADS * SWA_DIM
    dkv = SWA_KV_HEADS * SWA_DIM
    w_q, w_k, w_v, w_g = jnp.split(w_in, [d, d + dkv, d + 2 * dkv], axis=1)
    pair_rows = lambda a: jnp.swapaxes(
        a.reshape(SWA_KV_HEADS, SWA_HEADS // SWA_KV_HEADS, SWA_DIM, -1), 0, 1).reshape(a.shape)
    return dict(wqt=pair_rows(w_q.T).astype(BF16), wk=w_k.astype(BF16), wvt=w_v.T.astype(BF16),
                wg=_pair_heads(w_g).astype(BF16),
                sinks=_pair_heads(sinks.reshape(-1, 1).T).reshape(-1),
                w_out=pair_rows(w_out).astype(BF16))


def _rope_tables(s):
    pos = jnp.arange(s).astype(F32)

    def cs(dim):
        inv = ROPE_THETA ** (-jnp.arange(0, dim, 2, dtype=F32) / dim)
        ang = pos[:, None] * inv[None, :]
        cos, sin = jnp.cos(ang), jnp.sin(ang)
        return jnp.concatenate([cos, cos], axis=1), jnp.concatenate([-sin, sin], axis=1)

    c32, s32 = cs(MLA_ROPE)
    z = lambda n: jnp.zeros((s, n), F32)
    pad = LANES - MLA_NOPE - MLA_ROPE
    scale = (MLA_NOPE + MLA_ROPE) ** -0.5 * LOG2E
    even = dict(cqt=(jnp.concatenate([jnp.ones((s, MLA_NOPE), F32), c32, z(pad)], axis=1) * scale).T,
                sqt=(jnp.concatenate([z(MLA_NOPE), s32, z(pad)], axis=1) * scale).T,
                ck=jnp.concatenate([z(MLA_NOPE), c32, z(pad)], axis=1),
                sk=jnp.concatenate([z(MLA_NOPE), s32, z(pad)], axis=1))
    c64, s64 = cs(SWA_DIM)
    odd = dict(c=jnp.concatenate([c64, c64], axis=1), s=jnp.concatenate([s64, s64], axis=1))
    odd.update(ct=odd["c"].T, st=odd["s"].T)
    return even, odd


def kernel(x, even_w_in, even_q_norm, even_w_uq, even_kv_norm, even_w_ukv, even_b_f, even_w_out,
           even_ln_g, even_ln_b, odd_w_in, odd_sinks, odd_w_out, odd_ln_g, odd_ln_b):
    s = x.shape[1]
    even_tabs, odd_tabs = _rope_tables(s)
    we = jax.vmap(_even_weights)(even_w_in, even_q_norm, even_w_uq, even_kv_norm, even_w_ukv, even_b_f)
    wo = jax.vmap(_odd_weights)(odd_w_in, odd_sinks, odd_w_out)
    even_w_out = even_w_out.astype(BF16)
    row = lambda a: a[:, None, :]
    closing = {0: (even_w_out, row(even_ln_g), row(even_ln_b)),
               1: (wo["w_out"], row(odd_ln_g), row(odd_ln_b))}
    prev = None
    for layer in range(DEPTH):
        j = layer // 2
        if layer % 2 == 0:
            qt, k, vt, sg, *x_new = _even_proj(x, we, even_tabs, j, prev)
            x = x_new[0] if x_new else x
            o = _causal_attn(qt, k, vt, sg)
        else:
            qt, k, vt, sg, *x_new = _odd_proj(x, wo, odd_tabs, j, prev)
            x = x_new[0] if x_new else x
            o = _swa_attn(wo["sinks"][j], qt, k, vt, sg)
        prev = (o, *closing[layer % 2], j)
    o, w_out, g, bias, j = prev
    return _out_proj(o, x, w_out, g, bias, j)
```

```python
import functools
import math

import jax
import jax.numpy as jnp
import numpy as np
from jax import lax
from jax.experimental import pallas as pl
from jax.experimental.pallas import tpu as pltpu

D_MODEL = 1024
DEPTH = 4
ROPE_THETA = 10000.0
QBLK = 128
MLA_HEADS = 8
MLA_NOPE = 64
MLA_ROPE = 32
MLA_V = 64
MLA_Q_RANK = 256
MLA_KV_RANK = 128
FOX_HEADS = 8
FOX_DIM = 64
SWA_HEADS = 16
SWA_KV_HEADS = 2
SWA_DIM = 64
WINDOW = 128
RMS_EPS = 1e-6
LN_EPS = 1e-5
ALPHA = (2 * DEPTH) ** 0.25
LOG2E = math.log2(math.e)

LANES = 128
VMEM_LIMIT = 56 * 1024 * 1024
BF16 = jnp.bfloat16
F32 = jnp.float32

PROJ_TM = 512
ATT_T = 512
ATT_HG = 4
SWA_TQ = 512
EVEN_HEADS = MLA_HEADS + FOX_HEADS
EVEN_HW = EVEN_HEADS * LANES
MLA_HW = MLA_HEADS * LANES


def _dot(a, b):
    return jnp.dot(a, b, preferred_element_type=F32)


def _dot_nt(a, b):
    return lax.dot_general(a, b, (((1,), (1,)), ((), ())), preferred_element_type=F32)


def _rms(t, g):
    return t * lax.rsqrt(jnp.mean(t * t, axis=-1, keepdims=True) + RMS_EPS) * g


def _silu(g):
    return g / (1.0 + jnp.exp(-g))


def _pack3(v):
    hi = v.astype(BF16).astype(F32)
    r1 = v - hi
    mid = r1.astype(BF16).astype(F32)
    lo = r1 - mid
    return (hi + pltpu.roll(mid, FOX_HEADS, 1) + pltpu.roll(lo, 2 * FOX_HEADS, 1)).astype(BF16)


def _params(n_axes, flags=None):
    return pltpu.CompilerParams(dimension_semantics=("arbitrary",) * n_axes,
                                vmem_limit_bytes=VMEM_LIMIT, flags=flags)


def _residual_norm(o_ref, x_ref, w_ref, g_ref, b_ref):
    z = ALPHA * x_ref[...] + _dot(o_ref[...], w_ref[...])
    mu = jnp.mean(z, axis=-1, keepdims=True)
    zc = z - mu
    var = jnp.mean(zc * zc, axis=-1, keepdims=True)
    return zc * lax.rsqrt(var + LN_EPS) * g_ref[...] + b_ref[...]


def _layer_input(refs, fused):
    if not fused:
        return refs[0][...], refs[1:]
    x = _residual_norm(*refs[:5])
    return x, refs[5:]


def _even_proj_kernel(*refs, fused):
    x, refs = _layer_input(refs, fused)
    (wlat_ref, wuqt_ref, wuk_ref, wuvt_ref, wfqt_ref, wfk_ref, wfvt_ref, wg_ref, qn_ref, kvn_ref,
     bf_ref, cqt_ref, sqt_ref, ck_ref, sk_ref, qt_ref, k_ref, vt_ref, sg_ref) = refs[:19]
    carry_ref = refs[-1]
    if fused:
        refs[19][...] = x
    tm = x.shape[0]
    xb = x.astype(BF16)

    hrow = lax.broadcasted_iota(jnp.int32, (MLA_HW, 1), 0) % (2 * LANES)
    ones_col = jnp.where((hrow == MLA_V) | (hrow == LANES), 1.0, 0.0).astype(F32)
    lat = _dot(xb, wlat_ref[...])
    o = 0
    cq = lat[:, o:o + MLA_Q_RANK]
    o += MLA_Q_RANK
    ckv = lat[:, o:o + MLA_KV_RANK]
    o += MLA_KV_RANK
    kpe = lat[:, o:o + LANES] * ck_ref[...] + lat[:, o + LANES:o + 2 * LANES] * sk_ref[...]
    o += 2 * LANES
    fl = lat[:, o:o + LANES] + bf_ref[...]

    rq = _rms(cq, qn_ref[...]).astype(BF16)
    qa = _dot_nt(wuqt_ref[:MLA_HW, :], rq)
    qb = _dot_nt(wuqt_ref[MLA_HW:, :], rq)
    cqt = cqt_ref[...]
    sqt = sqt_ref[...]
    for h in range(MLA_HEADS):
        hs = slice(h * LANES, (h + 1) * LANES)
        qt_ref[hs, :] = (qa[hs, :] * cqt + qb[hs, :] * sqt).astype(BF16)
    rkv = _rms(ckv, kvn_ref[...]).astype(BF16)
    kk = _dot(rkv, wuk_ref[...])
    for h in range(MLA_HEADS):
        hs = slice(h * LANES, (h + 1) * LANES)
        k_ref[:, hs] = (kk[:, hs] + kpe).astype(BF16)
    vt_ref[:MLA_HW, :] = (_dot_nt(wuvt_ref[...], rkv) + ones_col).astype(BF16)

    lane = lax.broadcasted_iota(jnp.int32, (1, LANES), 1)
    lf = (jnp.minimum(fl, 0.0) - jnp.log1p(jnp.exp(-jnp.abs(fl)))) * LOG2E
    lf = jnp.where(lane < FOX_HEADS, lf, 0.0)
    row = lax.broadcasted_iota(jnp.int32, (tm, tm), 0)
    col = lax.broadcasted_iota(jnp.int32, (tm, tm), 1)
    tri = jnp.where(col <= row, 1.0, 0.0).astype(BF16)
    sums = _dot(tri, _pack3(lf))
    c_loc = sums + pltpu.roll(sums, LANES - FOX_HEADS, 1) + pltpu.roll(sums, LANES - 2 * FOX_HEADS, 1)
    c_loc = jnp.where(lane < FOX_HEADS, c_loc, 0.0)

    @pl.when(pl.program_id(1) == 0)
    def _():
        carry_ref[...] = jnp.zeros_like(carry_ref)

    c = c_loc + carry_ref[0:1, :]
    carry_ref[...] = jnp.broadcast_to(c[tm - 1:tm, :], carry_ref.shape)
    src = lax.broadcasted_iota(jnp.int32, (LANES, MLA_HW), 0)
    dst = lax.broadcasted_iota(jnp.int32, (LANES, MLA_HW), 1)
    src_h = src % FOX_HEADS
    c_lane = src_h * LANES + jnp.where(src_h % 2 == 0, FOX_DIM, 0) + src // FOX_HEADS
    place = jnp.where((dst == c_lane) & (src < 3 * FOX_HEADS), 1.0, 0.0).astype(BF16)
    k_extra = _dot(_pack3(c), place)

    fqt = _dot_nt(wfqt_ref[...], xb) * (FOX_DIM ** -0.5 * LOG2E)
    fk = _dot(xb, wfk_ref[...])
    fvt = _dot_nt(wfvt_ref[...], xb)
    half_row = lax.broadcasted_iota(jnp.int32, (FOX_DIM, tm), 0)
    neg_blk = jnp.where(half_row < 3, -1.0, 0.0).astype(BF16)
    ones_blk = jnp.where(half_row < 1, 1.0, 0.0).astype(BF16)
    lane_k = lax.broadcasted_iota(jnp.int32, (1, LANES), 1)
    for h in range(FOX_HEADS):
        off = (h % 2) * FOX_DIM
        base = MLA_HW + h * LANES
        data = slice(base + off, base + off + FOX_DIM)
        rest = slice(base + FOX_DIM - off, base + 2 * FOX_DIM - off)
        rows = slice(h * FOX_DIM, (h + 1) * FOX_DIM)
        qt_ref[data, :] = fqt[rows, :].astype(BF16)
        qt_ref[rest, :] = neg_blk
        vt_ref[data, :] = fvt[rows, :].astype(BF16)
        vt_ref[rest, :] = ones_blk
        pair = slice((h // 2) * LANES, (h // 2 + 1) * LANES)
        in_data = (lane_k >= off) & (lane_k < off + FOX_DIM)
        k_ref[:, base:base + LANES] = jnp.where(
            in_data, fk[:, pair], k_extra[:, h * LANES:(h + 1) * LANES]).astype(BF16)

    sg_ref[...] = _silu(_dot(xb, wg_ref[...])).astype(BF16)


def _input_specs(x, prev):
    b, s, d = x.shape
    tok = pl.BlockSpec((None, PROJ_TM, d), lambda bi, si: (bi, si, 0))
    if prev is None:
        return [x], [tok], [], []
    o, w_out, g, bias, layer = prev
    specs = [tok, tok] + [_layer_spec(a, layer) for a in (w_out, g, bias)]
    return [o, x, w_out, g, bias], specs, [tok], [jax.ShapeDtypeStruct((b, s, d), F32)]


def _even_proj(x, w, tabs, layer, prev=None):
    b, s, d = x.shape
    tm = PROJ_TM
    tok = lambda bi, si: (bi, si, 0)
    tok_t = lambda bi, si: (bi, 0, si)
    names = ("wlat", "wuqt", "wuk", "wuvt", "wfqt", "wfk", "wfvt", "wg", "qn", "kvn", "bf")
    operands, in_specs, x_spec, x_shape = _input_specs(x, prev)
    in_specs += [_layer_spec(w[k], layer) for k in names]
    in_specs += [pl.BlockSpec((LANES, tm), lambda bi, si: (0, si))] * 2
    in_specs += [pl.BlockSpec((tm, LANES), lambda bi, si: (si, 0))] * 2
    out_specs = [pl.BlockSpec((None, EVEN_HW, tm), tok_t), pl.BlockSpec((None, tm, EVEN_HW), tok),
                 pl.BlockSpec((None, EVEN_HW, tm), tok_t), pl.BlockSpec((None, tm, d), tok)]
    out_shape = [jax.ShapeDtypeStruct((b, EVEN_HW, s), BF16), jax.ShapeDtypeStruct((b, s, EVEN_HW), BF16),
                 jax.ShapeDtypeStruct((b, EVEN_HW, s), BF16), jax.ShapeDtypeStruct((b, s, d), BF16)]
    return pl.pallas_call(
        functools.partial(_even_proj_kernel, fused=prev is not None),
        grid=(b, s // tm),
        in_specs=in_specs,
        out_specs=out_specs + x_spec,
        out_shape=out_shape + x_shape,
        scratch_shapes=[pltpu.VMEM((8, LANES), F32)],
        compiler_params=_params(2),
        name="even_proj",
    )(*operands, *[w[k] for k in names], tabs["cqt"], tabs["sqt"], tabs["ck"], tabs["sk"])


def _attn_query_tile(qi, qt_ref, k_ref, vt_ref, sg_ref, o_ref, sa_ref, sb_ref, sc_ref, mc_ref):
    t = ATT_T
    hf = t // 2
    n_heads = qt_ref.shape[0] // LANES
    nq = qt_ref.shape[1] // t
    heads = [slice(h * LANES, (h + 1) * LANES) for h in range(n_heads)]

    def q_slice(tile, start=0, size=None):
        size = t if size is None else size
        return pl.ds(pl.multiple_of(tile * t + start, size), size)

    def key_slice(kj, start=0, size=None):
        size = t if size is None else size
        return pl.ds(pl.multiple_of(kj * t + start, size), size)

    def col_max(s):
        return jnp.max(s, axis=0, keepdims=True)

    def causal_mask(s):
        n = s.shape[0]
        keep = (lax.broadcasted_iota(jnp.int32, (n, n), 0) <= lax.broadcasted_iota(jnp.int32, (n, n), 1))
        return jnp.where(keep, s, -jnp.inf)

    def scores_head(q_tile, kj, buf, h):
        s = _dot(k_ref[key_slice(kj), heads[h]], qt_ref[heads[h], q_slice(q_tile)])
        buf[h] = s
        return col_max(s)

    def diag_scores_head(buf, h):
        early = _dot(k_ref[key_slice(qi, 0, hf), heads[h]], qt_ref[heads[h], q_slice(qi)])
        late = _dot(k_ref[key_slice(qi, hf, hf), heads[h]], qt_ref[heads[h], q_slice(qi, hf, hf)])
        early = jnp.concatenate([causal_mask(early[:, :hf]), early[:, hf:]], axis=1)
        late = causal_mask(late)
        buf[h, :hf, :] = early
        buf[h, hf:, hf:] = late
        return jnp.concatenate([col_max(early[:, :hf]),
                                jnp.maximum(col_max(early[:, hf:]), col_max(late))], axis=1)

    def process_head(kj, s, h, tile_max, m, acc):
        m_new = jnp.maximum(m, tile_max)
        p = jnp.exp2(s - m_new).astype(BF16)
        return m_new, jnp.exp2(m - m_new) * acc + _dot(vt_ref[heads[h], key_slice(kj)], p)

    def diag_process_head(buf, h, tile_max, m, acc):
        m_new = jnp.maximum(m, tile_max)
        p_early = jnp.exp2(buf[h, :hf, :] - m_new).astype(BF16)
        p_late = jnp.exp2(buf[h, hf:, hf:] - m_new[:, hf:]).astype(BF16)
        pv_early = _dot(vt_ref[heads[h], key_slice(qi, 0, hf)], p_early)
        pv_late = _dot(vt_ref[heads[h], key_slice(qi, hf, hf)], p_late)
        scaled = jnp.exp2(m - m_new) * acc + pv_early
        return m_new, jnp.concatenate([scaled[:, :hf], scaled[:, hf:] + pv_late], axis=1)

    def stage(kj, cur, nxt, tile_max, state, diag_next=False):
        next_max, out = [], []
        for h in range(n_heads):
            next_max.append(diag_scores_head(nxt, h) if diag_next else scores_head(qi, kj + 1, nxt, h))
            out.append(process_head(kj, cur[h], h, tile_max[h], *state[h]))
        return tuple(next_max), tuple(out)

    def two_stages(i, carry):
        tile_max, state = carry
        tile_max, state = stage(2 * i + 1, sa_ref, sb_ref, tile_max, state)
        return stage(2 * i + 2, sb_ref, sa_ref, tile_max, state)

    def last_tile(buf, tile_max, state):
        out = []
        for h in range(n_heads):
            mc_ref[h, 0:1, :] = scores_head(jnp.minimum(qi + 1, nq - 1), 0, sc_ref, h)
            out.append(diag_process_head(buf, h, tile_max[h], *state[h]))
        return tuple(out)

    def first_stage(state, diag_next=False):
        tile_max = tuple(mc_ref[h, 0:1, :] for h in range(n_heads))
        return stage(0, sc_ref, sa_ref, tile_max, state, diag_next)

    def case_first(state):
        tile_max = tuple(diag_scores_head(sa_ref, h) for h in range(n_heads))
        return last_tile(sa_ref, tile_max, state)

    def case_second(state):
        tile_max, state = first_stage(state, diag_next=True)
        return last_tile(sa_ref, tile_max, state)

    def case_even(state):
        tile_max, state = first_stage(state)
        tile_max, state = lax.fori_loop(0, (qi - 2) // 2, two_stages, (tile_max, state))
        tile_max, state = stage(qi - 1, sa_ref, sb_ref, tile_max, state, diag_next=True)
        return last_tile(sb_ref, tile_max, state)

    def case_odd(state):
        tile_max, state = first_stage(state)
        tile_max, state = lax.fori_loop(0, (qi - 3) // 2, two_stages, (tile_max, state))
        tile_max, state = stage(qi - 2, sa_ref, sb_ref, tile_max, state)
        tile_max, state = stage(qi - 1, sb_ref, sa_ref, tile_max, state, diag_next=True)
        return last_tile(sa_ref, tile_max, state)

    def case_later(state):
        return lax.cond(qi % 2 == 0, case_even, case_odd, state)

    def case_not_first(state):
        return lax.cond(qi == 1, case_second, case_later, state)

    init = tuple((jnp.full((1, t), -jnp.inf, F32), jnp.zeros((LANES, t), F32)) for _ in range(n_heads))
    state = lax.cond(qi == 0, case_first, case_not_first, init)

    row = lax.broadcasted_iota(jnp.int32, (LANES, t), 0)
    for hp in range(n_heads // 2):
        a0 = state[2 * hp][1]
        a1 = state[2 * hp + 1][1]
        pair_t = jnp.where(row < MLA_V, a0 / a0[MLA_V:MLA_V + 1, :], a1 / a1[0:1, :])
        cols = slice(hp * LANES, (hp + 1) * LANES)
        o_ref[q_slice(qi), cols] = (pair_t.T * sg_ref[q_slice(qi), cols].astype(F32)).astype(BF16)


def _causal_attn_kernel(qt_ref, k_ref, vt_ref, sg_ref, o_ref, sa_ref, sb_ref, sc_ref, mc_ref):
    def body(qi, carry):
        _attn_query_tile(qi, qt_ref, k_ref, vt_ref, sg_ref, o_ref, sa_ref, sb_ref, sc_ref, mc_ref)
        return carry

    lax.fori_loop(0, qt_ref.shape[1] // ATT_T, body, 0)


def _causal_attn(qt, k, vt, sg):
    b, s, hw = k.shape
    t = ATT_T
    gw = ATT_HG * LANES
    ow = ATT_HG * MLA_V
    score_buf = pltpu.VMEM((ATT_HG, t, t), F32)
    return pl.pallas_call(
        _causal_attn_kernel,
        grid=(b, hw // gw),
        in_specs=[pl.BlockSpec((None, gw, s), lambda bi, g: (bi, g, 0)),
                  pl.BlockSpec((None, s, gw), lambda bi, g: (bi, 0, g)),
                  pl.BlockSpec((None, gw, s), lambda bi, g: (bi, g, 0)),
                  pl.BlockSpec((None, s, ow), lambda bi, g: (bi, 0, g))],
        out_specs=pl.BlockSpec((None, s, ow), lambda bi, g: (bi, 0, g)),
        out_shape=jax.ShapeDtypeStruct((b, s, (hw // LANES) * MLA_V), BF16),
        scratch_shapes=[score_buf, score_buf, score_buf, pltpu.VMEM((ATT_HG, 8, t), F32)],
        compiler_params=_params(2),
        name="causal_attn",
    )(qt, k, vt, sg)


def _out_kernel(o_ref, x_ref, w_ref, g_ref, b_ref, y_ref):
    y_ref[...] = _residual_norm(o_ref, x_ref, w_ref, g_ref, b_ref)


def _out_proj(o, x, w_out, g, bias, layer):
    b, s, d = x.shape
    tm = PROJ_TM
    tok = lambda bi, si: (bi, si, 0)
    return pl.pallas_call(
        _out_kernel,
        grid=(b, s // tm),
        in_specs=[pl.BlockSpec((None, tm, d), tok), pl.BlockSpec((None, tm, d), tok),
                  _layer_spec(w_out, layer), _layer_spec(g, layer), _layer_spec(bias, layer)],
        out_specs=pl.BlockSpec((None, tm, d), tok),
        out_shape=jax.ShapeDtypeStruct((b, s, d), F32),
        compiler_params=_params(2),
        name="out_proj_ln",
    )(o, x, w_out, g, bias)


def _odd_proj_kernel(*refs, fused):
    x, refs = _layer_input(refs, fused)
    (wqt_ref, wk_ref, wvt_ref, wg_ref, c_ref, s_ref, ct_ref, st_ref,
     qt_ref, k_ref, vt_ref, sg_ref) = refs[:12]
    if fused:
        refs[12][...] = x
    d = SWA_HEADS * SWA_DIM
    half = SWA_DIM // 2
    xb = x.astype(BF16)

    kk = _dot(xb, wk_ref[...])
    lane = lax.broadcasted_iota(jnp.int32, (1, LANES), 1)
    first_half = (lane % SWA_DIM) < half
    k_sw = jnp.where(first_half, pltpu.roll(kk, LANES - half, 1), pltpu.roll(kk, half, 1))
    k_ref[...] = (kk * c_ref[...] + k_sw * s_ref[...]).astype(BF16)

    qq = _dot_nt(wqt_ref[...], xb)
    ctt = ct_ref[...]
    stt = st_ref[...]
    scale = SWA_DIM ** -0.5 * LOG2E
    for p in range(d // LANES):
        blk = qq[p * LANES:(p + 1) * LANES, :]
        q_sw = jnp.concatenate([blk[half:2 * half], blk[:half], blk[3 * half:], blk[2 * half:3 * half]], axis=0)
        qt_ref[p * LANES:(p + 1) * LANES, :] = ((blk * ctt + q_sw * stt) * scale).astype(BF16)

    vt_ref[...] = _dot_nt(wvt_ref[...], xb).astype(BF16)
    sg_ref[...] = _silu(_dot(xb, wg_ref[...])).astype(BF16)


def _layer_spec(a, layer):
    return pl.BlockSpec((None,) + a.shape[1:], lambda bi, si: (layer,) + (0,) * (a.ndim - 1))


def _odd_proj(x, w, tabs, layer, prev=None):
    b, s, d = x.shape
    tm = PROJ_TM
    tok = lambda bi, si: (bi, si, 0)
    tok_t = lambda bi, si: (bi, 0, si)
    tab = pl.BlockSpec((tm, LANES), lambda bi, si: (si, 0))
    tab_t = pl.BlockSpec((LANES, tm), lambda bi, si: (0, si))
    names = ("wqt", "wk", "wvt", "wg")
    operands, in_specs, x_spec, x_shape = _input_specs(x, prev)
    return pl.pallas_call(
        functools.partial(_odd_proj_kernel, fused=prev is not None),
        grid=(b, s // tm),
        in_specs=in_specs + [_layer_spec(w[k], layer) for k in names] + [tab, tab, tab_t, tab_t],
        out_specs=[pl.BlockSpec((None, d, tm), tok_t), pl.BlockSpec((None, tm, LANES), tok),
                   pl.BlockSpec((None, LANES, tm), tok_t), pl.BlockSpec((None, tm, d), tok)] + x_spec,
        out_shape=[jax.ShapeDtypeStruct((b, d, s), BF16), jax.ShapeDtypeStruct((b, s, LANES), BF16),
                   jax.ShapeDtypeStruct((b, LANES, s), BF16), jax.ShapeDtypeStruct((b, s, d), BF16)] + x_shape,
        compiler_params=_params(2),
        name="odd_proj",
    )(*operands, *[w[k] for k in names], tabs["c"], tabs["s"], tabs["ct"], tabs["st"])


def _swa_kernel(sink_ref, qt_ref, k_ref, vt_ref, sg_ref, o_ref):
    n_blk = qt_ref.shape[1] // QBLK
    n_pair = qt_ref.shape[0] // LANES
    win = 2 * QBLK
    qi = pl.program_id(1)
    row = lax.broadcasted_iota(jnp.int32, (LANES, QBLK), 0)
    lo_rows = row < SWA_DIM
    rel = (lax.broadcasted_iota(jnp.int32, (win, QBLK), 1) - lax.broadcasted_iota(jnp.int32, (win, QBLK), 0))
    sink_row = jnp.concatenate([jnp.full((1, QBLK), sink_ref[c] * LOG2E, F32) for c in range(2 * n_pair)],
                               axis=1)
    ones_blk = jnp.ones((16, win), BF16)

    def window_start(blk):
        return pl.multiple_of(jnp.maximum(qi * n_blk + blk - 1, 0) * QBLK, QBLK)

    def scores(blk):
        q0 = blk * QBLK
        cols = []
        for p in range(n_pair):
            qp = qt_ref[p * LANES:(p + 1) * LANES, q0:q0 + QBLK]
            zero = jnp.zeros_like(qp)
            cols.append(jnp.where(lo_rows, qp, zero))
            cols.append(jnp.where(lo_rows, zero, qp))
        s = _dot(k_ref[pl.ds(window_start(blk), win), :], jnp.concatenate(cols, axis=1))
        shift = jnp.where(qi * n_blk + blk == 0, 0, QBLK)
        diff = rel + shift
        band = (diff >= 0) & (diff < WINDOW)
        return jnp.concatenate([jnp.where(band, s[:, c * QBLK:(c + 1) * QBLK], -jnp.inf)
                                for c in range(2 * n_pair)], axis=1)

    def process(blk, s):
        q0 = blk * QBLK
        m = jnp.maximum(jnp.max(s, axis=0, keepdims=True), sink_row)
        p = jnp.exp2(s - m).astype(BF16)
        vt = jnp.concatenate([vt_ref[:, pl.ds(window_start(blk), win)], ones_blk], axis=0)
        ot = _dot(vt, p)
        inv = 1.0 / (ot[LANES:LANES + 1, :] + jnp.exp2(sink_row - m))
        for pr in range(n_pair):
            ca = slice((2 * pr) * QBLK, (2 * pr + 1) * QBLK)
            cb = slice((2 * pr + 1) * QBLK, (2 * pr + 2) * QBLK)
            pair = jnp.concatenate([ot[:SWA_DIM, ca] * inv[:, ca], ot[SWA_DIM:LANES, cb] * inv[:, cb]],
                                   axis=0).T
            cols = slice(pr * LANES, (pr + 1) * LANES)
            o_ref[q0:q0 + QBLK, cols] = (pair * sg_ref[q0:q0 + QBLK, cols].astype(F32)).astype(BF16)

    s_next = scores(0)
    for blk in range(n_blk):
        s_cur = s_next
        if blk + 1 < n_blk:
            s_next = scores(blk + 1)
        process(blk, s_cur)


def _swa_attn(sinks, qt, k, vt, sg):
    b, d, s = qt.shape
    tq = SWA_TQ
    tok = lambda bi, qi: (bi, qi, 0)
    return pl.pallas_call(
        _swa_kernel,
        grid=(b, s // tq),
        in_specs=[pl.BlockSpec(memory_space=pltpu.SMEM),
                  pl.BlockSpec((None, d, tq), lambda bi, qi: (bi, 0, qi)),
                  pl.BlockSpec((None, s, LANES), lambda bi, qi: (bi, 0, 0)),
                  pl.BlockSpec((None, LANES, s), lambda bi, qi: (bi, 0, 0)),
                  pl.BlockSpec((None, tq, d), tok)],
        out_specs=pl.BlockSpec((None, tq, d), tok),
        out_shape=jax.ShapeDtypeStruct((b, s, d), BF16),
        compiler_params=_params(2),
        name="swa_attn",
    )(sinks, qt, k, vt, sg)


def _pad_heads(w, n_heads, dim, offsets):
    k = w.shape[0]
    n_off = len(offsets)
    w4 = w.reshape(k, n_heads // n_off, n_off, dim)
    parts = [jnp.pad(w4[:, :, i, :], ((0, 0), (0, 0), (off, LANES - off - dim)))
             for i, off in enumerate(offsets)]
    return jnp.stack(parts, axis=2).reshape(k, n_heads * LANES)


def _swap_halves(w, n_heads, dim):
    k = w.shape[0]
    w3 = w.reshape(k, n_heads, dim)
    return jnp.concatenate([w3[..., dim // 2:], w3[..., :dim // 2]], axis=-1).reshape(k, n_heads * dim)


def _even_weights(w_in, q_norm, w_uq, kv_norm, w_ukv, b_f):
    sizes = (MLA_Q_RANK, MLA_KV_RANK, MLA_ROPE, FOX_HEADS * FOX_DIM, FOX_HEADS * FOX_DIM,
             FOX_HEADS * FOX_DIM, FOX_HEADS, MLA_HEADS * MLA_V + FOX_HEADS * FOX_DIM)
    cuts = [int(c) for c in np.cumsum(sizes)[:-1]]
    w_cq, w_ckv, w_kpe, w_fq, w_fk, w_fv, w_f, w_g = jnp.split(w_in, cuts, axis=1)
    place = lambda w: _pad_heads(w, 1, MLA_ROPE, (MLA_NOPE,))
    wlat = jnp.concatenate([w_cq, w_ckv, place(w_kpe), place(_swap_halves(w_kpe, 1, MLA_ROPE)),
                            _pad_heads(w_f, 1, FOX_HEADS, (0,))], axis=1)
    v_off = (0, MLA_V)
    dq = MLA_NOPE + MLA_ROPE
    uq3 = w_uq.reshape(MLA_Q_RANK, MLA_HEADS, dq)
    uq_pe = uq3[..., MLA_NOPE:].reshape(MLA_Q_RANK, MLA_HEADS * MLA_ROPE)
    uq_a = _pad_heads(w_uq, MLA_HEADS, dq, (0,))
    uq_b = _pad_heads(_swap_halves(uq_pe, MLA_HEADS, MLA_ROPE), MLA_HEADS, MLA_ROPE, (MLA_NOPE,))
    ukv3 = w_ukv.reshape(MLA_KV_RANK, MLA_HEADS, MLA_NOPE + MLA_V)
    uk = ukv3[..., :MLA_NOPE].reshape(MLA_KV_RANK, MLA_HEADS * MLA_NOPE)
    uv = ukv3[..., MLA_NOPE:].reshape(MLA_KV_RANK, MLA_HEADS * MLA_V)
    bf = lambda a: a.astype(BF16)
    return dict(wlat=bf(wlat),
                wuqt=bf(jnp.concatenate([uq_a, uq_b], axis=1).T),
                wuk=bf(_pad_heads(uk, MLA_HEADS, MLA_NOPE, (0,))),
                wuvt=bf(_pad_heads(uv, MLA_HEADS, MLA_V, v_off).T),
                wfqt=bf(w_fq.T), wfk=bf(w_fk), wfvt=bf(w_fv.T),
                wg=bf(w_g), qn=q_norm.reshape(1, -1), kvn=kv_norm.reshape(1, -1),
                bf=jnp.pad(b_f, (0, LANES - FOX_HEADS)).reshape(1, LANES))


def _pair_heads(w):
    lead = w.shape[:-1]
    w4 = w.reshape(*lead, SWA_KV_HEADS, SWA_HEADS // SWA_KV_HEADS, -1)
    return jnp.swapaxes(w4, -3, -2).reshape(*lead, -1)


def _odd_weights(w_in, sinks, w_out):
    d = SWA_HEADS * SWA_DIM
    dkv = SWA_KV_HEADS * SWA_DIM
    w_q, w_k, w_v, w_g = jnp.split(w_in, [d, d + dkv, d + 2 * dkv], axis=1)
    pair_rows = lambda a: jnp.swapaxes(
        a.reshape(SWA_KV_HEADS, SWA_HEADS // SWA_KV_HEADS, SWA_DIM, -1), 0, 1).reshape(a.shape)
    return dict(wqt=pair_rows(w_q.T).astype(BF16), wk=w_k.astype(BF16), wvt=w_v.T.astype(BF16),
                wg=_pair_heads(w_g).astype(BF16),
                sinks=_pair_heads(sinks.reshape(-1, 1).T).reshape(-1),
                w_out=pair_rows(w_out).astype(BF16))


def _rope_tables(s):
    pos = jnp.arange(s).astype(F32)

    def cs(dim):
        inv = ROPE_THETA ** (-jnp.arange(0, dim, 2, dtype=F32) / dim)
        ang = pos[:, None] * inv[None, :]
        cos, sin = jnp.cos(ang), jnp.sin(ang)
        return jnp.concatenate([cos, cos], axis=1), jnp.concatenate([-sin, sin], axis=1)

    c32, s32 = cs(MLA_ROPE)
    z = lambda n: jnp.zeros((s, n), F32)
    pad = LANES - MLA_NOPE - MLA_ROPE
    scale = (MLA_NOPE + MLA_ROPE) ** -0.5 * LOG2E
    even = dict(cqt=(jnp.concatenate([jnp.ones((s, MLA_NOPE), F32), c32, z(pad)], axis=1) * scale).T,
                sqt=(jnp.concatenate([z(MLA_NOPE), s32, z(pad)], axis=1) * scale).T,
                ck=jnp.concatenate([z(MLA_NOPE), c32, z(pad)], axis=1),
                sk=jnp.concatenate([z(MLA_NOPE), s32, z(pad)], axis=1))
    c64, s64 = cs(SWA_DIM)
    odd = dict(c=jnp.concatenate([c64, c64], axis=1), s=jnp.concatenate([s64, s64], axis=1))
    odd.update(ct=odd["c"].T, st=odd["s"].T)
    return even, odd


def kernel(x, even_w_in, even_q_norm, even_w_uq, even_kv_norm, even_w_ukv, even_b_f, even_w_out,
           even_ln_g, even_ln_b, odd_w_in, odd_sinks, odd_w_out, odd_ln_g, odd_ln_b):
    s = x.shape[1]
    even_tabs, odd_tabs = _rope_tables(s)
    we = jax.vmap(_even_weights)(even_w_in, even_q_norm, even_w_uq, even_kv_norm, even_w_ukv, even_b_f)
    wo = jax.vmap(_odd_weights)(odd_w_in, odd_sinks, odd_w_out)
    even_w_out = even_w_out.astype(BF16)
    row = lambda a: a[:, None, :]
    closing = {0: (even_w_out, row(even_ln_g), row(even_ln_b)),
               1: (wo["w_out"], row(odd_ln_g), row(odd_ln_b))}
    prev = None
    for layer in range(DEPTH):
        j = layer // 2
        if layer % 2 == 0:
            qt, k, vt, sg, *x_new = _even_proj(x, we, even_tabs, j, prev)
            x = x_new[0] if x_new else x
            o = _causal_attn(qt, k, vt, sg)
        else:
            qt, k, vt, sg, *x_new = _odd_proj(x, wo, odd_tabs, j, prev)
            x = x_new[0] if x_new else x
            o = _swa_attn(wo["sinks"][j], qt, k, vt, sg)
        prev = (o, *closing[layer % 2], j)
    o, w_out, g, bias, j = prev
    return _out_proj(o, x, w_out, g, bias, j)
```

```python
import functools
import math

import jax
import jax.numpy as jnp
import numpy as np
from jax import lax
from jax.experimental import pallas as pl
from jax.experimental.pallas import tpu as pltpu

D_MODEL = 1024
DEPTH = 4
ROPE_THETA = 10000.0
QBLK = 128
MLA_HEADS = 8
MLA_NOPE = 64
MLA_ROPE = 32
MLA_V = 64
MLA_Q_RANK = 256
MLA_KV_RANK = 128
FOX_HEADS = 8
FOX_DIM = 64
SWA_HEADS = 16
SWA_KV_HEADS = 2
SWA_DIM = 64
WINDOW = 128
RMS_EPS = 1e-6
LN_EPS = 1e-5
ALPHA = (2 * DEPTH) ** 0.25
LOG2E = math.log2(math.e)

LANES = 128
VMEM_LIMIT = 56 * 1024 * 1024
BF16 = jnp.bfloat16
F32 = jnp.float32

PROJ_TM = 512
ATT_T = 512
ATT_HG = 4
SWA_TQ = 512
EVEN_HEADS = MLA_HEADS + FOX_HEADS
EVEN_HW = EVEN_HEADS * LANES
MLA_HW = MLA_HEADS * LANES


def _dot(a, b):
    return jnp.dot(a, b, preferred_element_type=F32)


def _dot_nt(a, b):
    return lax.dot_general(a, b, (((1,), (1,)), ((), ())), preferred_element_type=F32)


def _rms(t, g):
    return t * lax.rsqrt(jnp.mean(t * t, axis=-1, keepdims=True) + RMS_EPS) * g


def _silu(g):
    return g / (1.0 + jnp.exp(-g))


def _pack3(v):
    hi = v.astype(BF16).astype(F32)
    r1 = v - hi
    mid = r1.astype(BF16).astype(F32)
    lo = r1 - mid
    return (hi + pltpu.roll(mid, FOX_HEADS, 1) + pltpu.roll(lo, 2 * FOX_HEADS, 1)).astype(BF16)


def _params(n_axes, flags=None):
    return pltpu.CompilerParams(dimension_semantics=("arbitrary",) * n_axes,
                                vmem_limit_bytes=VMEM_LIMIT, flags=flags)


def _residual_norm(o_ref, x_ref, w_ref, g_ref, b_ref):
    z = ALPHA * x_ref[...] + _dot(o_ref[...], w_ref[...])
    mu = jnp.mean(z, axis=-1, keepdims=True)
    zc = z - mu
    var = jnp.mean(zc * zc, axis=-1, keepdims=True)
    return zc * lax.rsqrt(var + LN_EPS) * g_ref[...] + b_ref[...]


def _layer_input(refs, fused):
    if not fused:
        return refs[0][...], refs[1:]
    x = _residual_norm(*refs[:5])
    return x, refs[5:]


def _even_proj_kernel(*refs, fused):
    x, refs = _layer_input(refs, fused)
    (wlat_ref, wuqt_ref, wuk_ref, wuvt_ref, wfqt_ref, wfk_ref, wfvt_ref, wg_ref, qn_ref, kvn_ref,
     bf_ref, cqt_ref, sqt_ref, ck_ref, sk_ref, qt_ref, k_ref, vt_ref, sg_ref) = refs[:19]
    carry_ref = refs[-1]
    if fused:
        refs[19][...] = x
    tm = x.shape[0]
    xb = x.astype(BF16)

    hrow = lax.broadcasted_iota(jnp.int32, (MLA_HW, 1), 0) % (2 * LANES)
    ones_col = jnp.where((hrow == MLA_V) | (hrow == LANES), 1.0, 0.0).astype(F32)
    lat = _dot(xb, wlat_ref[...])
    o = 0
    cq = lat[:, o:o + MLA_Q_RANK]
    o += MLA_Q_RANK
    ckv = lat[:, o:o + MLA_KV_RANK]
    o += MLA_KV_RANK
    kp = lat[:, o:o + LANES]
    hr = MLA_ROPE // 2
    lane_p = lax.broadcasted_iota(jnp.int32, (1, LANES), 1)
    kp_sw = jnp.where(lane_p < MLA_NOPE + hr, pltpu.roll(kp, LANES - hr, 1), pltpu.roll(kp, hr, 1))
    kpe = kp * ck_ref[...] + kp_sw * sk_ref[...]
    o += LANES
    fl = lat[:, o:o + LANES] + bf_ref[...]

    rq = _rms(cq, qn_ref[...]).astype(BF16)
    qa = _dot_nt(wuqt_ref[...], rq)
    cqt = cqt_ref[...]
    sqt = sqt_ref[...]
    pe = MLA_NOPE
    for h in range(MLA_HEADS):
        hs = slice(h * LANES, (h + 1) * LANES)
        blk = qa[hs, :]
        q_sw = jnp.concatenate([blk[:pe], blk[pe + hr:pe + 2 * hr], blk[pe:pe + hr], blk[pe + 2 * hr:]], axis=0)
        qt_ref[hs, :] = (blk * cqt + q_sw * sqt).astype(BF16)
    rkv = _rms(ckv, kvn_ref[...]).astype(BF16)
    kk = _dot(rkv, wuk_ref[...])
    for h in range(MLA_HEADS):
        hs = slice(h * LANES, (h + 1) * LANES)
        k_ref[:, hs] = (kk[:, hs] + kpe).astype(BF16)
    vt_ref[:MLA_HW, :] = (_dot_nt(wuvt_ref[...], rkv) + ones_col).astype(BF16)

    lane = lax.broadcasted_iota(jnp.int32, (1, LANES), 1)
    lf = (jnp.minimum(fl, 0.0) - jnp.log1p(jnp.exp(-jnp.abs(fl)))) * LOG2E
    lf = jnp.where(lane < FOX_HEADS, lf, 0.0)
    row = lax.broadcasted_iota(jnp.int32, (tm, tm), 0)
    col = lax.broadcasted_iota(jnp.int32, (tm, tm), 1)
    tri = jnp.where(col <= row, 1.0, 0.0).astype(BF16)
    sums = _dot(tri, _pack3(lf))
    c_loc = sums + pltpu.roll(sums, LANES - FOX_HEADS, 1) + pltpu.roll(sums, LANES - 2 * FOX_HEADS, 1)
    c_loc = jnp.where(lane < FOX_HEADS, c_loc, 0.0)

    @pl.when(pl.program_id(1) == 0)
    def _():
        carry_ref[...] = jnp.zeros_like(carry_ref)

    c = c_loc + carry_ref[0:1, :]
    carry_ref[...] = jnp.broadcast_to(c[tm - 1:tm, :], carry_ref.shape)
    src = lax.broadcasted_iota(jnp.int32, (LANES, MLA_HW), 0)
    dst = lax.broadcasted_iota(jnp.int32, (LANES, MLA_HW), 1)
    src_h = src % FOX_HEADS
    c_lane = src_h * LANES + jnp.where(src_h % 2 == 0, FOX_DIM, 0) + src // FOX_HEADS
    place = jnp.where((dst == c_lane) & (src < 3 * FOX_HEADS), 1.0, 0.0).astype(BF16)
    k_extra = _dot(_pack3(c), place)

    fqt = _dot_nt(wfqt_ref[...], xb) * (FOX_DIM ** -0.5 * LOG2E)
    fk = _dot(xb, wfk_ref[...])
    fvt = _dot_nt(wfvt_ref[...], xb)
    half_row = lax.broadcasted_iota(jnp.int32, (FOX_DIM, tm), 0)
    neg_blk = jnp.where(half_row < 3, -1.0, 0.0).astype(BF16)
    ones_blk = jnp.where(half_row < 1, 1.0, 0.0).astype(BF16)
    lane_k = lax.broadcasted_iota(jnp.int32, (1, LANES), 1)
    for h in range(FOX_HEADS):
        off = (h % 2) * FOX_DIM
        base = MLA_HW + h * LANES
        data = slice(base + off, base + off + FOX_DIM)
        rest = slice(base + FOX_DIM - off, base + 2 * FOX_DIM - off)
        rows = slice(h * FOX_DIM, (h + 1) * FOX_DIM)
        qt_ref[data, :] = fqt[rows, :].astype(BF16)
        qt_ref[rest, :] = neg_blk
        vt_ref[data, :] = fvt[rows, :].astype(BF16)
        vt_ref[rest, :] = ones_blk
        pair = slice((h // 2) * LANES, (h // 2 + 1) * LANES)
        in_data = (lane_k >= off) & (lane_k < off + FOX_DIM)
        k_ref[:, base:base + LANES] = jnp.where(
            in_data, fk[:, pair], k_extra[:, h * LANES:(h + 1) * LANES]).astype(BF16)

    sg_ref[...] = _silu(_dot(xb, wg_ref[...])).astype(BF16)


def _input_specs(x, prev):
    b, s, d = x.shape
    tok = pl.BlockSpec((None, PROJ_TM, d), lambda bi, si: (bi, si, 0))
    if prev is None:
        return [x], [tok], [], []
    o, w_out, g, bias, layer = prev
    specs = [tok, tok] + [_layer_spec(a, layer) for a in (w_out, g, bias)]
    return [o, x, w_out, g, bias], specs, [tok], [jax.ShapeDtypeStruct((b, s, d), F32)]


def _even_proj(x, w, tabs, layer, prev=None):
    b, s, d = x.shape
    tm = PROJ_TM
    tok = lambda bi, si: (bi, si, 0)
    tok_t = lambda bi, si: (bi, 0, si)
    names = ("wlat", "wuqt", "wuk", "wuvt", "wfqt", "wfk", "wfvt", "wg", "qn", "kvn", "bf")
    operands, in_specs, x_spec, x_shape = _input_specs(x, prev)
    in_specs += [_layer_spec(w[k], layer) for k in names]
    in_specs += [pl.BlockSpec((LANES, tm), lambda bi, si: (0, si))] * 2
    in_specs += [pl.BlockSpec((tm, LANES), lambda bi, si: (si, 0))] * 2
    out_specs = [pl.BlockSpec((None, EVEN_HW, tm), tok_t), pl.BlockSpec((None, tm, EVEN_HW), tok),
                 pl.BlockSpec((None, EVEN_HW, tm), tok_t), pl.BlockSpec((None, tm, d), tok)]
    out_shape = [jax.ShapeDtypeStruct((b, EVEN_HW, s), BF16), jax.ShapeDtypeStruct((b, s, EVEN_HW), BF16),
                 jax.ShapeDtypeStruct((b, EVEN_HW, s), BF16), jax.ShapeDtypeStruct((b, s, d), BF16)]
    return pl.pallas_call(
        functools.partial(_even_proj_kernel, fused=prev is not None),
        grid=(b, s // tm),
        in_specs=in_specs,
        out_specs=out_specs + x_spec,
        out_shape=out_shape + x_shape,
        scratch_shapes=[pltpu.VMEM((8, LANES), F32)],
        compiler_params=_params(2),
        name="even_proj",
    )(*operands, *[w[k] for k in names], tabs["cqt"], tabs["sqt"], tabs["ck"], tabs["sk"])


def _attn_query_tile(qi, qt_ref, k_ref, vt_ref, sg_ref, o_ref, sa_ref, sb_ref, sc_ref, mc_ref):
    t = ATT_T
    hf = t // 2
    n_heads = qt_ref.shape[0] // LANES
    nq = qt_ref.shape[1] // t
    heads = [slice(h * LANES, (h + 1) * LANES) for h in range(n_heads)]

    def q_slice(tile, start=0, size=None):
        size = t if size is None else size
        return pl.ds(pl.multiple_of(tile * t + start, size), size)

    def key_slice(kj, start=0, size=None):
        size = t if size is None else size
        return pl.ds(pl.multiple_of(kj * t + start, size), size)

    def col_max(s):
        return jnp.max(s, axis=0, keepdims=True)

    def causal_mask(s):
        n = s.shape[0]
        keep = (lax.broadcasted_iota(jnp.int32, (n, n), 0) <= lax.broadcasted_iota(jnp.int32, (n, n), 1))
        return jnp.where(keep, s, -jnp.inf)

    def scores_head(q_tile, kj, buf, h):
        s = _dot(k_ref[key_slice(kj), heads[h]], qt_ref[heads[h], q_slice(q_tile)])
        buf[h] = s
        return col_max(s)

    def diag_scores_head(buf, h):
        early = _dot(k_ref[key_slice(qi, 0, hf), heads[h]], qt_ref[heads[h], q_slice(qi)])
        late = _dot(k_ref[key_slice(qi, hf, hf), heads[h]], qt_ref[heads[h], q_slice(qi, hf, hf)])
        early = jnp.concatenate([causal_mask(early[:, :hf]), early[:, hf:]], axis=1)
        late = causal_mask(late)
        buf[h, :hf, :] = early
        buf[h, hf:, hf:] = late
        return jnp.concatenate([col_max(early[:, :hf]),
                                jnp.maximum(col_max(early[:, hf:]), col_max(late))], axis=1)

    def process_head(kj, s, h, tile_max, m, acc):
        m_new = jnp.maximum(m, tile_max)
        p = jnp.exp2(s - m_new).astype(BF16)
        return m_new, jnp.exp2(m - m_new) * acc + _dot(vt_ref[heads[h], key_slice(kj)], p)

    def diag_process_head(buf, h, tile_max, m, acc):
        m_new = jnp.maximum(m, tile_max)
        p_early = jnp.exp2(buf[h, :hf, :] - m_new).astype(BF16)
        p_late = jnp.exp2(buf[h, hf:, hf:] - m_new[:, hf:]).astype(BF16)
        pv_early = _dot(vt_ref[heads[h], key_slice(qi, 0, hf)], p_early)
        pv_late = _dot(vt_ref[heads[h], key_slice(qi, hf, hf)], p_late)
        scaled = jnp.exp2(m - m_new) * acc + pv_early
        return m_new, jnp.concatenate([scaled[:, :hf], scaled[:, hf:] + pv_late], axis=1)

    def stage(kj, cur, nxt, tile_max, state, diag_next=False):
        next_max, out = [], []
        for h in range(n_heads):
            next_max.append(diag_scores_head(nxt, h) if diag_next else scores_head(qi, kj + 1, nxt, h))
            out.append(process_head(kj, cur[h], h, tile_max[h], *state[h]))
        return tuple(next_max), tuple(out)

    def two_stages(i, carry):
        tile_max, state = carry
        tile_max, state = stage(2 * i + 1, sa_ref, sb_ref, tile_max, state)
        return stage(2 * i + 2, sb_ref, sa_ref, tile_max, state)

    def last_tile(buf, tile_max, state):
        out = []
        for h in range(n_heads):
            mc_ref[h, 0:1, :] = scores_head(jnp.minimum(qi + 1, nq - 1), 0, sc_ref, h)
            out.append(diag_process_head(buf, h, tile_max[h], *state[h]))
        return tuple(out)

    def first_stage(state, diag_next=False):
        tile_max = tuple(mc_ref[h, 0:1, :] for h in range(n_heads))
        return stage(0, sc_ref, sa_ref, tile_max, state, diag_next)

    def case_first(state):
        tile_max = tuple(diag_scores_head(sa_ref, h) for h in range(n_heads))
        return last_tile(sa_ref, tile_max, state)

    def case_second(state):
        tile_max, state = first_stage(state, diag_next=True)
        return last_tile(sa_ref, tile_max, state)

    def case_even(state):
        tile_max, state = first_stage(state)
        tile_max, state = lax.fori_loop(0, (qi - 2) // 2, two_stages, (tile_max, state))
        tile_max, state = stage(qi - 1, sa_ref, sb_ref, tile_max, state, diag_next=True)
        return last_tile(sb_ref, tile_max, state)

    def case_odd(state):
        tile_max, state = first_stage(state)
        tile_max, state = lax.fori_loop(0, (qi - 3) // 2, two_stages, (tile_max, state))
        tile_max, state = stage(qi - 2, sa_ref, sb_ref, tile_max, state)
        tile_max, state = stage(qi - 1, sb_ref, sa_ref, tile_max, state, diag_next=True)
        return last_tile(sa_ref, tile_max, state)

    def case_later(state):
        return lax.cond(qi % 2 == 0, case_even, case_odd, state)

    def case_not_first(state):
        return lax.cond(qi == 1, case_second, case_later, state)

    init = tuple((jnp.full((1, t), -jnp.inf, F32), jnp.zeros((LANES, t), F32)) for _ in range(n_heads))
    state = lax.cond(qi == 0, case_first, case_not_first, init)

    row = lax.broadcasted_iota(jnp.int32, (LANES, t), 0)
    for hp in range(n_heads // 2):
        a0 = state[2 * hp][1]
        a1 = state[2 * hp + 1][1]
        pair_t = jnp.where(row < MLA_V, a0 / a0[MLA_V:MLA_V + 1, :], a1 / a1[0:1, :])
        cols = slice(hp * LANES, (hp + 1) * LANES)
        o_ref[q_slice(qi), cols] = (pair_t.T * sg_ref[q_slice(qi), cols].astype(F32)).astype(BF16)


def _causal_attn_kernel(qt_ref, k_ref, vt_ref, sg_ref, o_ref, sa_ref, sb_ref, sc_ref, mc_ref):
    def body(qi, carry):
        _attn_query_tile(qi, qt_ref, k_ref, vt_ref, sg_ref, o_ref, sa_ref, sb_ref, sc_ref, mc_ref)
        return carry

    lax.fori_loop(0, qt_ref.shape[1] // ATT_T, body, 0)


def _causal_attn(qt, k, vt, sg):
    b, s, hw = k.shape
    t = ATT_T
    gw = ATT_HG * LANES
    ow = ATT_HG * MLA_V
    score_buf = pltpu.VMEM((ATT_HG, t, t), F32)
    return pl.pallas_call(
        _causal_attn_kernel,
        grid=(b, hw // gw),
        in_specs=[pl.BlockSpec((None, gw, s), lambda bi, g: (bi, g, 0)),
                  pl.BlockSpec((None, s, gw), lambda bi, g: (bi, 0, g)),
                  pl.BlockSpec((None, gw, s), lambda bi, g: (bi, g, 0)),
                  pl.BlockSpec((None, s, ow), lambda bi, g: (bi, 0, g))],
        out_specs=pl.BlockSpec((None, s, ow), lambda bi, g: (bi, 0, g)),
        out_shape=jax.ShapeDtypeStruct((b, s, (hw // LANES) * MLA_V), BF16),
        scratch_shapes=[score_buf, score_buf, score_buf, pltpu.VMEM((ATT_HG, 8, t), F32)],
        compiler_params=_params(2),
        name="causal_attn",
    )(qt, k, vt, sg)


def _out_kernel(o_ref, x_ref, w_ref, g_ref, b_ref, y_ref):
    y_ref[...] = _residual_norm(o_ref, x_ref, w_ref, g_ref, b_ref)


def _out_proj(o, x, w_out, g, bias, layer):
    b, s, d = x.shape
    tm = PROJ_TM
    tok = lambda bi, si: (bi, si, 0)
    return pl.pallas_call(
        _out_kernel,
        grid=(b, s // tm),
        in_specs=[pl.BlockSpec((None, tm, d), tok), pl.BlockSpec((None, tm, d), tok),
                  _layer_spec(w_out, layer), _layer_spec(g, layer), _layer_spec(bias, layer)],
        out_specs=pl.BlockSpec((None, tm, d), tok),
        out_shape=jax.ShapeDtypeStruct((b, s, d), F32),
        compiler_params=_params(2),
        name="out_proj_ln",
    )(o, x, w_out, g, bias)


def _odd_proj_kernel(*refs, fused):
    x, refs = _layer_input(refs, fused)
    (wqt_ref, wk_ref, wvt_ref, wg_ref, c_ref, s_ref, ct_ref, st_ref,
     qt_ref, k_ref, vt_ref, sg_ref) = refs[:12]
    if fused:
        refs[12][...] = x
    d = SWA_HEADS * SWA_DIM
    half = SWA_DIM // 2
    xb = x.astype(BF16)

    kk = _dot(xb, wk_ref[...])
    lane = lax.broadcasted_iota(jnp.int32, (1, LANES), 1)
    first_half = (lane % SWA_DIM) < half
    k_sw = jnp.where(first_half, pltpu.roll(kk, LANES - half, 1), pltpu.roll(kk, half, 1))
    k_ref[...] = (kk * c_ref[...] + k_sw * s_ref[...]).astype(BF16)

    qq = _dot_nt(wqt_ref[...], xb)
    ctt = ct_ref[...]
    stt = st_ref[...]
    scale = SWA_DIM ** -0.5 * LOG2E
    for p in range(d // LANES):
        blk = qq[p * LANES:(p + 1) * LANES, :]
        q_sw = jnp.concatenate([blk[half:2 * half], blk[:half], blk[3 * half:], blk[2 * half:3 * half]], axis=0)
        qt_ref[p * LANES:(p + 1) * LANES, :] = ((blk * ctt + q_sw * stt) * scale).astype(BF16)

    vt_ref[...] = _dot_nt(wvt_ref[...], xb).astype(BF16)
    sg_ref[...] = _silu(_dot(xb, wg_ref[...])).astype(BF16)


def _layer_spec(a, layer):
    return pl.BlockSpec((None,) + a.shape[1:], lambda bi, si: (layer,) + (0,) * (a.ndim - 1))


def _odd_proj(x, w, tabs, layer, prev=None):
    b, s, d = x.shape
    tm = PROJ_TM
    tok = lambda bi, si: (bi, si, 0)
    tok_t = lambda bi, si: (bi, 0, si)
    tab = pl.BlockSpec((tm, LANES), lambda bi, si: (si, 0))
    tab_t = pl.BlockSpec((LANES, tm), lambda bi, si: (0, si))
    names = ("wqt", "wk", "wvt", "wg")
    operands, in_specs, x_spec, x_shape = _input_specs(x, prev)
    return pl.pallas_call(
        functools.partial(_odd_proj_kernel, fused=prev is not None),
        grid=(b, s // tm),
        in_specs=in_specs + [_layer_spec(w[k], layer) for k in names] + [tab, tab, tab_t, tab_t],
        out_specs=[pl.BlockSpec((None, d, tm), tok_t), pl.BlockSpec((None, tm, LANES), tok),
                   pl.BlockSpec((None, LANES, tm), tok_t), pl.BlockSpec((None, tm, d), tok)] + x_spec,
        out_shape=[jax.ShapeDtypeStruct((b, d, s), BF16), jax.ShapeDtypeStruct((b, s, LANES), BF16),
                   jax.ShapeDtypeStruct((b, LANES, s), BF16), jax.ShapeDtypeStruct((b, s, d), BF16)] + x_shape,
        compiler_params=_params(2),
        name="odd_proj",
    )(*operands, *[w[k] for k in names], tabs["c"], tabs["s"], tabs["ct"], tabs["st"])


def _swa_kernel(sink_ref, qt_ref, k_ref, vt_ref, sg_ref, o_ref):
    n_blk = qt_ref.shape[1] // QBLK
    n_pair = qt_ref.shape[0] // LANES
    win = 2 * QBLK
    qi = pl.program_id(1)
    row = lax.broadcasted_iota(jnp.int32, (LANES, QBLK), 0)
    lo_rows = row < SWA_DIM
    rel = (lax.broadcasted_iota(jnp.int32, (win, QBLK), 1) - lax.broadcasted_iota(jnp.int32, (win, QBLK), 0))
    sink_row = jnp.concatenate([jnp.full((1, QBLK), sink_ref[c] * LOG2E, F32) for c in range(2 * n_pair)],
                               axis=1)
    ones_blk = jnp.ones((16, win), BF16)

    def window_start(blk):
        return pl.multiple_of(jnp.maximum(qi * n_blk + blk - 1, 0) * QBLK, QBLK)

    def scores(blk):
        q0 = blk * QBLK
        cols = []
        for p in range(n_pair):
            qp = qt_ref[p * LANES:(p + 1) * LANES, q0:q0 + QBLK]
            zero = jnp.zeros_like(qp)
            cols.append(jnp.where(lo_rows, qp, zero))
            cols.append(jnp.where(lo_rows, zero, qp))
        s = _dot(k_ref[pl.ds(window_start(blk), win), :], jnp.concatenate(cols, axis=1))
        shift = jnp.where(qi * n_blk + blk == 0, 0, QBLK)
        diff = rel + shift
        band = (diff >= 0) & (diff < WINDOW)
        return jnp.concatenate([jnp.where(band, s[:, c * QBLK:(c + 1) * QBLK], -jnp.inf)
                                for c in range(2 * n_pair)], axis=1)

    def process(blk, s):
        q0 = blk * QBLK
        m = jnp.maximum(jnp.max(s, axis=0, keepdims=True), sink_row)
        p = jnp.exp2(s - m).astype(BF16)
        vt = jnp.concatenate([vt_ref[:, pl.ds(window_start(blk), win)], ones_blk], axis=0)
        ot = _dot(vt, p)
        inv = 1.0 / (ot[LANES:LANES + 1, :] + jnp.exp2(sink_row - m))
        for pr in range(n_pair):
            ca = slice((2 * pr) * QBLK, (2 * pr + 1) * QBLK)
            cb = slice((2 * pr + 1) * QBLK, (2 * pr + 2) * QBLK)
            pair = jnp.concatenate([ot[:SWA_DIM, ca] * inv[:, ca], ot[SWA_DIM:LANES, cb] * inv[:, cb]],
                                   axis=0).T
            cols = slice(pr * LANES, (pr + 1) * LANES)
            o_ref[q0:q0 + QBLK, cols] = (pair * sg_ref[q0:q0 + QBLK, cols].astype(F32)).astype(BF16)

    s_next = scores(0)
    for blk in range(n_blk):
        s_cur = s_next
        if blk + 1 < n_blk:
            s_next = scores(blk + 1)
        process(blk, s_cur)


def _swa_attn(sinks, qt, k, vt, sg):
    b, d, s = qt.shape
    tq = SWA_TQ
    tok = lambda bi, qi: (bi, qi, 0)
    return pl.pallas_call(
        _swa_kernel,
        grid=(b, s // tq),
        in_specs=[pl.BlockSpec(memory_space=pltpu.SMEM),
                  pl.BlockSpec((None, d, tq), lambda bi, qi: (bi, 0, qi)),
                  pl.BlockSpec((None, s, LANES), lambda bi, qi: (bi, 0, 0)),
                  pl.BlockSpec((None, LANES, s), lambda bi, qi: (bi, 0, 0)),
                  pl.BlockSpec((None, tq, d), tok)],
        out_specs=pl.BlockSpec((None, tq, d), tok),
        out_shape=jax.ShapeDtypeStruct((b, s, d), BF16),
        compiler_params=_params(2),
        name="swa_attn",
    )(sinks, qt, k, vt, sg)


def _pad_heads(w, n_heads, dim, offsets):
    k = w.shape[0]
    n_off = len(offsets)
    w4 = w.reshape(k, n_heads // n_off, n_off, dim)
    parts = [jnp.pad(w4[:, :, i, :], ((0, 0), (0, 0), (off, LANES - off - dim)))
             for i, off in enumerate(offsets)]
    return jnp.stack(parts, axis=2).reshape(k, n_heads * LANES)


def _swap_halves(w, n_heads, dim):
    k = w.shape[0]
    w3 = w.reshape(k, n_heads, dim)
    return jnp.concatenate([w3[..., dim // 2:], w3[..., :dim // 2]], axis=-1).reshape(k, n_heads * dim)


def _even_weights(w_in, q_norm, w_uq, kv_norm, w_ukv, b_f):
    sizes = (MLA_Q_RANK, MLA_KV_RANK, MLA_ROPE, FOX_HEADS * FOX_DIM, FOX_HEADS * FOX_DIM,
             FOX_HEADS * FOX_DIM, FOX_HEADS, MLA_HEADS * MLA_V + FOX_HEADS * FOX_DIM)
    cuts = [int(c) for c in np.cumsum(sizes)[:-1]]
    w_cq, w_ckv, w_kpe, w_fq, w_fk, w_fv, w_f, w_g = jnp.split(w_in, cuts, axis=1)
    place = lambda w: _pad_heads(w, 1, MLA_ROPE, (MLA_NOPE,))
    wlat = jnp.concatenate([w_cq, w_ckv, place(w_kpe), _pad_heads(w_f, 1, FOX_HEADS, (0,))], axis=1)
    v_off = (0, MLA_V)
    ukv3 = w_ukv.reshape(MLA_KV_RANK, MLA_HEADS, MLA_NOPE + MLA_V)
    uk = ukv3[..., :MLA_NOPE].reshape(MLA_KV_RANK, MLA_HEADS * MLA_NOPE)
    uv = ukv3[..., MLA_NOPE:].reshape(MLA_KV_RANK, MLA_HEADS * MLA_V)
    bf = lambda a: a.astype(BF16)
    return dict(wlat=bf(wlat),
                wuqt=bf(_pad_heads(w_uq, MLA_HEADS, MLA_NOPE + MLA_ROPE, (0,)).T),
                wuk=bf(_pad_heads(uk, MLA_HEADS, MLA_NOPE, (0,))),
                wuvt=bf(_pad_heads(uv, MLA_HEADS, MLA_V, v_off).T),
                wfqt=bf(w_fq.T), wfk=bf(w_fk), wfvt=bf(w_fv.T),
                wg=bf(w_g), qn=q_norm.reshape(1, -1), kvn=kv_norm.reshape(1, -1),
                bf=jnp.pad(b_f, (0, LANES - FOX_HEADS)).reshape(1, LANES))


def _pair_heads(w):
    lead = w.shape[:-1]
    w4 = w.reshape(*lead, SWA_KV_HEADS, SWA_HEADS // SWA_KV_HEADS, -1)
    return jnp.swapaxes(w4, -3, -2).reshape(*lead, -1)


def _odd_weights(w_in, sinks, w_out):
    d = SWA_HEADS * SWA_DIM
    dkv = SWA_KV_HEADS * SWA_DIM
    w_q, w_k, w_v, w_g = jnp.split(w_in, [d, d + dkv, d + 2 * dkv], axis=1)
    pair_rows = lambda a: jnp.swapaxes(
        a.reshape(SWA_KV_HEADS, SWA_HEADS // SWA_KV_HEADS, SWA_DIM, -1), 0, 1).reshape(a.shape)
    return dict(wqt=pair_rows(w_q.T).astype(BF16), wk=w_k.astype(BF16), wvt=w_v.T.astype(BF16),
                wg=_pair_heads(w_g).astype(BF16),
                sinks=_pair_heads(sinks.reshape(-1, 1).T).reshape(-1),
                w_out=pair_rows(w_out).astype(BF16))


def _rope_tables(s):
    pos = np.arange(s, dtype=np.float64)

    def cs(dim):
        inv = ROPE_THETA ** (-np.arange(0, dim, 2, dtype=np.float64) / dim)
        ang = pos[:, None] * inv[None, :]
        cos, sin = np.cos(ang), np.sin(ang)
        return np.concatenate([cos, cos], axis=1), np.concatenate([-sin, sin], axis=1)

    c32, s32 = cs(MLA_ROPE)
    z = lambda n: np.zeros((s, n))
    pad = LANES - MLA_NOPE - MLA_ROPE
    scale = (MLA_NOPE + MLA_ROPE) ** -0.5 * LOG2E
    even = dict(cqt=(np.concatenate([np.ones((s, MLA_NOPE)), c32, z(pad)], axis=1) * scale).T,
                sqt=(np.concatenate([z(MLA_NOPE), s32, z(pad)], axis=1) * scale).T,
                ck=np.concatenate([z(MLA_NOPE), c32, z(pad)], axis=1),
                sk=np.concatenate([z(MLA_NOPE), s32, z(pad)], axis=1))
    c64, s64 = cs(SWA_DIM)
    odd = dict(c=np.concatenate([c64, c64], axis=1), s=np.concatenate([s64, s64], axis=1))
    odd.update(ct=odd["c"].T, st=odd["s"].T)
    as_f32 = lambda d: {k: jnp.asarray(np.ascontiguousarray(v), dtype=F32) for k, v in d.items()}
    return as_f32(even), as_f32(odd)


def kernel(x, even_w_in, even_q_norm, even_w_uq, even_kv_norm, even_w_ukv, even_b_f, even_w_out,
           even_ln_g, even_ln_b, odd_w_in, odd_sinks, odd_w_out, odd_ln_g, odd_ln_b):
    s = x.shape[1]
    even_tabs, odd_tabs = _rope_tables(s)
    bf = lambda a: a.astype(BF16)
    we = jax.vmap(_even_weights)(bf(even_w_in), even_q_norm, bf(even_w_uq), even_kv_norm, bf(even_w_ukv),
                                 even_b_f)
    wo = jax.vmap(_odd_weights)(bf(odd_w_in), odd_sinks, bf(odd_w_out))
    even_w_out = even_w_out.astype(BF16)
    row = lambda a: a[:, None, :]
    closing = {0: (even_w_out, row(even_ln_g), row(even_ln_b)),
               1: (wo["w_out"], row(odd_ln_g), row(odd_ln_b))}
    prev = None
    for layer in range(DEPTH):
        j = layer // 2
        if layer % 2 == 0:
            qt, k, vt, sg, *x_new = _even_proj(x, we, even_tabs, j, prev)
            x = x_new[0] if x_new else x
            o = _causal_attn(qt, k, vt, sg)
        else:
            qt, k, vt, sg, *x_new = _odd_proj(x, wo, odd_tabs, j, prev)
            x = x_new[0] if x_new else x
            o = _swa_attn(wo["sinks"][j], qt, k, vt, sg)
        prev = (o, *closing[layer % 2], j)
    o, w_out, g, bias, j = prev
    return _out_proj(o, x, w_out, g, bias, j)
```

```python
import functools
import math

import jax
import jax.numpy as jnp
import numpy as np
from jax import lax
from jax.experimental import pallas as pl
from jax.experimental.pallas import tpu as pltpu

D_MODEL = 1024
DEPTH = 4
ROPE_THETA = 10000.0
QBLK = 128
MLA_HEADS = 8
MLA_NOPE = 64
MLA_ROPE = 32
MLA_V = 64
MLA_Q_RANK = 256
MLA_KV_RANK = 128
FOX_HEADS = 8
FOX_DIM = 64
SWA_HEADS = 16
SWA_KV_HEADS = 2
SWA_DIM = 64
WINDOW = 128
RMS_EPS = 1e-6
LN_EPS = 1e-5
ALPHA = (2 * DEPTH) ** 0.25
LOG2E = math.log2(math.e)

LANES = 128
VMEM_LIMIT = 56 * 1024 * 1024
BF16 = jnp.bfloat16
F32 = jnp.float32

PROJ_TM = 512
ATT_T = 512
ATT_HG = 4
SWA_TQ = 512
EVEN_HEADS = MLA_HEADS + FOX_HEADS
EVEN_HW = EVEN_HEADS * LANES
MLA_HW = MLA_HEADS * LANES


def _dot(a, b):
    return jnp.dot(a, b, preferred_element_type=F32)


def _dot_nt(a, b):
    return lax.dot_general(a, b, (((1,), (1,)), ((), ())), preferred_element_type=F32)


def _rms(t, g):
    return t * lax.rsqrt(jnp.mean(t * t, axis=-1, keepdims=True) + RMS_EPS) * g


def _silu(g):
    return g / (1.0 + jnp.exp(-g))


def _pack3(v):
    hi = v.astype(BF16).astype(F32)
    r1 = v - hi
    mid = r1.astype(BF16).astype(F32)
    lo = r1 - mid
    return (hi + pltpu.roll(mid, FOX_HEADS, 1) + pltpu.roll(lo, 2 * FOX_HEADS, 1)).astype(BF16)


def _params(n_axes, flags=None):
    return pltpu.CompilerParams(dimension_semantics=("arbitrary",) * n_axes,
                                vmem_limit_bytes=VMEM_LIMIT, flags=flags)


def _residual_norm(o_ref, x_ref, w_ref, g_ref, b_ref):
    z = ALPHA * x_ref[...] + _dot(o_ref[...], w_ref[...])
    mu = jnp.mean(z, axis=-1, keepdims=True)
    zc = z - mu
    var = jnp.mean(zc * zc, axis=-1, keepdims=True)
    return zc * lax.rsqrt(var + LN_EPS) * g_ref[...] + b_ref[...]


def _layer_input(refs, fused):
    if not fused:
        return refs[0][...], refs[1:]
    x = _residual_norm(*refs[:5])
    return x, refs[5:]


def _even_proj_kernel(*refs, fused):
    x, refs = _layer_input(refs, fused)
    (wlat_ref, wuqt_ref, wuk_ref, wuvt_ref, wfqt_ref, wfk_ref, wfvt_ref, wg_ref, qn_ref, kvn_ref,
     bf_ref, cqt_ref, sqt_ref, ck_ref, sk_ref, qt_ref, k_ref, vt_ref, sg_ref) = refs[:19]
    carry_ref = refs[-1]
    if fused:
        refs[19][...] = x
    tm = x.shape[0]
    xb = x.astype(BF16)

    hrow = lax.broadcasted_iota(jnp.int32, (MLA_HW, 1), 0) % (2 * LANES)
    ones_col = jnp.where((hrow == MLA_V) | (hrow == LANES), 1.0, 0.0).astype(F32)
    lat = _dot(xb, wlat_ref[...])
    o = 0
    cq = lat[:, o:o + MLA_Q_RANK]
    o += MLA_Q_RANK
    ckv = lat[:, o:o + MLA_KV_RANK]
    o += MLA_KV_RANK
    kp = lat[:, o:o + LANES]
    hr = MLA_ROPE // 2
    lane_p = lax.broadcasted_iota(jnp.int32, (1, LANES), 1)
    kp_sw = jnp.where(lane_p < MLA_NOPE + hr, pltpu.roll(kp, LANES - hr, 1), pltpu.roll(kp, hr, 1))
    kpe = kp * ck_ref[...] + kp_sw * sk_ref[...]
    o += LANES
    fl = lat[:, o:o + LANES] + bf_ref[...]

    rq = _rms(cq, qn_ref[...]).astype(BF16)
    qa = _dot_nt(wuqt_ref[...], rq)
    cqt = cqt_ref[...]
    sqt = sqt_ref[...]
    pe = MLA_NOPE
    for h in range(MLA_HEADS):
        hs = slice(h * LANES, (h + 1) * LANES)
        blk = qa[hs, :]
        q_sw = jnp.concatenate([blk[:pe], blk[pe + hr:pe + 2 * hr], blk[pe:pe + hr], blk[pe + 2 * hr:]], axis=0)
        qt_ref[hs, :] = (blk * cqt + q_sw * sqt).astype(BF16)
    rkv = _rms(ckv, kvn_ref[...]).astype(BF16)
    kk = _dot(rkv, wuk_ref[...])
    for h in range(MLA_HEADS):
        hs = slice(h * LANES, (h + 1) * LANES)
        k_ref[:, hs] = (kk[:, hs] + kpe).astype(BF16)
    vt_ref[:MLA_HW, :] = (_dot_nt(wuvt_ref[...], rkv) + ones_col).astype(BF16)

    lane = lax.broadcasted_iota(jnp.int32, (1, LANES), 1)
    lf = (jnp.minimum(fl, 0.0) - jnp.log1p(jnp.exp(-jnp.abs(fl)))) * LOG2E
    lf = jnp.where(lane < FOX_HEADS, lf, 0.0)
    row = lax.broadcasted_iota(jnp.int32, (tm, tm), 0)
    col = lax.broadcasted_iota(jnp.int32, (tm, tm), 1)
    tri = jnp.where(col <= row, 1.0, 0.0).astype(BF16)
    sums = _dot(tri, _pack3(lf))
    c_loc = sums + pltpu.roll(sums, LANES - FOX_HEADS, 1) + pltpu.roll(sums, LANES - 2 * FOX_HEADS, 1)
    c_loc = jnp.where(lane < FOX_HEADS, c_loc, 0.0)

    @pl.when(pl.program_id(1) == 0)
    def _():
        carry_ref[...] = jnp.zeros_like(carry_ref)

    c = c_loc + carry_ref[0:1, :]
    carry_ref[...] = jnp.broadcast_to(c[tm - 1:tm, :], carry_ref.shape)
    src = lax.broadcasted_iota(jnp.int32, (LANES, MLA_HW), 0)
    dst = lax.broadcasted_iota(jnp.int32, (LANES, MLA_HW), 1)
    src_h = src % FOX_HEADS
    c_lane = src_h * LANES + jnp.where(src_h % 2 == 0, FOX_DIM, 0) + src // FOX_HEADS
    place = jnp.where((dst == c_lane) & (src < 3 * FOX_HEADS), 1.0, 0.0).astype(BF16)
    k_extra = _dot(_pack3(c), place)

    fqt = _dot_nt(wfqt_ref[...], xb) * (FOX_DIM ** -0.5 * LOG2E)
    fk = _dot(xb, wfk_ref[...])
    fvt = _dot_nt(wfvt_ref[...], xb)
    half_row = lax.broadcasted_iota(jnp.int32, (FOX_DIM, tm), 0)
    neg_blk = jnp.where(half_row < 3, -1.0, 0.0).astype(BF16)
    ones_blk = jnp.where(half_row < 1, 1.0, 0.0).astype(BF16)
    lane_k = lax.broadcasted_iota(jnp.int32, (1, LANES), 1)
    for h in range(FOX_HEADS):
        off = (h % 2) * FOX_DIM
        base = MLA_HW + h * LANES
        data = slice(base + off, base + off + FOX_DIM)
        rest = slice(base + FOX_DIM - off, base + 2 * FOX_DIM - off)
        rows = slice(h * FOX_DIM, (h + 1) * FOX_DIM)
        qt_ref[data, :] = fqt[rows, :].astype(BF16)
        qt_ref[rest, :] = neg_blk
        vt_ref[data, :] = fvt[rows, :].astype(BF16)
        vt_ref[rest, :] = ones_blk
        pair = slice((h // 2) * LANES, (h // 2 + 1) * LANES)
        in_data = (lane_k >= off) & (lane_k < off + FOX_DIM)
        k_ref[:, base:base + LANES] = jnp.where(
            in_data, fk[:, pair], k_extra[:, h * LANES:(h + 1) * LANES]).astype(BF16)

    sg_ref[...] = _silu(_dot(xb, wg_ref[...])).astype(BF16)


def _input_specs(x, prev):
    b, s, d = x.shape
    tok = pl.BlockSpec((None, PROJ_TM, d), lambda bi, si: (bi, si, 0))
    if prev is None:
        return [x], [tok], [], []
    o, w_out, g, bias, layer = prev
    specs = [tok, tok] + [_layer_spec(a, layer) for a in (w_out, g, bias)]
    return [o, x, w_out, g, bias], specs, [tok], [jax.ShapeDtypeStruct((b, s, d), F32)]


def _even_proj(x, w, tabs, layer, prev=None):
    b, s, d = x.shape
    tm = PROJ_TM
    tok = lambda bi, si: (bi, si, 0)
    tok_t = lambda bi, si: (bi, 0, si)
    names = ("wlat", "wuqt", "wuk", "wuvt", "wfqt", "wfk", "wfvt", "wg", "qn", "kvn", "bf")
    operands, in_specs, x_spec, x_shape = _input_specs(x, prev)
    in_specs += [_layer_spec(w[k], layer) for k in names]
    in_specs += [pl.BlockSpec((LANES, tm), lambda bi, si: (0, si))] * 2
    in_specs += [pl.BlockSpec((tm, LANES), lambda bi, si: (si, 0))] * 2
    out_specs = [pl.BlockSpec((None, EVEN_HW, tm), tok_t), pl.BlockSpec((None, tm, EVEN_HW), tok),
                 pl.BlockSpec((None, EVEN_HW, tm), tok_t), pl.BlockSpec((None, tm, d), tok)]
    out_shape = [jax.ShapeDtypeStruct((b, EVEN_HW, s), BF16), jax.ShapeDtypeStruct((b, s, EVEN_HW), BF16),
                 jax.ShapeDtypeStruct((b, EVEN_HW, s), BF16), jax.ShapeDtypeStruct((b, s, d), BF16)]
    return pl.pallas_call(
        functools.partial(_even_proj_kernel, fused=prev is not None),
        grid=(b, s // tm),
        in_specs=in_specs,
        out_specs=out_specs + x_spec,
        out_shape=out_shape + x_shape,
        scratch_shapes=[pltpu.VMEM((8, LANES), F32)],
        compiler_params=_params(2),
        name="even_proj",
    )(*operands, *[w[k] for k in names], tabs["cqt"], tabs["sqt"], tabs["ck"], tabs["sk"])


def _attn_write_out(tile, accs, sg_ref, o_ref):
    t = ATT_T
    rows = pl.ds(pl.multiple_of(tile * t, t), t)
    row = lax.broadcasted_iota(jnp.int32, (LANES, t), 0)
    for hp in range(len(accs) // 2):
        a0 = accs[2 * hp]
        a1 = accs[2 * hp + 1]
        pair_t = jnp.where(row < MLA_V, a0 / a0[MLA_V:MLA_V + 1, :], a1 / a1[0:1, :])
        cols = slice(hp * LANES, (hp + 1) * LANES)
        o_ref[rows, cols] = (pair_t.T * sg_ref[rows, cols].astype(F32)).astype(BF16)


def _attn_query_tile(qi, prev, qt_ref, k_ref, vt_ref, sg_ref, o_ref, sa_ref, sb_ref, sc_ref, mc_ref):
    t = ATT_T
    hf = t // 2
    n_heads = qt_ref.shape[0] // LANES
    nq = qt_ref.shape[1] // t
    heads = [slice(h * LANES, (h + 1) * LANES) for h in range(n_heads)]

    def q_slice(tile, start=0, size=None):
        size = t if size is None else size
        return pl.ds(pl.multiple_of(tile * t + start, size), size)

    def key_slice(kj, start=0, size=None):
        size = t if size is None else size
        return pl.ds(pl.multiple_of(kj * t + start, size), size)

    def col_max(s):
        return jnp.max(s, axis=0, keepdims=True)

    def causal_mask(s):
        n = s.shape[0]
        keep = (lax.broadcasted_iota(jnp.int32, (n, n), 0) <= lax.broadcasted_iota(jnp.int32, (n, n), 1))
        return jnp.where(keep, s, -jnp.inf)

    def scores_head(q_tile, kj, buf, h):
        s = _dot(k_ref[key_slice(kj), heads[h]], qt_ref[heads[h], q_slice(q_tile)])
        buf[h] = s
        return col_max(s)

    def diag_scores_head(buf, h):
        early = _dot(k_ref[key_slice(qi, 0, hf), heads[h]], qt_ref[heads[h], q_slice(qi)])
        late = _dot(k_ref[key_slice(qi, hf, hf), heads[h]], qt_ref[heads[h], q_slice(qi, hf, hf)])
        early = jnp.concatenate([causal_mask(early[:, :hf]), early[:, hf:]], axis=1)
        late = causal_mask(late)
        buf[h, :hf, :] = early
        buf[h, hf:, hf:] = late
        return jnp.concatenate([col_max(early[:, :hf]),
                                jnp.maximum(col_max(early[:, hf:]), col_max(late))], axis=1)

    def process_head(kj, s, h, tile_max, m, acc):
        m_new = jnp.maximum(m, tile_max)
        p = jnp.exp2(s - m_new).astype(BF16)
        return m_new, jnp.exp2(m - m_new) * acc + _dot(vt_ref[heads[h], key_slice(kj)], p)

    def diag_process_head(buf, h, tile_max, m, acc):
        m_new = jnp.maximum(m, tile_max)
        p_early = jnp.exp2(buf[h, :hf, :] - m_new).astype(BF16)
        p_late = jnp.exp2(buf[h, hf:, hf:] - m_new[:, hf:]).astype(BF16)
        pv_early = _dot(vt_ref[heads[h], key_slice(qi, 0, hf)], p_early)
        pv_late = _dot(vt_ref[heads[h], key_slice(qi, hf, hf)], p_late)
        scaled = jnp.exp2(m - m_new) * acc + pv_early
        return m_new, jnp.concatenate([scaled[:, :hf], scaled[:, hf:] + pv_late], axis=1)

    def stage(kj, cur, nxt, tile_max, state, diag_next=False):
        next_max, out = [], []
        for h in range(n_heads):
            next_max.append(diag_scores_head(nxt, h) if diag_next else scores_head(qi, kj + 1, nxt, h))
            out.append(process_head(kj, cur[h], h, tile_max[h], *state[h]))
        return tuple(next_max), tuple(out)

    def two_stages(i, carry):
        tile_max, state = carry
        tile_max, state = stage(2 * i + 1, sa_ref, sb_ref, tile_max, state)
        return stage(2 * i + 2, sb_ref, sa_ref, tile_max, state)

    def last_tile(buf, tile_max, state):
        out = []
        for h in range(n_heads):
            mc_ref[h, 0:1, :] = scores_head(jnp.minimum(qi + 1, nq - 1), 0, sc_ref, h)
            out.append(diag_process_head(buf, h, tile_max[h], *state[h]))
        return tuple(out)

    def first_stage(state, diag_next=False):
        tile_max = tuple(mc_ref[h, 0:1, :] for h in range(n_heads))
        return stage(0, sc_ref, sa_ref, tile_max, state, diag_next)

    def fresh():
        return tuple((jnp.full((1, t), -jnp.inf, F32), jnp.zeros((LANES, t), F32)) for _ in range(n_heads))

    def accs_of(state):
        return tuple(acc for _, acc in state)

    def case_first(prev):
        tile_max = tuple(diag_scores_head(sa_ref, h) for h in range(n_heads))
        return accs_of(last_tile(sa_ref, tile_max, fresh()))

    def case_second(prev):
        _attn_write_out(qi - 1, prev, sg_ref, o_ref)
        tile_max, state = first_stage(fresh(), diag_next=True)
        return accs_of(last_tile(sa_ref, tile_max, state))

    def case_even(prev):
        _attn_write_out(qi - 1, prev, sg_ref, o_ref)
        tile_max, state = first_stage(fresh())
        tile_max, state = lax.fori_loop(0, (qi - 2) // 2, two_stages, (tile_max, state))
        tile_max, state = stage(qi - 1, sa_ref, sb_ref, tile_max, state, diag_next=True)
        return accs_of(last_tile(sb_ref, tile_max, state))

    def case_odd(prev):
        _attn_write_out(qi - 1, prev, sg_ref, o_ref)
        tile_max, state = first_stage(fresh())
        tile_max, state = lax.fori_loop(0, (qi - 3) // 2, two_stages, (tile_max, state))
        tile_max, state = stage(qi - 2, sa_ref, sb_ref, tile_max, state)
        tile_max, state = stage(qi - 1, sb_ref, sa_ref, tile_max, state, diag_next=True)
        return accs_of(last_tile(sa_ref, tile_max, state))

    def case_later(prev):
        return lax.cond(qi % 2 == 0, case_even, case_odd, prev)

    def case_not_first(prev):
        return lax.cond(qi == 1, case_second, case_later, prev)

    return lax.cond(qi == 0, case_first, case_not_first, prev)


def _causal_attn_kernel(qt_ref, k_ref, vt_ref, sg_ref, o_ref, sa_ref, sb_ref, sc_ref, mc_ref):
    n_heads = qt_ref.shape[0] // LANES
    nq = qt_ref.shape[1] // ATT_T

    def body(qi, prev):
        return _attn_query_tile(qi, prev, qt_ref, k_ref, vt_ref, sg_ref, o_ref, sa_ref, sb_ref, sc_ref, mc_ref)

    none_yet = tuple(jnp.zeros((LANES, ATT_T), F32) for _ in range(n_heads))
    last = lax.fori_loop(0, nq, body, none_yet)
    _attn_write_out(nq - 1, last, sg_ref, o_ref)


def _causal_attn(qt, k, vt, sg):
    b, s, hw = k.shape
    t = ATT_T
    gw = ATT_HG * LANES
    ow = ATT_HG * MLA_V
    score_buf = pltpu.VMEM((ATT_HG, t, t), F32)
    return pl.pallas_call(
        _causal_attn_kernel,
        grid=(b, hw // gw),
        in_specs=[pl.BlockSpec((None, gw, s), lambda bi, g: (bi, g, 0)),
                  pl.BlockSpec((None, s, gw), lambda bi, g: (bi, 0, g)),
                  pl.BlockSpec((None, gw, s), lambda bi, g: (bi, g, 0)),
                  pl.BlockSpec((None, s, ow), lambda bi, g: (bi, 0, g))],
        out_specs=pl.BlockSpec((None, s, ow), lambda bi, g: (bi, 0, g)),
        out_shape=jax.ShapeDtypeStruct((b, s, (hw // LANES) * MLA_V), BF16),
        scratch_shapes=[score_buf, score_buf, score_buf, pltpu.VMEM((ATT_HG, 8, t), F32)],
        compiler_params=_params(2),
        name="causal_attn",
    )(qt, k, vt, sg)


def _out_kernel(o_ref, x_ref, w_ref, g_ref, b_ref, y_ref):
    y_ref[...] = _residual_norm(o_ref, x_ref, w_ref, g_ref, b_ref)


def _out_proj(o, x, w_out, g, bias, layer):
    b, s, d = x.shape
    tm = PROJ_TM
    tok = lambda bi, si: (bi, si, 0)
    return pl.pallas_call(
        _out_kernel,
        grid=(b, s // tm),
        in_specs=[pl.BlockSpec((None, tm, d), tok), pl.BlockSpec((None, tm, d), tok),
                  _layer_spec(w_out, layer), _layer_spec(g, layer), _layer_spec(bias, layer)],
        out_specs=pl.BlockSpec((None, tm, d), tok),
        out_shape=jax.ShapeDtypeStruct((b, s, d), F32),
        compiler_params=_params(2),
        name="out_proj_ln",
    )(o, x, w_out, g, bias)


def _odd_proj_kernel(*refs, fused):
    x, refs = _layer_input(refs, fused)
    (wqt_ref, wk_ref, wvt_ref, wg_ref, c_ref, s_ref, ct_ref, st_ref,
     qt_ref, k_ref, vt_ref, sg_ref) = refs[:12]
    if fused:
        refs[12][...] = x
    d = SWA_HEADS * SWA_DIM
    half = SWA_DIM // 2
    xb = x.astype(BF16)

    kk = _dot(xb, wk_ref[...])
    lane = lax.broadcasted_iota(jnp.int32, (1, LANES), 1)
    first_half = (lane % SWA_DIM) < half
    k_sw = jnp.where(first_half, pltpu.roll(kk, LANES - half, 1), pltpu.roll(kk, half, 1))
    k_ref[...] = (kk * c_ref[...] + k_sw * s_ref[...]).astype(BF16)

    qq = _dot_nt(wqt_ref[...], xb)
    ctt = ct_ref[...]
    stt = st_ref[...]
    scale = SWA_DIM ** -0.5 * LOG2E
    for p in range(d // LANES):
        blk = qq[p * LANES:(p + 1) * LANES, :]
        q_sw = jnp.concatenate([blk[half:2 * half], blk[:half], blk[3 * half:], blk[2 * half:3 * half]], axis=0)
        qt_ref[p * LANES:(p + 1) * LANES, :] = ((blk * ctt + q_sw * stt) * scale).astype(BF16)

    vt_ref[...] = _dot_nt(wvt_ref[...], xb).astype(BF16)
    sg_ref[...] = _silu(_dot(xb, wg_ref[...])).astype(BF16)


def _layer_spec(a, layer):
    return pl.BlockSpec((None,) + a.shape[1:], lambda bi, si: (layer,) + (0,) * (a.ndim - 1))


def _odd_proj(x, w, tabs, layer, prev=None):
    b, s, d = x.shape
    tm = PROJ_TM
    tok = lambda bi, si: (bi, si, 0)
    tok_t = lambda bi, si: (bi, 0, si)
    tab = pl.BlockSpec((tm, LANES), lambda bi, si: (si, 0))
    tab_t = pl.BlockSpec((LANES, tm), lambda bi, si: (0, si))
    names = ("wqt", "wk", "wvt", "wg")
    operands, in_specs, x_spec, x_shape = _input_specs(x, prev)
    return pl.pallas_call(
        functools.partial(_odd_proj_kernel, fused=prev is not None),
        grid=(b, s // tm),
        in_specs=in_specs + [_layer_spec(w[k], layer) for k in names] + [tab, tab, tab_t, tab_t],
        out_specs=[pl.BlockSpec((None, d, tm), tok_t), pl.BlockSpec((None, tm, LANES), tok),
                   pl.BlockSpec((None, LANES, tm), tok_t), pl.BlockSpec((None, tm, d), tok)] + x_spec,
        out_shape=[jax.ShapeDtypeStruct((b, d, s), BF16), jax.ShapeDtypeStruct((b, s, LANES), BF16),
                   jax.ShapeDtypeStruct((b, LANES, s), BF16), jax.ShapeDtypeStruct((b, s, d), BF16)] + x_shape,
        compiler_params=_params(2),
        name="odd_proj",
    )(*operands, *[w[k] for k in names], tabs["c"], tabs["s"], tabs["ct"], tabs["st"])


def _swa_kernel(sink_ref, qt_ref, k_ref, vt_ref, sg_ref, o_ref):
    n_blk = qt_ref.shape[1] // QBLK
    n_pair = qt_ref.shape[0] // LANES
    win = 2 * QBLK
    qi = pl.program_id(1)
    row = lax.broadcasted_iota(jnp.int32, (LANES, QBLK), 0)
    lo_rows = row < SWA_DIM
    rel = (lax.broadcasted_iota(jnp.int32, (win, QBLK), 1) - lax.broadcasted_iota(jnp.int32, (win, QBLK), 0))
    sink_row = jnp.concatenate([jnp.full((1, QBLK), sink_ref[c] * LOG2E, F32) for c in range(2 * n_pair)],
                               axis=1)
    ones_blk = jnp.ones((16, win), BF16)

    def window_start(blk):
        return pl.multiple_of(jnp.maximum(qi * n_blk + blk - 1, 0) * QBLK, QBLK)

    def scores(blk):
        q0 = blk * QBLK
        cols = []
        for p in range(n_pair):
            qp = qt_ref[p * LANES:(p + 1) * LANES, q0:q0 + QBLK]
            zero = jnp.zeros_like(qp)
            cols.append(jnp.where(lo_rows, qp, zero))
            cols.append(jnp.where(lo_rows, zero, qp))
        s = _dot(k_ref[pl.ds(window_start(blk), win), :], jnp.concatenate(cols, axis=1))
        shift = jnp.where(qi * n_blk + blk == 0, 0, QBLK)
        diff = rel + shift
        band = (diff >= 0) & (diff < WINDOW)
        return jnp.concatenate([jnp.where(band, s[:, c * QBLK:(c + 1) * QBLK], -jnp.inf)
                                for c in range(2 * n_pair)], axis=1)

    def process(blk, s):
        q0 = blk * QBLK
        m = jnp.maximum(jnp.max(s, axis=0, keepdims=True), sink_row)
        p = jnp.exp2(s - m).astype(BF16)
        vt = jnp.concatenate([vt_ref[:, pl.ds(window_start(blk), win)], ones_blk], axis=0)
        ot = _dot(vt, p)
        inv = 1.0 / (ot[LANES:LANES + 1, :] + jnp.exp2(sink_row - m))
        for pr in range(n_pair):
            ca = slice((2 * pr) * QBLK, (2 * pr + 1) * QBLK)
            cb = slice((2 * pr + 1) * QBLK, (2 * pr + 2) * QBLK)
            pair = jnp.concatenate([ot[:SWA_DIM, ca] * inv[:, ca], ot[SWA_DIM:LANES, cb] * inv[:, cb]],
                                   axis=0).T
            cols = slice(pr * LANES, (pr + 1) * LANES)
            o_ref[q0:q0 + QBLK, cols] = (pair * sg_ref[q0:q0 + QBLK, cols].astype(F32)).astype(BF16)

    s_next = scores(0)
    for blk in range(n_blk):
        s_cur = s_next
        if blk + 1 < n_blk:
            s_next = scores(blk + 1)
        process(blk, s_cur)


def _swa_attn(sinks, qt, k, vt, sg):
    b, d, s = qt.shape
    tq = SWA_TQ
    tok = lambda bi, qi: (bi, qi, 0)
    return pl.pallas_call(
        _swa_kernel,
        grid=(b, s // tq),
        in_specs=[pl.BlockSpec(memory_space=pltpu.SMEM),
                  pl.BlockSpec((None, d, tq), lambda bi, qi: (bi, 0, qi)),
                  pl.BlockSpec((None, s, LANES), lambda bi, qi: (bi, 0, 0)),
                  pl.BlockSpec((None, LANES, s), lambda bi, qi: (bi, 0, 0)),
                  pl.BlockSpec((None, tq, d), tok)],
        out_specs=pl.BlockSpec((None, tq, d), tok),
        out_shape=jax.ShapeDtypeStruct((b, s, d), BF16),
        compiler_params=_params(2),
        name="swa_attn",
    )(sinks, qt, k, vt, sg)


def _pad_heads(w, n_heads, dim, offsets):
    k = w.shape[0]
    n_off = len(offsets)
    w4 = w.reshape(k, n_heads // n_off, n_off, dim)
    parts = [jnp.pad(w4[:, :, i, :], ((0, 0), (0, 0), (off, LANES - off - dim)))
             for i, off in enumerate(offsets)]
    return jnp.stack(parts, axis=2).reshape(k, n_heads * LANES)


def _swap_halves(w, n_heads, dim):
    k = w.shape[0]
    w3 = w.reshape(k, n_heads, dim)
    return jnp.concatenate([w3[..., dim // 2:], w3[..., :dim // 2]], axis=-1).reshape(k, n_heads * dim)


def _even_weights(w_in, q_norm, w_uq, kv_norm, w_ukv, b_f):
    sizes = (MLA_Q_RANK, MLA_KV_RANK, MLA_ROPE, FOX_HEADS * FOX_DIM, FOX_HEADS * FOX_DIM,
             FOX_HEADS * FOX_DIM, FOX_HEADS, MLA_HEADS * MLA_V + FOX_HEADS * FOX_DIM)
    cuts = [int(c) for c in np.cumsum(sizes)[:-1]]
    w_cq, w_ckv, w_kpe, w_fq, w_fk, w_fv, w_f, w_g = jnp.split(w_in, cuts, axis=1)
    place = lambda w: _pad_heads(w, 1, MLA_ROPE, (MLA_NOPE,))
    wlat = jnp.concatenate([w_cq, w_ckv, place(w_kpe), _pad_heads(w_f, 1, FOX_HEADS, (0,))], axis=1)
    v_off = (0, MLA_V)
    ukv3 = w_ukv.reshape(MLA_KV_RANK, MLA_HEADS, MLA_NOPE + MLA_V)
    uk = ukv3[..., :MLA_NOPE].reshape(MLA_KV_RANK, MLA_HEADS * MLA_NOPE)
    uv = ukv3[..., MLA_NOPE:].reshape(MLA_KV_RANK, MLA_HEADS * MLA_V)
    bf = lambda a: a.astype(BF16)
    return dict(wlat=bf(wlat),
                wuqt=bf(_pad_heads(w_uq, MLA_HEADS, MLA_NOPE + MLA_ROPE, (0,)).T),
                wuk=bf(_pad_heads(uk, MLA_HEADS, MLA_NOPE, (0,))),
                wuvt=bf(_pad_heads(uv, MLA_HEADS, MLA_V, v_off).T),
                wfqt=bf(w_fq.T), wfk=bf(w_fk), wfvt=bf(w_fv.T),
                wg=bf(w_g), qn=q_norm.reshape(1, -1), kvn=kv_norm.reshape(1, -1),
                bf=jnp.pad(b_f, (0, LANES - FOX_HEADS)).reshape(1, LANES))


def _pair_heads(w):
    lead = w.shape[:-1]
    w4 = w.reshape(*lead, SWA_KV_HEADS, SWA_HEADS // SWA_KV_HEADS, -1)
    return jnp.swapaxes(w4, -3, -2).reshape(*lead, -1)


def _odd_weights(w_in, sinks, w_out):
    d = SWA_HEADS * SWA_DIM
    dkv = SWA_KV_HEADS * SWA_DIM
    w_q, w_k, w_v, w_g = jnp.split(w_in, [d, d + dkv, d + 2 * dkv], axis=1)
    pair_rows = lambda a: jnp.swapaxes(
        a.reshape(SWA_KV_HEADS, SWA_HEADS // SWA_KV_HEADS, SWA_DIM, -1), 0, 1).reshape(a.shape)
    return dict(wqt=pair_rows(w_q.T).astype(BF16), wk=w_k.astype(BF16), wvt=w_v.T.astype(BF16),
                wg=_pair_heads(w_g).astype(BF16),
                sinks=_pair_heads(sinks.reshape(-1, 1).T).reshape(-1),
                w_out=pair_rows(w_out).astype(BF16))


def _rope_tables(s):
    pos = np.arange(s, dtype=np.float64)

    def cs(dim):
        inv = ROPE_THETA ** (-np.arange(0, dim, 2, dtype=np.float64) / dim)
        ang = pos[:, None] * inv[None, :]
        cos, sin = np.cos(ang), np.sin(ang)
        return np.concatenate([cos, cos], axis=1), np.concatenate([-sin, sin], axis=1)

    c32, s32 = cs(MLA_ROPE)
    z = lambda n: np.zeros((s, n))
    pad = LANES - MLA_NOPE - MLA_ROPE
    scale = (MLA_NOPE + MLA_ROPE) ** -0.5 * LOG2E
    even = dict(cqt=(np.concatenate([np.ones((s, MLA_NOPE)), c32, z(pad)], axis=1) * scale).T,
                sqt=(np.concatenate([z(MLA_NOPE), s32, z(pad)], axis=1) * scale).T,
                ck=np.concatenate([z(MLA_NOPE), c32, z(pad)], axis=1),
                sk=np.concatenate([z(MLA_NOPE), s32, z(pad)], axis=1))
    c64, s64 = cs(SWA_DIM)
    odd = dict(c=np.concatenate([c64, c64], axis=1), s=np.concatenate([s64, s64], axis=1))
    odd.update(ct=odd["c"].T, st=odd["s"].T)
    as_f32 = lambda d: {k: jnp.asarray(np.ascontiguousarray(v), dtype=F32) for k, v in d.items()}
    return as_f32(even), as_f32(odd)


def kernel(x, even_w_in, even_q_norm, even_w_uq, even_kv_norm, even_w_ukv, even_b_f, even_w_out,
           even_ln_g, even_ln_b, odd_w_in, odd_sinks, odd_w_out, odd_ln_g, odd_ln_b):
    s = x.shape[1]
    even_tabs, odd_tabs = _rope_tables(s)
    bf = lambda a: a.astype(BF16)
    we = jax.vmap(_even_weights)(bf(even_w_in), even_q_norm, bf(even_w_uq), even_kv_norm, bf(even_w_ukv),
                                 even_b_f)
    wo = jax.vmap(_odd_weights)(bf(odd_w_in), odd_sinks, bf(odd_w_out))
    even_w_out = even_w_out.astype(BF16)
    row = lambda a: a[:, None, :]
    closing = {0: (even_w_out, row(even_ln_g), row(even_ln_b)),
               1: (wo["w_out"], row(odd_ln_g), row(odd_ln_b))}
    prev = None
    for layer in range(DEPTH):
        j = layer // 2
        if layer % 2 == 0:
            qt, k, vt, sg, *x_new = _even_proj(x, we, even_tabs, j, prev)
            x = x_new[0] if x_new else x
            o = _causal_attn(qt, k, vt, sg)
        else:
            qt, k, vt, sg, *x_new = _odd_proj(x, wo, odd_tabs, j, prev)
            x = x_new[0] if x_new else x
            o = _swa_attn(wo["sinks"][j], qt, k, vt, sg)
        prev = (o, *closing[layer % 2], j)
    o, w_out, g, bias, j = prev
    return _out_proj(o, x, w_out, g, bias, j)
```

```python
import functools
import math

import jax
import jax.numpy as jnp
import numpy as np
from jax import lax
from jax.experimental import pallas as pl
from jax.experimental.pallas import tpu as pltpu

D_MODEL = 1024
DEPTH = 4
ROPE_THETA = 10000.0
QBLK = 128
MLA_HEADS = 8
MLA_NOPE = 64
MLA_ROPE = 32
MLA_V = 64
MLA_Q_RANK = 256
MLA_KV_RANK = 128
FOX_HEADS = 8
FOX_DIM = 64
SWA_HEADS = 16
SWA_KV_HEADS = 2
SWA_DIM = 64
WINDOW = 128
RMS_EPS = 1e-6
LN_EPS = 1e-5
ALPHA = (2 * DEPTH) ** 0.25
LOG2E = math.log2(math.e)

LANES = 128
VMEM_LIMIT = 56 * 1024 * 1024
BF16 = jnp.bfloat16
F32 = jnp.float32

PROJ_TM = 512
ATT_T = 512
ATT_HG = 4
SWA_TQ = 512
EVEN_HEADS = MLA_HEADS + FOX_HEADS
EVEN_HW = EVEN_HEADS * LANES
MLA_HW = MLA_HEADS * LANES


def _dot(a, b):
    return jnp.dot(a, b, preferred_element_type=F32)


def _dot_nt(a, b):
    return lax.dot_general(a, b, (((1,), (1,)), ((), ())), preferred_element_type=F32)


def _rms(t, g):
    return t * lax.rsqrt(jnp.mean(t * t, axis=-1, keepdims=True) + RMS_EPS) * g


def _silu(g):
    return g / (1.0 + jnp.exp(-g))


def _pack3(v):
    hi = v.astype(BF16).astype(F32)
    r1 = v - hi
    mid = r1.astype(BF16).astype(F32)
    lo = r1 - mid
    return (hi + pltpu.roll(mid, FOX_HEADS, 1) + pltpu.roll(lo, 2 * FOX_HEADS, 1)).astype(BF16)


def _params(n_axes, flags=None):
    return pltpu.CompilerParams(dimension_semantics=("arbitrary",) * n_axes,
                                vmem_limit_bytes=VMEM_LIMIT, flags=flags)


def _residual_norm(o_ref, x_ref, w_ref, g_ref, b_ref):
    z = ALPHA * x_ref[...] + _dot(o_ref[...], w_ref[...])
    mu = jnp.mean(z, axis=-1, keepdims=True)
    zc = z - mu
    var = jnp.mean(zc * zc, axis=-1, keepdims=True)
    return zc * lax.rsqrt(var + LN_EPS) * g_ref[...] + b_ref[...]


def _layer_input(refs, fused):
    if not fused:
        return refs[0][...], refs[1:]
    x = _residual_norm(*refs[:5])
    return x, refs[5:]


def _even_proj_kernel(*refs, fused):
    x, refs = _layer_input(refs, fused)
    (wlat_ref, wuqt_ref, wuk_ref, wuvt_ref, wfqt_ref, wfk_ref, wfvt_ref, wg_ref, qn_ref, kvn_ref,
     bf_ref, cqt_ref, sqt_ref, ck_ref, sk_ref, qt_ref, k_ref, vt_ref, sg_ref) = refs[:19]
    carry_ref = refs[-1]
    if fused:
        refs[19][...] = x
    tm = x.shape[0]
    xb = x.astype(BF16)

    hrow = lax.broadcasted_iota(jnp.int32, (MLA_HW, 1), 0) % (2 * LANES)
    ones_col = jnp.where((hrow == MLA_V) | (hrow == LANES), 1.0, 0.0).astype(F32)
    lat = _dot(xb, wlat_ref[...])
    o = 0
    cq = lat[:, o:o + MLA_Q_RANK]
    o += MLA_Q_RANK
    ckv = lat[:, o:o + MLA_KV_RANK]
    o += MLA_KV_RANK
    kp = lat[:, o:o + LANES]
    hr = MLA_ROPE // 2
    lane_p = lax.broadcasted_iota(jnp.int32, (1, LANES), 1)
    kp_sw = jnp.where(lane_p < MLA_NOPE + hr, pltpu.roll(kp, LANES - hr, 1), pltpu.roll(kp, hr, 1))
    kpe = kp * ck_ref[...] + kp_sw * sk_ref[...]
    o += LANES
    fl = lat[:, o:o + LANES] + bf_ref[...]

    rq = _rms(cq, qn_ref[...]).astype(BF16)
    qa = _dot_nt(wuqt_ref[...], rq)
    cqt = cqt_ref[...]
    sqt = sqt_ref[...]
    pe = MLA_NOPE
    for h in range(MLA_HEADS):
        hs = slice(h * LANES, (h + 1) * LANES)
        blk = qa[hs, :]
        q_sw = jnp.concatenate([blk[:pe], blk[pe + hr:pe + 2 * hr], blk[pe:pe + hr], blk[pe + 2 * hr:]], axis=0)
        qt_ref[hs, :] = (blk * cqt + q_sw * sqt).astype(BF16)
    rkv = _rms(ckv, kvn_ref[...]).astype(BF16)
    kk = _dot(rkv, wuk_ref[...])
    for h in range(MLA_HEADS):
        hs = slice(h * LANES, (h + 1) * LANES)
        k_ref[:, hs] = (kk[:, hs] + kpe).astype(BF16)
    vt_ref[:MLA_HW, :] = (_dot_nt(wuvt_ref[...], rkv) + ones_col).astype(BF16)

    lane = lax.broadcasted_iota(jnp.int32, (1, LANES), 1)
    lf = (jnp.minimum(fl, 0.0) - jnp.log1p(jnp.exp(-jnp.abs(fl)))) * LOG2E
    lf = jnp.where(lane < FOX_HEADS, lf, 0.0)
    row = lax.broadcasted_iota(jnp.int32, (tm, tm), 0)
    col = lax.broadcasted_iota(jnp.int32, (tm, tm), 1)
    tri = jnp.where(col <= row, 1.0, 0.0).astype(BF16)
    sums = _dot(tri, _pack3(lf))
    c_loc = sums + pltpu.roll(sums, LANES - FOX_HEADS, 1) + pltpu.roll(sums, LANES - 2 * FOX_HEADS, 1)
    c_loc = jnp.where(lane < FOX_HEADS, c_loc, 0.0)

    @pl.when(pl.program_id(1) == 0)
    def _():
        carry_ref[...] = jnp.zeros_like(carry_ref)

    c = c_loc + carry_ref[0:1, :]
    carry_ref[...] = jnp.broadcast_to(c[tm - 1:tm, :], carry_ref.shape)
    src = lax.broadcasted_iota(jnp.int32, (LANES, MLA_HW), 0)
    dst = lax.broadcasted_iota(jnp.int32, (LANES, MLA_HW), 1)
    src_h = src % FOX_HEADS
    c_lane = src_h * LANES + jnp.where(src_h % 2 == 0, FOX_DIM, 0) + src // FOX_HEADS
    place = jnp.where((dst == c_lane) & (src < 3 * FOX_HEADS), 1.0, 0.0).astype(BF16)
    k_extra = _dot(_pack3(c), place)

    fqt = _dot_nt(wfqt_ref[...], xb) * (FOX_DIM ** -0.5 * LOG2E)
    fk = _dot(xb, wfk_ref[...])
    fvt = _dot_nt(wfvt_ref[...], xb)
    half_row = lax.broadcasted_iota(jnp.int32, (FOX_DIM, tm), 0)
    neg_blk = jnp.where(half_row < 3, -1.0, 0.0).astype(BF16)
    ones_blk = jnp.where(half_row < 1, 1.0, 0.0).astype(BF16)
    lane_k = lax.broadcasted_iota(jnp.int32, (1, LANES), 1)
    for h in range(FOX_HEADS):
        off = (h % 2) * FOX_DIM
        base = MLA_HW + h * LANES
        data = slice(base + off, base + off + FOX_DIM)
        rest = slice(base + FOX_DIM - off, base + 2 * FOX_DIM - off)
        rows = slice(h * FOX_DIM, (h + 1) * FOX_DIM)
        qt_ref[data, :] = fqt[rows, :].astype(BF16)
        qt_ref[rest, :] = neg_blk
        vt_ref[data, :] = fvt[rows, :].astype(BF16)
        vt_ref[rest, :] = ones_blk
        pair = slice((h // 2) * LANES, (h // 2 + 1) * LANES)
        in_data = (lane_k >= off) & (lane_k < off + FOX_DIM)
        k_ref[:, base:base + LANES] = jnp.where(
            in_data, fk[:, pair], k_extra[:, h * LANES:(h + 1) * LANES]).astype(BF16)

    sg_ref[...] = _silu(_dot(xb, wg_ref[...])).astype(BF16)


def _input_specs(x, prev):
    b, s, d = x.shape
    tok = pl.BlockSpec((None, PROJ_TM, d), lambda bi, si: (bi, si, 0))
    if prev is None:
        return [x], [tok], [], []
    o, w_out, g, bias, layer = prev
    specs = [tok, tok] + [_layer_spec(a, layer) for a in (w_out, g, bias)]
    return [o, x, w_out, g, bias], specs, [tok], [jax.ShapeDtypeStruct((b, s, d), F32)]


def _even_proj(x, w, tabs, layer, prev=None):
    b, s, d = x.shape
    tm = PROJ_TM
    tok = lambda bi, si: (bi, si, 0)
    tok_t = lambda bi, si: (bi, 0, si)
    names = ("wlat", "wuqt", "wuk", "wuvt", "wfqt", "wfk", "wfvt", "wg", "qn", "kvn", "bf")
    operands, in_specs, x_spec, x_shape = _input_specs(x, prev)
    in_specs += [_layer_spec(w[k], layer) for k in names]
    in_specs += [pl.BlockSpec((LANES, tm), lambda bi, si: (0, si))] * 2
    in_specs += [pl.BlockSpec((tm, LANES), lambda bi, si: (si, 0))] * 2
    out_specs = [pl.BlockSpec((None, EVEN_HW, tm), tok_t), pl.BlockSpec((None, tm, EVEN_HW), tok),
                 pl.BlockSpec((None, EVEN_HW, tm), tok_t), pl.BlockSpec((None, tm, d), tok)]
    out_shape = [jax.ShapeDtypeStruct((b, EVEN_HW, s), BF16), jax.ShapeDtypeStruct((b, s, EVEN_HW), BF16),
                 jax.ShapeDtypeStruct((b, EVEN_HW, s), BF16), jax.ShapeDtypeStruct((b, s, d), BF16)]
    return pl.pallas_call(
        functools.partial(_even_proj_kernel, fused=prev is not None),
        grid=(b, s // tm),
        in_specs=in_specs,
        out_specs=out_specs + x_spec,
        out_shape=out_shape + x_shape,
        scratch_shapes=[pltpu.VMEM((8, LANES), F32)],
        compiler_params=_params(2),
        name="even_proj",
    )(*operands, *[w[k] for k in names], tabs["cqt"], tabs["sqt"], tabs["ck"], tabs["sk"])


def _attn_write_out(tile, accs, sg_ref, o_ref):
    t = ATT_T
    rows = pl.ds(pl.multiple_of(tile * t, t), t)
    row = lax.broadcasted_iota(jnp.int32, (LANES, t), 0)
    for hp in range(len(accs) // 2):
        a0 = accs[2 * hp]
        a1 = accs[2 * hp + 1]
        pair_t = jnp.where(row < MLA_V, a0 / a0[MLA_V:MLA_V + 1, :], a1 / a1[0:1, :])
        cols = slice(hp * LANES, (hp + 1) * LANES)
        o_ref[rows, cols] = (pair_t.T * sg_ref[rows, cols].astype(F32)).astype(BF16)


def _attn_query_tile(qi, prev, qt_ref, k_ref, vt_ref, sg_ref, o_ref, sa_ref, sb_ref, sc_ref, mc_ref):
    t = ATT_T
    hf = t // 2
    n_heads = qt_ref.shape[0] // LANES
    nq = qt_ref.shape[1] // t
    heads = [slice(h * LANES, (h + 1) * LANES) for h in range(n_heads)]

    def q_slice(tile, start=0, size=None):
        size = t if size is None else size
        return pl.ds(pl.multiple_of(tile * t + start, size), size)

    def key_slice(kj, start=0, size=None):
        size = t if size is None else size
        return pl.ds(pl.multiple_of(kj * t + start, size), size)

    def col_max(s):
        return jnp.max(s, axis=0, keepdims=True)

    def causal_mask(s):
        n = s.shape[0]
        keep = (lax.broadcasted_iota(jnp.int32, (n, n), 0) <= lax.broadcasted_iota(jnp.int32, (n, n), 1))
        return jnp.where(keep, s, -jnp.inf)

    def scores_head(q_tile, kj, buf, h):
        s = _dot(k_ref[key_slice(kj), heads[h]], qt_ref[heads[h], q_slice(q_tile)])
        buf[h] = s
        return col_max(s)

    def diag_scores_head(buf, h):
        early = _dot(k_ref[key_slice(qi, 0, hf), heads[h]], qt_ref[heads[h], q_slice(qi)])
        late = _dot(k_ref[key_slice(qi, hf, hf), heads[h]], qt_ref[heads[h], q_slice(qi, hf, hf)])
        early = jnp.concatenate([causal_mask(early[:, :hf]), early[:, hf:]], axis=1)
        late = causal_mask(late)
        buf[h, :hf, :] = early
        buf[h, hf:, hf:] = late
        return jnp.concatenate([col_max(early[:, :hf]),
                                jnp.maximum(col_max(early[:, hf:]), col_max(late))], axis=1)

    def process_head(kj, s, h, tile_max, m, acc):
        m_new = jnp.maximum(m, tile_max)
        p = jnp.exp2(s - m_new).astype(BF16)
        return m_new, jnp.exp2(m - m_new) * acc + _dot(vt_ref[heads[h], key_slice(kj)], p)

    def diag_process_head(buf, h, tile_max, m, acc):
        m_new = jnp.maximum(m, tile_max)
        p_early = jnp.exp2(buf[h, :hf, :] - m_new).astype(BF16)
        p_late = jnp.exp2(buf[h, hf:, hf:] - m_new[:, hf:]).astype(BF16)
        pv_early = _dot(vt_ref[heads[h], key_slice(qi, 0, hf)], p_early)
        pv_late = _dot(vt_ref[heads[h], key_slice(qi, hf, hf)], p_late)
        scaled = jnp.exp2(m - m_new) * acc + pv_early
        return m_new, jnp.concatenate([scaled[:, :hf], scaled[:, hf:] + pv_late], axis=1)

    def stage(kj, cur, nxt, tile_max, state, diag_next=False):
        next_max, out = [], []
        for h in range(n_heads):
            next_max.append(diag_scores_head(nxt, h) if diag_next else scores_head(qi, kj + 1, nxt, h))
            out.append(process_head(kj, cur[h], h, tile_max[h], *state[h]))
        return tuple(next_max), tuple(out)

    def two_stages(i, carry):
        tile_max, state = carry
        tile_max, state = stage(2 * i + 1, sa_ref, sb_ref, tile_max, state)
        return stage(2 * i + 2, sb_ref, sa_ref, tile_max, state)

    def last_tile(buf, tile_max, state):
        out = []
        for h in range(n_heads):
            mc_ref[h, 0:1, :] = scores_head(jnp.minimum(qi + 1, nq - 1), 0, sc_ref, h)
            out.append(diag_process_head(buf, h, tile_max[h], *state[h]))
        return tuple(out)

    def first_stage(state, diag_next=False):
        tile_max = tuple(mc_ref[h, 0:1, :] for h in range(n_heads))
        return stage(0, sc_ref, sa_ref, tile_max, state, diag_next)

    def fresh():
        return tuple((jnp.full((1, t), -jnp.inf, F32), jnp.zeros((LANES, t), F32)) for _ in range(n_heads))

    def accs_of(state):
        return tuple(acc for _, acc in state)

    def case_first(prev):
        tile_max = tuple(diag_scores_head(sa_ref, h) for h in range(n_heads))
        return accs_of(last_tile(sa_ref, tile_max, fresh()))

    def case_second(prev):
        _attn_write_out(qi - 1, prev, sg_ref, o_ref)
        tile_max, state = first_stage(fresh(), diag_next=True)
        return accs_of(last_tile(sa_ref, tile_max, state))

    def case_even(prev):
        _attn_write_out(qi - 1, prev, sg_ref, o_ref)
        tile_max, state = first_stage(fresh())
        tile_max, state = lax.fori_loop(0, (qi - 2) // 2, two_stages, (tile_max, state))
        tile_max, state = stage(qi - 1, sa_ref, sb_ref, tile_max, state, diag_next=True)
        return accs_of(last_tile(sb_ref, tile_max, state))

    def case_odd(prev):
        _attn_write_out(qi - 1, prev, sg_ref, o_ref)
        tile_max, state = first_stage(fresh())
        tile_max, state = lax.fori_loop(0, (qi - 3) // 2, two_stages, (tile_max, state))
        tile_max, state = stage(qi - 2, sa_ref, sb_ref, tile_max, state)
        tile_max, state = stage(qi - 1, sb_ref, sa_ref, tile_max, state, diag_next=True)
        return accs_of(last_tile(sa_ref, tile_max, state))

    def case_later(prev):
        return lax.cond(qi % 2 == 0, case_even, case_odd, prev)

    def case_not_first(prev):
        return lax.cond(qi == 1, case_second, case_later, prev)

    return lax.cond(qi == 0, case_first, case_not_first, prev)


def _causal_attn_kernel(qt_ref, k_ref, vt_ref, sg_ref, o_ref, sa_ref, sb_ref, sc_ref, mc_ref):
    n_heads = qt_ref.shape[0] // LANES
    nq = qt_ref.shape[1] // ATT_T

    def body(qi, prev):
        return _attn_query_tile(qi, prev, qt_ref, k_ref, vt_ref, sg_ref, o_ref, sa_ref, sb_ref, sc_ref, mc_ref)

    none_yet = tuple(jnp.zeros((LANES, ATT_T), F32) for _ in range(n_heads))
    last = lax.fori_loop(0, nq, body, none_yet)
    _attn_write_out(nq - 1, last, sg_ref, o_ref)


def _causal_attn(qt, k, vt, sg):
    b, s, hw = k.shape
    t = ATT_T
    gw = ATT_HG * LANES
    ow = ATT_HG * MLA_V
    score_buf = pltpu.VMEM((ATT_HG, t, t), F32)
    return pl.pallas_call(
        _causal_attn_kernel,
        grid=(b, hw // gw),
        in_specs=[pl.BlockSpec((None, gw, s), lambda bi, g: (bi, g, 0)),
                  pl.BlockSpec((None, s, gw), lambda bi, g: (bi, 0, g)),
                  pl.BlockSpec((None, gw, s), lambda bi, g: (bi, g, 0)),
                  pl.BlockSpec((None, s, ow), lambda bi, g: (bi, 0, g))],
        out_specs=pl.BlockSpec((None, s, ow), lambda bi, g: (bi, 0, g)),
        out_shape=jax.ShapeDtypeStruct((b, s, (hw // LANES) * MLA_V), BF16),
        scratch_shapes=[score_buf, score_buf, score_buf, pltpu.VMEM((ATT_HG, 8, t), F32)],
        compiler_params=_params(2),
        name="causal_attn",
    )(qt, k, vt, sg)


def _out_kernel(o_ref, x_ref, w_ref, g_ref, b_ref, y_ref):
    y_ref[...] = _residual_norm(o_ref, x_ref, w_ref, g_ref, b_ref)


def _out_proj(o, x, w_out, g, bias, layer):
    b, s, d = x.shape
    tm = PROJ_TM
    tok = lambda bi, si: (bi, si, 0)
    return pl.pallas_call(
        _out_kernel,
        grid=(b, s // tm),
        in_specs=[pl.BlockSpec((None, tm, d), tok), pl.BlockSpec((None, tm, d), tok),
                  _layer_spec(w_out, layer), _layer_spec(g, layer), _layer_spec(bias, layer)],
        out_specs=pl.BlockSpec((None, tm, d), tok),
        out_shape=jax.ShapeDtypeStruct((b, s, d), F32),
        compiler_params=_params(2),
        name="out_proj_ln",
    )(o, x, w_out, g, bias)


def _odd_proj_kernel(*refs, fused):
    x, refs = _layer_input(refs, fused)
    (wqt_ref, wkv_ref, wg_ref, c_ref, s_ref, ct_ref, st_ref,
     qt_ref, k_ref, vt_ref, sg_ref) = refs[:11]
    if fused:
        refs[11][...] = x
    d = SWA_HEADS * SWA_DIM
    half = SWA_DIM // 2
    xb = x.astype(BF16)

    kv = _dot(xb, wkv_ref[...])
    vt_ref[...] = kv[:, LANES:].T.astype(BF16)
    kk = kv[:, :LANES]
    lane = lax.broadcasted_iota(jnp.int32, (1, LANES), 1)
    first_half = (lane % SWA_DIM) < half
    k_sw = jnp.where(first_half, pltpu.roll(kk, LANES - half, 1), pltpu.roll(kk, half, 1))
    k_ref[...] = (kk * c_ref[...] + k_sw * s_ref[...]).astype(BF16)

    qq = _dot_nt(wqt_ref[...], xb)
    ctt = ct_ref[...]
    stt = st_ref[...]
    scale = SWA_DIM ** -0.5 * LOG2E
    for p in range(d // LANES):
        blk = qq[p * LANES:(p + 1) * LANES, :]
        q_sw = jnp.concatenate([blk[half:2 * half], blk[:half], blk[3 * half:], blk[2 * half:3 * half]], axis=0)
        qt_ref[p * LANES:(p + 1) * LANES, :] = ((blk * ctt + q_sw * stt) * scale).astype(BF16)

    sg_ref[...] = _silu(_dot(xb, wg_ref[...])).astype(BF16)


def _layer_spec(a, layer):
    return pl.BlockSpec((None,) + a.shape[1:], lambda bi, si: (layer,) + (0,) * (a.ndim - 1))


def _odd_proj(x, w, tabs, layer, prev=None):
    b, s, d = x.shape
    tm = PROJ_TM
    tok = lambda bi, si: (bi, si, 0)
    tok_t = lambda bi, si: (bi, 0, si)
    tab = pl.BlockSpec((tm, LANES), lambda bi, si: (si, 0))
    tab_t = pl.BlockSpec((LANES, tm), lambda bi, si: (0, si))
    names = ("wqt", "wkv", "wg")
    operands, in_specs, x_spec, x_shape = _input_specs(x, prev)
    return pl.pallas_call(
        functools.partial(_odd_proj_kernel, fused=prev is not None),
        grid=(b, s // tm),
        in_specs=in_specs + [_layer_spec(w[k], layer) for k in names] + [tab, tab, tab_t, tab_t],
        out_specs=[pl.BlockSpec((None, d, tm), tok_t), pl.BlockSpec((None, tm, LANES), tok),
                   pl.BlockSpec((None, LANES, tm), tok_t), pl.BlockSpec((None, tm, d), tok)] + x_spec,
        out_shape=[jax.ShapeDtypeStruct((b, d, s), BF16), jax.ShapeDtypeStruct((b, s, LANES), BF16),
                   jax.ShapeDtypeStruct((b, LANES, s), BF16), jax.ShapeDtypeStruct((b, s, d), BF16)] + x_shape,
        compiler_params=_params(2),
        name="odd_proj",
    )(*operands, *[w[k] for k in names], tabs["c"], tabs["s"], tabs["ct"], tabs["st"])


def _swa_kernel(sink_ref, qt_ref, k_ref, vt_ref, sg_ref, o_ref):
    n_blk = qt_ref.shape[1] // QBLK
    n_pair = qt_ref.shape[0] // LANES
    win = 2 * QBLK
    qi = pl.program_id(1)
    row = lax.broadcasted_iota(jnp.int32, (LANES, QBLK), 0)
    lo_rows = row < SWA_DIM
    rel = (lax.broadcasted_iota(jnp.int32, (win, QBLK), 1) - lax.broadcasted_iota(jnp.int32, (win, QBLK), 0))
    sink_row = jnp.concatenate([jnp.full((1, QBLK), sink_ref[c] * LOG2E, F32) for c in range(2 * n_pair)],
                               axis=1)
    ones_blk = jnp.ones((16, win), BF16)

    def window_start(blk):
        return pl.multiple_of(jnp.maximum(qi * n_blk + blk - 1, 0) * QBLK, QBLK)

    def scores(blk):
        q0 = blk * QBLK
        cols = []
        for p in range(n_pair):
            qp = qt_ref[p * LANES:(p + 1) * LANES, q0:q0 + QBLK]
            zero = jnp.zeros_like(qp)
            cols.append(jnp.where(lo_rows, qp, zero))
            cols.append(jnp.where(lo_rows, zero, qp))
        s = _dot(k_ref[pl.ds(window_start(blk), win), :], jnp.concatenate(cols, axis=1))
        shift = jnp.where(qi * n_blk + blk == 0, 0, QBLK)
        diff = rel + shift
        band = (diff >= 0) & (diff < WINDOW)
        return jnp.concatenate([jnp.where(band, s[:, c * QBLK:(c + 1) * QBLK], -jnp.inf)
                                for c in range(2 * n_pair)], axis=1)

    def process(blk, s):
        q0 = blk * QBLK
        m = jnp.maximum(jnp.max(s, axis=0, keepdims=True), sink_row)
        p = jnp.exp2(s - m).astype(BF16)
        vt = jnp.concatenate([vt_ref[:, pl.ds(window_start(blk), win)], ones_blk], axis=0)
        ot = _dot(vt, p)
        inv = 1.0 / (ot[LANES:LANES + 1, :] + jnp.exp2(sink_row - m))
        for pr in range(n_pair):
            ca = slice((2 * pr) * QBLK, (2 * pr + 1) * QBLK)
            cb = slice((2 * pr + 1) * QBLK, (2 * pr + 2) * QBLK)
            pair = jnp.concatenate([ot[:SWA_DIM, ca] * inv[:, ca], ot[SWA_DIM:LANES, cb] * inv[:, cb]],
                                   axis=0).T
            cols = slice(pr * LANES, (pr + 1) * LANES)
            o_ref[q0:q0 + QBLK, cols] = (pair * sg_ref[q0:q0 + QBLK, cols].astype(F32)).astype(BF16)

    s_next = scores(0)
    for blk in range(n_blk):
        s_cur = s_next
        if blk + 1 < n_blk:
            s_next = scores(blk + 1)
        process(blk, s_cur)


def _swa_attn(sinks, qt, k, vt, sg):
    b, d, s = qt.shape
    tq = SWA_TQ
    tok = lambda bi, qi: (bi, qi, 0)
    return pl.pallas_call(
        _swa_kernel,
        grid=(b, s // tq),
        in_specs=[pl.BlockSpec(memory_space=pltpu.SMEM),
                  pl.BlockSpec((None, d, tq), lambda bi, qi: (bi, 0, qi)),
                  pl.BlockSpec((None, s, LANES), lambda bi, qi: (bi, 0, 0)),
                  pl.BlockSpec((None, LANES, s), lambda bi, qi: (bi, 0, 0)),
                  pl.BlockSpec((None, tq, d), tok)],
        out_specs=pl.BlockSpec((None, tq, d), tok),
        out_shape=jax.ShapeDtypeStruct((b, s, d), BF16),
        compiler_params=_params(2),
        name="swa_attn",
    )(sinks, qt, k, vt, sg)


def _pad_heads(w, n_heads, dim, offsets):
    k = w.shape[0]
    n_off = len(offsets)
    w4 = w.reshape(k, n_heads // n_off, n_off, dim)
    parts = [jnp.pad(w4[:, :, i, :], ((0, 0), (0, 0), (off, LANES - off - dim)))
             for i, off in enumerate(offsets)]
    return jnp.stack(parts, axis=2).reshape(k, n_heads * LANES)


def _swap_halves(w, n_heads, dim):
    k = w.shape[0]
    w3 = w.reshape(k, n_heads, dim)
    return jnp.concatenate([w3[..., dim // 2:], w3[..., :dim // 2]], axis=-1).reshape(k, n_heads * dim)


def _even_weights(w_in, q_norm, w_uq, kv_norm, w_ukv, b_f):
    sizes = (MLA_Q_RANK, MLA_KV_RANK, MLA_ROPE, FOX_HEADS * FOX_DIM, FOX_HEADS * FOX_DIM,
             FOX_HEADS * FOX_DIM, FOX_HEADS, MLA_HEADS * MLA_V + FOX_HEADS * FOX_DIM)
    cuts = [int(c) for c in np.cumsum(sizes)[:-1]]
    w_cq, w_ckv, w_kpe, w_fq, w_fk, w_fv, w_f, w_g = jnp.split(w_in, cuts, axis=1)
    place = lambda w: _pad_heads(w, 1, MLA_ROPE, (MLA_NOPE,))
    wlat = jnp.concatenate([w_cq, w_ckv, place(w_kpe), _pad_heads(w_f, 1, FOX_HEADS, (0,))], axis=1)
    v_off = (0, MLA_V)
    ukv3 = w_ukv.reshape(MLA_KV_RANK, MLA_HEADS, MLA_NOPE + MLA_V)
    uk = ukv3[..., :MLA_NOPE].reshape(MLA_KV_RANK, MLA_HEADS * MLA_NOPE)
    uv = ukv3[..., MLA_NOPE:].reshape(MLA_KV_RANK, MLA_HEADS * MLA_V)
    bf = lambda a: a.astype(BF16)
    return dict(wlat=bf(wlat),
                wuqt=bf(_pad_heads(w_uq, MLA_HEADS, MLA_NOPE + MLA_ROPE, (0,)).T),
                wuk=bf(_pad_heads(uk, MLA_HEADS, MLA_NOPE, (0,))),
                wuvt=bf(_pad_heads(uv, MLA_HEADS, MLA_V, v_off).T),
                wfqt=bf(w_fq.T), wfk=bf(w_fk), wfvt=bf(w_fv.T),
                wg=bf(w_g), qn=q_norm.reshape(1, -1), kvn=kv_norm.reshape(1, -1),
                bf=jnp.pad(b_f, (0, LANES - FOX_HEADS)).reshape(1, LANES))


def _pair_heads(w):
    lead = w.shape[:-1]
    w4 = w.reshape(*lead, SWA_KV_HEADS, SWA_HEADS // SWA_KV_HEADS, -1)
    return jnp.swapaxes(w4, -3, -2).reshape(*lead, -1)


def _odd_weights(w_in, sinks, w_out):
    d = SWA_HEADS * SWA_DIM
    dkv = SWA_KV_HEADS * SWA_DIM
    w_q, w_k, w_v, w_g = jnp.split(w_in, [d, d + dkv, d + 2 * dkv], axis=1)
    pair_rows = lambda a: jnp.swapaxes(
        a.reshape(SWA_KV_HEADS, SWA_HEADS // SWA_KV_HEADS, SWA_DIM, -1), 0, 1).reshape(a.shape)
    return dict(wqt=pair_rows(w_q.T).astype(BF16), wkv=jnp.concatenate([w_k, w_v], axis=1).astype(BF16),
                wg=_pair_heads(w_g).astype(BF16),
                sinks=_pair_heads(sinks.reshape(-1, 1).T).reshape(-1),
                w_out=pair_rows(w_out).astype(BF16))


def _rope_tables(s):
    pos = np.arange(s, dtype=np.float64)

    def cs(dim):
        inv = ROPE_THETA ** (-np.arange(0, dim, 2, dtype=np.float64) / dim)
        ang = pos[:, None] * inv[None, :]
        cos, sin = np.cos(ang), np.sin(ang)
        return np.concatenate([cos, cos], axis=1), np.concatenate([-sin, sin], axis=1)

    c32, s32 = cs(MLA_ROPE)
    z = lambda n: np.zeros((s, n))
    pad = LANES - MLA_NOPE - MLA_ROPE
    scale = (MLA_NOPE + MLA_ROPE) ** -0.5 * LOG2E
    even = dict(cqt=(np.concatenate([np.ones((s, MLA_NOPE)), c32, z(pad)], axis=1) * scale).T,
                sqt=(np.concatenate([z(MLA_NOPE), s32, z(pad)], axis=1) * scale).T,
                ck=np.concatenate([z(MLA_NOPE), c32, z(pad)], axis=1),
                sk=np.concatenate([z(MLA_NOPE), s32, z(pad)], axis=1))
    c64, s64 = cs(SWA_DIM)
    odd = dict(c=np.concatenate([c64, c64], axis=1), s=np.concatenate([s64, s64], axis=1))
    odd.update(ct=odd["c"].T, st=odd["s"].T)
    as_f32 = lambda d: {k: jnp.asarray(np.ascontiguousarray(v), dtype=F32) for k, v in d.items()}
    return as_f32(even), as_f32(odd)


def kernel(x, even_w_in, even_q_norm, even_w_uq, even_kv_norm, even_w_ukv, even_b_f, even_w_out,
           even_ln_g, even_ln_b, odd_w_in, odd_sinks, odd_w_out, odd_ln_g, odd_ln_b):
    s = x.shape[1]
    even_tabs, odd_tabs = _rope_tables(s)
    bf = lambda a: a.astype(BF16)
    we = jax.vmap(_even_weights)(bf(even_w_in), even_q_norm, bf(even_w_uq), even_kv_norm, bf(even_w_ukv),
                                 even_b_f)
    wo = jax.vmap(_odd_weights)(bf(odd_w_in), odd_sinks, bf(odd_w_out))
    even_w_out = even_w_out.astype(BF16)
    row = lambda a: a[:, None, :]
    closing = {0: (even_w_out, row(even_ln_g), row(even_ln_b)),
               1: (wo["w_out"], row(odd_ln_g), row(odd_ln_b))}
    prev = None
    for layer in range(DEPTH):
        j = layer // 2
        if layer % 2 == 0:
            qt, k, vt, sg, *x_new = _even_proj(x, we, even_tabs, j, prev)
            x = x_new[0] if x_new else x
            o = _causal_attn(qt, k, vt, sg)
        else:
            qt, k, vt, sg, *x_new = _odd_proj(x, wo, odd_tabs, j, prev)
            x = x_new[0] if x_new else x
            o = _swa_attn(wo["sinks"][j], qt, k, vt, sg)
        prev = (o, *closing[layer % 2], j)
    o, w_out, g, bias, j = prev
    return _out_proj(o, x, w_out, g, bias, j)
```

```python
import functools
import math

import jax
import jax.numpy as jnp
import numpy as np
from jax import lax
from jax.experimental import pallas as pl
from jax.experimental.pallas import tpu as pltpu

D_MODEL = 1024
DEPTH = 4
ROPE_THETA = 10000.0
QBLK = 128
MLA_HEADS = 8
MLA_NOPE = 64
MLA_ROPE = 32
MLA_V = 64
MLA_Q_RANK = 256
MLA_KV_RANK = 128
FOX_HEADS = 8
FOX_DIM = 64
SWA_HEADS = 16
SWA_KV_HEADS = 2
SWA_DIM = 64
WINDOW = 128
RMS_EPS = 1e-6
LN_EPS = 1e-5
ALPHA = (2 * DEPTH) ** 0.25
LOG2E = math.log2(math.e)

LANES = 128
VMEM_LIMIT = 56 * 1024 * 1024
BF16 = jnp.bfloat16
F32 = jnp.float32

PROJ_TM = 512
OUT_TM = 1024
ATT_T = 512
ATT_HG = 4
SWA_TQ = 1024
EVEN_HEADS = MLA_HEADS + FOX_HEADS
EVEN_HW = EVEN_HEADS * LANES
MLA_HW = MLA_HEADS * LANES


def _dot(a, b):
    return jnp.dot(a, b, preferred_element_type=F32)


def _dot_nt(a, b):
    return lax.dot_general(a, b, (((1,), (1,)), ((), ())), preferred_element_type=F32)


def _rms(t, g):
    return t * lax.rsqrt(jnp.mean(t * t, axis=-1, keepdims=True) + RMS_EPS) * g


def _silu(g):
    return g / (1.0 + jnp.exp(-g))


def _pack3(v):
    hi = v.astype(BF16).astype(F32)
    r1 = v - hi
    mid = r1.astype(BF16).astype(F32)
    lo = r1 - mid
    return (hi + pltpu.roll(mid, FOX_HEADS, 1) + pltpu.roll(lo, 2 * FOX_HEADS, 1)).astype(BF16)


def _params(n_axes, flags=None):
    return pltpu.CompilerParams(dimension_semantics=("arbitrary",) * n_axes,
                                vmem_limit_bytes=VMEM_LIMIT, flags=flags)


def _residual_norm(o_ref, x_ref, w_ref, g_ref, b_ref):
    z = ALPHA * x_ref[...] + _dot(o_ref[...], w_ref[...])
    mu = jnp.mean(z, axis=-1, keepdims=True)
    zc = z - mu
    var = jnp.mean(zc * zc, axis=-1, keepdims=True)
    return zc * lax.rsqrt(var + LN_EPS) * g_ref[...] + b_ref[...]


def _layer_input(refs, fused):
    if not fused:
        return refs[0][...], refs[1:]
    x = _residual_norm(*refs[:5])
    return x, refs[5:]


def _even_proj_kernel(*refs, fused):
    x, refs = _layer_input(refs, fused)
    (wlat_ref, wuqt_ref, wuk_ref, wuvt_ref, wfqt_ref, wfk_ref, wfvt_ref, wg_ref, qn_ref, kvn_ref,
     bf_ref, cqt_ref, sqt_ref, ck_ref, sk_ref, qt_ref, k_ref, vt_ref, sg_ref) = refs[:19]
    carry_ref = refs[-1]
    if fused:
        refs[19][...] = x
    tm = x.shape[0]
    xb = x.astype(BF16)

    hrow = lax.broadcasted_iota(jnp.int32, (MLA_HW, 1), 0) % (2 * LANES)
    ones_col = jnp.where((hrow == MLA_V) | (hrow == LANES), 1.0, 0.0).astype(F32)
    lat = _dot(xb, wlat_ref[...])
    o = 0
    cq = lat[:, o:o + MLA_Q_RANK]
    o += MLA_Q_RANK
    ckv = lat[:, o:o + MLA_KV_RANK]
    o += MLA_KV_RANK
    kp = lat[:, o:o + LANES]
    hr = MLA_ROPE // 2
    lane_p = lax.broadcasted_iota(jnp.int32, (1, LANES), 1)
    kp_sw = jnp.where(lane_p < MLA_NOPE + hr, pltpu.roll(kp, LANES - hr, 1), pltpu.roll(kp, hr, 1))
    kpe = kp * ck_ref[...] + kp_sw * sk_ref[...]
    o += LANES
    fl = lat[:, o:o + LANES] + bf_ref[...]

    rq = _rms(cq, qn_ref[...]).astype(BF16)
    qa = _dot_nt(wuqt_ref[...], rq)
    cqt = cqt_ref[...]
    sqt = sqt_ref[...]
    pe = MLA_NOPE
    for h in range(MLA_HEADS):
        hs = slice(h * LANES, (h + 1) * LANES)
        blk = qa[hs, :]
        q_sw = jnp.concatenate([blk[:pe], blk[pe + hr:pe + 2 * hr], blk[pe:pe + hr], blk[pe + 2 * hr:]], axis=0)
        qt_ref[hs, :] = (blk * cqt + q_sw * sqt).astype(BF16)
    rkv = _rms(ckv, kvn_ref[...]).astype(BF16)
    kk = _dot(rkv, wuk_ref[...])
    for h in range(MLA_HEADS):
        hs = slice(h * LANES, (h + 1) * LANES)
        k_ref[:, hs] = (kk[:, hs] + kpe).astype(BF16)
    vt_ref[:MLA_HW, :] = (_dot_nt(wuvt_ref[...], rkv) + ones_col).astype(BF16)

    lane = lax.broadcasted_iota(jnp.int32, (1, LANES), 1)
    lf = (jnp.minimum(fl, 0.0) - jnp.log1p(jnp.exp(-jnp.abs(fl)))) * LOG2E
    lf = jnp.where(lane < FOX_HEADS, lf, 0.0)
    row = lax.broadcasted_iota(jnp.int32, (tm, tm), 0)
    col = lax.broadcasted_iota(jnp.int32, (tm, tm), 1)
    tri = jnp.where(col <= row, 1.0, 0.0).astype(BF16)
    sums = _dot(tri, _pack3(lf))
    c_loc = sums + pltpu.roll(sums, LANES - FOX_HEADS, 1) + pltpu.roll(sums, LANES - 2 * FOX_HEADS, 1)
    c_loc = jnp.where(lane < FOX_HEADS, c_loc, 0.0)

    @pl.when(pl.program_id(1) == 0)
    def _():
        carry_ref[...] = jnp.zeros_like(carry_ref)

    c = c_loc + carry_ref[0:1, :]
    carry_ref[...] = jnp.broadcast_to(c[tm - 1:tm, :], carry_ref.shape)
    src = lax.broadcasted_iota(jnp.int32, (LANES, MLA_HW), 0)
    dst = lax.broadcasted_iota(jnp.int32, (LANES, MLA_HW), 1)
    src_h = src % FOX_HEADS
    c_lane = src_h * LANES + jnp.where(src_h % 2 == 0, FOX_DIM, 0) + src // FOX_HEADS
    place = jnp.where((dst == c_lane) & (src < 3 * FOX_HEADS), 1.0, 0.0).astype(BF16)
    k_extra = _dot(_pack3(c), place)

    fqt = _dot_nt(wfqt_ref[...], xb) * (FOX_DIM ** -0.5 * LOG2E)
    fk = _dot(xb, wfk_ref[...])
    fvt = _dot_nt(wfvt_ref[...], xb)
    half_row = lax.broadcasted_iota(jnp.int32, (FOX_DIM, tm), 0)
    neg_blk = jnp.where(half_row < 3, -1.0, 0.0).astype(BF16)
    ones_blk = jnp.where(half_row < 1, 1.0, 0.0).astype(BF16)
    lane_k = lax.broadcasted_iota(jnp.int32, (1, LANES), 1)
    for h in range(FOX_HEADS):
        off = (h % 2) * FOX_DIM
        base = MLA_HW + h * LANES
        data = slice(base + off, base + off + FOX_DIM)
        rest = slice(base + FOX_DIM - off, base + 2 * FOX_DIM - off)
        rows = slice(h * FOX_DIM, (h + 1) * FOX_DIM)
        qt_ref[data, :] = fqt[rows, :].astype(BF16)
        qt_ref[rest, :] = neg_blk
        vt_ref[data, :] = fvt[rows, :].astype(BF16)
        vt_ref[rest, :] = ones_blk
        pair = slice((h // 2) * LANES, (h // 2 + 1) * LANES)
        in_data = (lane_k >= off) & (lane_k < off + FOX_DIM)
        k_ref[:, base:base + LANES] = jnp.where(
            in_data, fk[:, pair], k_extra[:, h * LANES:(h + 1) * LANES]).astype(BF16)

    sg_ref[...] = _silu(_dot(xb, wg_ref[...])).astype(BF16)


def _input_specs(x, prev):
    b, s, d = x.shape
    tok = pl.BlockSpec((None, PROJ_TM, d), lambda bi, si: (bi, si, 0))
    if prev is None:
        return [x], [tok], [], []
    o, w_out, g, bias, layer = prev
    specs = [tok, tok] + [_layer_spec(a, layer) for a in (w_out, g, bias)]
    return [o, x, w_out, g, bias], specs, [tok], [jax.ShapeDtypeStruct((b, s, d), F32)]


def _even_proj(x, w, tabs, layer, prev=None):
    b, s, d = x.shape
    tm = PROJ_TM
    tok = lambda bi, si: (bi, si, 0)
    tok_t = lambda bi, si: (bi, 0, si)
    names = ("wlat", "wuqt", "wuk", "wuvt", "wfqt", "wfk", "wfvt", "wg", "qn", "kvn", "bf")
    operands, in_specs, x_spec, x_shape = _input_specs(x, prev)
    in_specs += [_layer_spec(w[k], layer) for k in names]
    in_specs += [pl.BlockSpec((LANES, tm), lambda bi, si: (0, si))] * 2
    in_specs += [pl.BlockSpec((tm, LANES), lambda bi, si: (si, 0))] * 2
    out_specs = [pl.BlockSpec((None, EVEN_HW, tm), tok_t), pl.BlockSpec((None, tm, EVEN_HW), tok),
                 pl.BlockSpec((None, EVEN_HW, tm), tok_t), pl.BlockSpec((None, tm, d), tok)]
    out_shape = [jax.ShapeDtypeStruct((b, EVEN_HW, s), BF16), jax.ShapeDtypeStruct((b, s, EVEN_HW), BF16),
                 jax.ShapeDtypeStruct((b, EVEN_HW, s), BF16), jax.ShapeDtypeStruct((b, s, d), BF16)]
    return pl.pallas_call(
        functools.partial(_even_proj_kernel, fused=prev is not None),
        grid=(b, s // tm),
        in_specs=in_specs,
        out_specs=out_specs + x_spec,
        out_shape=out_shape + x_shape,
        scratch_shapes=[pltpu.VMEM((8, LANES), F32)],
        compiler_params=_params(2),
        name="even_proj",
    )(*operands, *[w[k] for k in names], tabs["cqt"], tabs["sqt"], tabs["ck"], tabs["sk"])


def _attn_write_out(tile, accs, sg_ref, o_ref):
    t = ATT_T
    rows = pl.ds(pl.multiple_of(tile * t, t), t)
    row = lax.broadcasted_iota(jnp.int32, (LANES, t), 0)
    for hp in range(len(accs) // 2):
        a0 = accs[2 * hp]
        a1 = accs[2 * hp + 1]
        pair_t = jnp.where(row < MLA_V, a0 / a0[MLA_V:MLA_V + 1, :], a1 / a1[0:1, :])
        cols = slice(hp * LANES, (hp + 1) * LANES)
        o_ref[rows, cols] = (pair_t.T * sg_ref[rows, cols].astype(F32)).astype(BF16)


def _attn_query_tile(qi, prev, qt_ref, k_ref, vt_ref, sg_ref, o_ref, sa_ref, sb_ref, sc_ref, mc_ref):
    t = ATT_T
    hf = t // 2
    n_heads = qt_ref.shape[0] // LANES
    nq = qt_ref.shape[1] // t
    heads = [slice(h * LANES, (h + 1) * LANES) for h in range(n_heads)]

    def q_slice(tile, start=0, size=None):
        size = t if size is None else size
        return pl.ds(pl.multiple_of(tile * t + start, size), size)

    def key_slice(kj, start=0, size=None):
        size = t if size is None else size
        return pl.ds(pl.multiple_of(kj * t + start, size), size)

    def col_max(s):
        return jnp.max(s, axis=0, keepdims=True)

    def causal_mask(s):
        n = s.shape[0]
        keep = (lax.broadcasted_iota(jnp.int32, (n, n), 0) <= lax.broadcasted_iota(jnp.int32, (n, n), 1))
        return jnp.where(keep, s, -jnp.inf)

    def scores_head(q_tile, kj, buf, h):
        s = _dot(k_ref[key_slice(kj), heads[h]], qt_ref[heads[h], q_slice(q_tile)])
        buf[h] = s
        return col_max(s)

    def diag_scores_head(buf, h):
        early = _dot(k_ref[key_slice(qi, 0, hf), heads[h]], qt_ref[heads[h], q_slice(qi)])
        late = _dot(k_ref[key_slice(qi, hf, hf), heads[h]], qt_ref[heads[h], q_slice(qi, hf, hf)])
        early = jnp.concatenate([causal_mask(early[:, :hf]), early[:, hf:]], axis=1)
        late = causal_mask(late)
        buf[h, :hf, :] = early
        buf[h, hf:, hf:] = late
        return jnp.concatenate([col_max(early[:, :hf]),
                                jnp.maximum(col_max(early[:, hf:]), col_max(late))], axis=1)

    def process_head(kj, s, h, tile_max, m, acc):
        m_new = jnp.maximum(m, tile_max)
        p = jnp.exp2(s - m_new).astype(BF16)
        return m_new, jnp.exp2(m - m_new) * acc + _dot(vt_ref[heads[h], key_slice(kj)], p)

    def diag_process_head(buf, h, tile_max, m, acc):
        m_new = jnp.maximum(m, tile_max)
        p_early = jnp.exp2(buf[h, :hf, :] - m_new).astype(BF16)
        p_late = jnp.exp2(buf[h, hf:, hf:] - m_new[:, hf:]).astype(BF16)
        pv_early = _dot(vt_ref[heads[h], key_slice(qi, 0, hf)], p_early)
        pv_late = _dot(vt_ref[heads[h], key_slice(qi, hf, hf)], p_late)
        scaled = jnp.exp2(m - m_new) * acc + pv_early
        return m_new, jnp.concatenate([scaled[:, :hf], scaled[:, hf:] + pv_late], axis=1)

    def stage(kj, cur, nxt, tile_max, state, diag_next=False):
        next_max, out = [], []
        for h in range(n_heads):
            next_max.append(diag_scores_head(nxt, h) if diag_next else scores_head(qi, kj + 1, nxt, h))
            out.append(process_head(kj, cur[h], h, tile_max[h], *state[h]))
        return tuple(next_max), tuple(out)

    def two_stages(i, carry):
        tile_max, state = carry
        tile_max, state = stage(2 * i + 1, sa_ref, sb_ref, tile_max, state)
        return stage(2 * i + 2, sb_ref, sa_ref, tile_max, state)

    def last_tile(buf, tile_max, state):
        out = []
        for h in range(n_heads):
            mc_ref[h, 0:1, :] = scores_head(jnp.minimum(qi + 1, nq - 1), 0, sc_ref, h)
            out.append(diag_process_head(buf, h, tile_max[h], *state[h]))
        return tuple(out)

    def first_stage(state, diag_next=False):
        tile_max = tuple(mc_ref[h, 0:1, :] for h in range(n_heads))
        return stage(0, sc_ref, sa_ref, tile_max, state, diag_next)

    def fresh():
        return tuple((jnp.full((1, t), -jnp.inf, F32), jnp.zeros((LANES, t), F32)) for _ in range(n_heads))

    def accs_of(state):
        return tuple(acc for _, acc in state)

    def case_first(prev):
        tile_max = tuple(diag_scores_head(sa_ref, h) for h in range(n_heads))
        return accs_of(last_tile(sa_ref, tile_max, fresh()))

    def case_second(prev):
        _attn_write_out(qi - 1, prev, sg_ref, o_ref)
        tile_max, state = first_stage(fresh(), diag_next=True)
        return accs_of(last_tile(sa_ref, tile_max, state))

    def case_even(prev):
        _attn_write_out(qi - 1, prev, sg_ref, o_ref)
        tile_max, state = first_stage(fresh())
        tile_max, state = lax.fori_loop(0, (qi - 2) // 2, two_stages, (tile_max, state))
        tile_max, state = stage(qi - 1, sa_ref, sb_ref, tile_max, state, diag_next=True)
        return accs_of(last_tile(sb_ref, tile_max, state))

    def case_odd(prev):
        _attn_write_out(qi - 1, prev, sg_ref, o_ref)
        tile_max, state = first_stage(fresh())
        tile_max, state = lax.fori_loop(0, (qi - 3) // 2, two_stages, (tile_max, state))
        tile_max, state = stage(qi - 2, sa_ref, sb_ref, tile_max, state)
        tile_max, state = stage(qi - 1, sb_ref, sa_ref, tile_max, state, diag_next=True)
        return accs_of(last_tile(sa_ref, tile_max, state))

    def case_later(prev):
        return lax.cond(qi % 2 == 0, case_even, case_odd, prev)

    def case_not_first(prev):
        return lax.cond(qi == 1, case_second, case_later, prev)

    return lax.cond(qi == 0, case_first, case_not_first, prev)


def _causal_attn_kernel(qt_ref, k_ref, vt_ref, sg_ref, o_ref, sa_ref, sb_ref, sc_ref, mc_ref):
    n_heads = qt_ref.shape[0] // LANES
    nq = qt_ref.shape[1] // ATT_T

    def body(qi, prev):
        return _attn_query_tile(qi, prev, qt_ref, k_ref, vt_ref, sg_ref, o_ref, sa_ref, sb_ref, sc_ref, mc_ref)

    none_yet = tuple(jnp.zeros((LANES, ATT_T), F32) for _ in range(n_heads))
    last = lax.fori_loop(0, nq, body, none_yet)
    _attn_write_out(nq - 1, last, sg_ref, o_ref)


def _causal_attn(qt, k, vt, sg):
    b, s, hw = k.shape
    t = ATT_T
    gw = ATT_HG * LANES
    ow = ATT_HG * MLA_V
    score_buf = pltpu.VMEM((ATT_HG, t, t), F32)
    return pl.pallas_call(
        _causal_attn_kernel,
        grid=(b, hw // gw),
        in_specs=[pl.BlockSpec((None, gw, s), lambda bi, g: (bi, g, 0)),
                  pl.BlockSpec((None, s, gw), lambda bi, g: (bi, 0, g)),
                  pl.BlockSpec((None, gw, s), lambda bi, g: (bi, g, 0)),
                  pl.BlockSpec((None, s, ow), lambda bi, g: (bi, 0, g))],
        out_specs=pl.BlockSpec((None, s, ow), lambda bi, g: (bi, 0, g)),
        out_shape=jax.ShapeDtypeStruct((b, s, (hw // LANES) * MLA_V), BF16),
        scratch_shapes=[score_buf, score_buf, score_buf, pltpu.VMEM((ATT_HG, 8, t), F32)],
        compiler_params=_params(2),
        name="causal_attn",
    )(qt, k, vt, sg)


def _out_kernel(o_ref, x_ref, w_ref, g_ref, b_ref, y_ref):
    y_ref[...] = _residual_norm(o_ref, x_ref, w_ref, g_ref, b_ref)


def _out_proj(o, x, w_out, g, bias, layer):
    b, s, d = x.shape
    tm = OUT_TM
    tok = lambda bi, si: (bi, si, 0)
    return pl.pallas_call(
        _out_kernel,
        grid=(b, s // tm),
        in_specs=[pl.BlockSpec((None, tm, d), tok), pl.BlockSpec((None, tm, d), tok),
                  _layer_spec(w_out, layer), _layer_spec(g, layer), _layer_spec(bias, layer)],
        out_specs=pl.BlockSpec((None, tm, d), tok),
        out_shape=jax.ShapeDtypeStruct((b, s, d), F32),
        compiler_params=_params(2),
        name="out_proj_ln",
    )(o, x, w_out, g, bias)


def _odd_proj_kernel(*refs, fused):
    x, refs = _layer_input(refs, fused)
    (wqt_ref, wkv_ref, wg_ref, c_ref, s_ref, ct_ref, st_ref,
     qt_ref, k_ref, vt_ref, sg_ref) = refs[:11]
    if fused:
        refs[11][...] = x
    d = SWA_HEADS * SWA_DIM
    half = SWA_DIM // 2
    xb = x.astype(BF16)

    kv = _dot(xb, wkv_ref[...])
    vt_ref[...] = kv[:, LANES:].T.astype(BF16)
    kk = kv[:, :LANES]
    lane = lax.broadcasted_iota(jnp.int32, (1, LANES), 1)
    first_half = (lane % SWA_DIM) < half
    k_sw = jnp.where(first_half, pltpu.roll(kk, LANES - half, 1), pltpu.roll(kk, half, 1))
    k_ref[...] = (kk * c_ref[...] + k_sw * s_ref[...]).astype(BF16)

    qq = _dot_nt(wqt_ref[...], xb)
    ctt = ct_ref[...]
    stt = st_ref[...]
    scale = SWA_DIM ** -0.5 * LOG2E
    for p in range(d // LANES):
        blk = qq[p * LANES:(p + 1) * LANES, :]
        q_sw = jnp.concatenate([blk[half:2 * half], blk[:half], blk[3 * half:], blk[2 * half:3 * half]], axis=0)
        qt_ref[p * LANES:(p + 1) * LANES, :] = ((blk * ctt + q_sw * stt) * scale).astype(BF16)

    sg_ref[...] = _silu(_dot(xb, wg_ref[...])).astype(BF16)


def _layer_spec(a, layer):
    return pl.BlockSpec((None,) + a.shape[1:], lambda bi, si: (layer,) + (0,) * (a.ndim - 1))


def _odd_proj(x, w, tabs, layer, prev=None):
    b, s, d = x.shape
    tm = PROJ_TM
    tok = lambda bi, si: (bi, si, 0)
    tok_t = lambda bi, si: (bi, 0, si)
    tab = pl.BlockSpec((tm, LANES), lambda bi, si: (si, 0))
    tab_t = pl.BlockSpec((LANES, tm), lambda bi, si: (0, si))
    names = ("wqt", "wkv", "wg")
    operands, in_specs, x_spec, x_shape = _input_specs(x, prev)
    return pl.pallas_call(
        functools.partial(_odd_proj_kernel, fused=prev is not None),
        grid=(b, s // tm),
        in_specs=in_specs + [_layer_spec(w[k], layer) for k in names] + [tab, tab, tab_t, tab_t],
        out_specs=[pl.BlockSpec((None, d, tm), tok_t), pl.BlockSpec((None, tm, LANES), tok),
                   pl.BlockSpec((None, LANES, tm), tok_t), pl.BlockSpec((None, tm, d), tok)] + x_spec,
        out_shape=[jax.ShapeDtypeStruct((b, d, s), BF16), jax.ShapeDtypeStruct((b, s, LANES), BF16),
                   jax.ShapeDtypeStruct((b, LANES, s), BF16), jax.ShapeDtypeStruct((b, s, d), BF16)] + x_shape,
        compiler_params=_params(2),
        name="odd_proj",
    )(*operands, *[w[k] for k in names], tabs["c"], tabs["s"], tabs["ct"], tabs["st"])


def _swa_kernel(sink_ref, qt_ref, k_ref, vt_ref, sg_ref, o_ref):
    n_blk = qt_ref.shape[1] // QBLK
    n_pair = qt_ref.shape[0] // LANES
    win = 2 * QBLK
    qi = pl.program_id(1)
    row = lax.broadcasted_iota(jnp.int32, (LANES, QBLK), 0)
    lo_rows = row < SWA_DIM
    rel = (lax.broadcasted_iota(jnp.int32, (win, QBLK), 1) - lax.broadcasted_iota(jnp.int32, (win, QBLK), 0))
    sink_row = jnp.concatenate([jnp.full((1, QBLK), sink_ref[c] * LOG2E, F32) for c in range(2 * n_pair)],
                               axis=1)
    ones_blk = jnp.ones((16, win), BF16)

    def window_start(blk):
        return pl.multiple_of(jnp.maximum(qi * n_blk + blk - 1, 0) * QBLK, QBLK)

    def scores(blk):
        q0 = blk * QBLK
        cols = []
        for p in range(n_pair):
            qp = qt_ref[p * LANES:(p + 1) * LANES, q0:q0 + QBLK]
            zero = jnp.zeros_like(qp)
            cols.append(jnp.where(lo_rows, qp, zero))
            cols.append(jnp.where(lo_rows, zero, qp))
        s = _dot(k_ref[pl.ds(window_start(blk), win), :], jnp.concatenate(cols, axis=1))
        shift = jnp.where(qi * n_blk + blk == 0, 0, QBLK)
        diff = rel + shift
        band = (diff >= 0) & (diff < WINDOW)
        return jnp.concatenate([jnp.where(band, s[:, c * QBLK:(c + 1) * QBLK], -jnp.inf)
                                for c in range(2 * n_pair)], axis=1)

    def process(blk, s):
        q0 = blk * QBLK
        m = jnp.maximum(jnp.max(s, axis=0, keepdims=True), sink_row)
        p = jnp.exp2(s - m).astype(BF16)
        vt = jnp.concatenate([vt_ref[:, pl.ds(window_start(blk), win)], ones_blk], axis=0)
        ot = _dot(vt, p)
        inv = 1.0 / (ot[LANES:LANES + 1, :] + jnp.exp2(sink_row - m))
        for pr in range(n_pair):
            ca = slice((2 * pr) * QBLK, (2 * pr + 1) * QBLK)
            cb = slice((2 * pr + 1) * QBLK, (2 * pr + 2) * QBLK)
            pair = jnp.concatenate([ot[:SWA_DIM, ca] * inv[:, ca], ot[SWA_DIM:LANES, cb] * inv[:, cb]],
                                   axis=0).T
            cols = slice(pr * LANES, (pr + 1) * LANES)
            o_ref[q0:q0 + QBLK, cols] = (pair * sg_ref[q0:q0 + QBLK, cols].astype(F32)).astype(BF16)

    s_next = scores(0)
    for blk in range(n_blk):
        s_cur = s_next
        if blk + 1 < n_blk:
            s_next = scores(blk + 1)
        process(blk, s_cur)


def _swa_attn(sinks, qt, k, vt, sg):
    b, d, s = qt.shape
    tq = SWA_TQ
    tok = lambda bi, qi: (bi, qi, 0)
    return pl.pallas_call(
        _swa_kernel,
        grid=(b, s // tq),
        in_specs=[pl.BlockSpec(memory_space=pltpu.SMEM),
                  pl.BlockSpec((None, d, tq), lambda bi, qi: (bi, 0, qi)),
                  pl.BlockSpec((None, s, LANES), lambda bi, qi: (bi, 0, 0)),
                  pl.BlockSpec((None, LANES, s), lambda bi, qi: (bi, 0, 0)),
                  pl.BlockSpec((None, tq, d), tok)],
        out_specs=pl.BlockSpec((None, tq, d), tok),
        out_shape=jax.ShapeDtypeStruct((b, s, d), BF16),
        compiler_params=_params(2),
        name="swa_attn",
    )(sinks, qt, k, vt, sg)


def _pad_heads(w, n_heads, dim, offsets):
    k = w.shape[0]
    n_off = len(offsets)
    w4 = w.reshape(k, n_heads // n_off, n_off, dim)
    parts = [jnp.pad(w4[:, :, i, :], ((0, 0), (0, 0), (off, LANES - off - dim)))
             for i, off in enumerate(offsets)]
    return jnp.stack(parts, axis=2).reshape(k, n_heads * LANES)


def _swap_halves(w, n_heads, dim):
    k = w.shape[0]
    w3 = w.reshape(k, n_heads, dim)
    return jnp.concatenate([w3[..., dim // 2:], w3[..., :dim // 2]], axis=-1).reshape(k, n_heads * dim)


def _even_weights(w_in, q_norm, w_uq, kv_norm, w_ukv, b_f):
    sizes = (MLA_Q_RANK, MLA_KV_RANK, MLA_ROPE, FOX_HEADS * FOX_DIM, FOX_HEADS * FOX_DIM,
             FOX_HEADS * FOX_DIM, FOX_HEADS, MLA_HEADS * MLA_V + FOX_HEADS * FOX_DIM)
    cuts = [int(c) for c in np.cumsum(sizes)[:-1]]
    w_cq, w_ckv, w_kpe, w_fq, w_fk, w_fv, w_f, w_g = jnp.split(w_in, cuts, axis=1)
    place = lambda w: _pad_heads(w, 1, MLA_ROPE, (MLA_NOPE,))
    wlat = jnp.concatenate([w_cq, w_ckv, place(w_kpe), _pad_heads(w_f, 1, FOX_HEADS, (0,))], axis=1)
    v_off = (0, MLA_V)
    ukv3 = w_ukv.reshape(MLA_KV_RANK, MLA_HEADS, MLA_NOPE + MLA_V)
    uk = ukv3[..., :MLA_NOPE].reshape(MLA_KV_RANK, MLA_HEADS * MLA_NOPE)
    uv = ukv3[..., MLA_NOPE:].reshape(MLA_KV_RANK, MLA_HEADS * MLA_V)
    bf = lambda a: a.astype(BF16)
    return dict(wlat=bf(wlat),
                wuqt=bf(_pad_heads(w_uq, MLA_HEADS, MLA_NOPE + MLA_ROPE, (0,)).T),
                wuk=bf(_pad_heads(uk, MLA_HEADS, MLA_NOPE, (0,))),
                wuvt=bf(_pad_heads(uv, MLA_HEADS, MLA_V, v_off).T),
                wfqt=bf(w_fq.T), wfk=bf(w_fk), wfvt=bf(w_fv.T),
                wg=bf(w_g), qn=q_norm.reshape(1, -1), kvn=kv_norm.reshape(1, -1),
                bf=jnp.pad(b_f, (0, LANES - FOX_HEADS)).reshape(1, LANES))


def _pair_heads(w):
    lead = w.shape[:-1]
    w4 = w.reshape(*lead, SWA_KV_HEADS, SWA_HEADS // SWA_KV_HEADS, -1)
    return jnp.swapaxes(w4, -3, -2).reshape(*lead, -1)


def _odd_weights(w_in, sinks, w_out):
    d = SWA_HEADS * SWA_DIM
    dkv = SWA_KV_HEADS * SWA_DIM
    w_q, w_k, w_v, w_g = jnp.split(w_in, [d, d + dkv, d + 2 * dkv], axis=1)
    pair_rows = lambda a: jnp.swapaxes(
        a.reshape(SWA_KV_HEADS, SWA_HEADS // SWA_KV_HEADS, SWA_DIM, -1), 0, 1).reshape(a.shape)
    return dict(wqt=pair_rows(w_q.T).astype(BF16), wkv=jnp.concatenate([w_k, w_v], axis=1).astype(BF16),
                wg=_pair_heads(w_g).astype(BF16),
                sinks=_pair_heads(sinks.reshape(-1, 1).T).reshape(-1),
                w_out=pair_rows(w_out).astype(BF16))


def _rope_tables(s):
    pos = np.arange(s, dtype=np.float64)

    def cs(dim):
        inv = ROPE_THETA ** (-np.arange(0, dim, 2, dtype=np.float64) / dim)
        ang = pos[:, None] * inv[None, :]
        cos, sin = np.cos(ang), np.sin(ang)
        return np.concatenate([cos, cos], axis=1), np.concatenate([-sin, sin], axis=1)

    c32, s32 = cs(MLA_ROPE)
    z = lambda n: np.zeros((s, n))
    pad = LANES - MLA_NOPE - MLA_ROPE
    scale = (MLA_NOPE + MLA_ROPE) ** -0.5 * LOG2E
    even = dict(cqt=(np.concatenate([np.ones((s, MLA_NOPE)), c32, z(pad)], axis=1) * scale).T,
                sqt=(np.concatenate([z(MLA_NOPE), s32, z(pad)], axis=1) * scale).T,
                ck=np.concatenate([z(MLA_NOPE), c32, z(pad)], axis=1),
                sk=np.concatenate([z(MLA_NOPE), s32, z(pad)], axis=1))
    c64, s64 = cs(SWA_DIM)
    odd = dict(c=np.concatenate([c64, c64], axis=1), s=np.concatenate([s64, s64], axis=1))
    odd.update(ct=odd["c"].T, st=odd["s"].T)
    as_f32 = lambda d: {k: jnp.asarray(np.ascontiguousarray(v), dtype=F32) for k, v in d.items()}
    return as_f32(even), as_f32(odd)


def kernel(x, even_w_in, even_q_norm, even_w_uq, even_kv_norm, even_w_ukv, even_b_f, even_w_out,
           even_ln_g, even_ln_b, odd_w_in, odd_sinks, odd_w_out, odd_ln_g, odd_ln_b):
    s = x.shape[1]
    even_tabs, odd_tabs = _rope_tables(s)
    bf = lambda a: a.astype(BF16)
    we = jax.vmap(_even_weights)(bf(even_w_in), even_q_norm, bf(even_w_uq), even_kv_norm, bf(even_w_ukv),
                                 even_b_f)
    wo = jax.vmap(_odd_weights)(bf(odd_w_in), odd_sinks, bf(odd_w_out))
    even_w_out = even_w_out.astype(BF16)
    row = lambda a: a[:, None, :]
    closing = {0: (even_w_out, row(even_ln_g), row(even_ln_b)),
               1: (wo["w_out"], row(odd_ln_g), row(odd_ln_b))}
    prev = None
    for layer in range(DEPTH):
        j = layer // 2
        if layer % 2 == 0:
            qt, k, vt, sg, *x_new = _even_proj(x, we, even_tabs, j, prev)
            x = x_new[0] if x_new else x
            o = _causal_attn(qt, k, vt, sg)
        else:
            qt, k, vt, sg, *x_new = _odd_proj(x, wo, odd_tabs, j, prev)
            x = x_new[0] if x_new else x
            o = _swa_attn(wo["sinks"][j], qt, k, vt, sg)
        prev = (o, *closing[layer % 2], j)
    o, w_out, g, bias, j = prev
    return _out_proj(o, x, w_out, g, bias, j)
```

```python
import functools
import math

import jax
import jax.numpy as jnp
import numpy as np
from jax import lax
from jax.experimental import pallas as pl
from jax.experimental.pallas import tpu as pltpu

D_MODEL = 1024
DEPTH = 4
ROPE_THETA = 10000.0
QBLK = 128
MLA_HEADS = 8
MLA_NOPE = 64
MLA_ROPE = 32
MLA_V = 64
MLA_Q_RANK = 256
MLA_KV_RANK = 128
FOX_HEADS = 8
FOX_DIM = 64
SWA_HEADS = 16
SWA_KV_HEADS = 2
SWA_DIM = 64
WINDOW = 128
RMS_EPS = 1e-6
LN_EPS = 1e-5
ALPHA = (2 * DEPTH) ** 0.25
LOG2E = math.log2(math.e)

LANES = 128
VMEM_LIMIT = 56 * 1024 * 1024
BF16 = jnp.bfloat16
F32 = jnp.float32

PROJ_TM = 512
OUT_TM = 1024
ATT_T = 512
ATT_HG = 4
SWA_TQ = 1024
EVEN_HEADS = MLA_HEADS + FOX_HEADS
EVEN_HW = EVEN_HEADS * LANES
MLA_HW = MLA_HEADS * LANES


def _dot(a, b):
    return jnp.dot(a, b, preferred_element_type=F32)


def _dot_nt(a, b):
    return lax.dot_general(a, b, (((1,), (1,)), ((), ())), preferred_element_type=F32)


def _rms(t, g):
    return t * lax.rsqrt(jnp.mean(t * t, axis=-1, keepdims=True) + RMS_EPS) * g


def _silu(g):
    return g / (1.0 + jnp.exp(-g))


def _pack3(v):
    hi = v.astype(BF16).astype(F32)
    r1 = v - hi
    mid = r1.astype(BF16).astype(F32)
    lo = r1 - mid
    return (hi + pltpu.roll(mid, FOX_HEADS, 1) + pltpu.roll(lo, 2 * FOX_HEADS, 1)).astype(BF16)


def _params(n_axes, flags=None):
    return pltpu.CompilerParams(dimension_semantics=("arbitrary",) * n_axes,
                                vmem_limit_bytes=VMEM_LIMIT, flags=flags)


def _residual_norm(o_ref, x_ref, w_ref, g_ref, b_ref):
    z = ALPHA * x_ref[...] + _dot(o_ref[...], w_ref[...])
    mu = jnp.mean(z, axis=-1, keepdims=True)
    zc = z - mu
    var = jnp.mean(zc * zc, axis=-1, keepdims=True)
    return zc * lax.rsqrt(var + LN_EPS) * g_ref[...] + b_ref[...]


def _layer_input(refs, fused):
    if not fused:
        return refs[0][...], refs[1:]
    x = _residual_norm(*refs[:5])
    return x, refs[5:]


def _even_proj_kernel(*refs, fused):
    x, refs = _layer_input(refs, fused)
    (wlat_ref, wuqt_ref, wuk_ref, wuvt_ref, wfqt_ref, wfk_ref, wfvt_ref, wg_ref, qn_ref, kvn_ref,
     bf_ref, cqt_ref, sqt_ref, ck_ref, sk_ref, qt_ref, k_ref, vt_ref, sg_ref) = refs[:19]
    carry_ref = refs[-1]
    if fused:
        refs[19][...] = x
    tm = x.shape[0]
    xb = x.astype(BF16)

    hrow = lax.broadcasted_iota(jnp.int32, (MLA_HW, 1), 0) % (2 * LANES)
    ones_col = jnp.where((hrow == MLA_V) | (hrow == LANES), 1.0, 0.0).astype(F32)
    lat = _dot(xb, wlat_ref[...])
    o = 0
    cq = lat[:, o:o + MLA_Q_RANK]
    o += MLA_Q_RANK
    ckv = lat[:, o:o + MLA_KV_RANK]
    o += MLA_KV_RANK
    kp = lat[:, o:o + LANES]
    hr = MLA_ROPE // 2
    lane_p = lax.broadcasted_iota(jnp.int32, (1, LANES), 1)
    kp_sw = jnp.where(lane_p < MLA_NOPE + hr, pltpu.roll(kp, LANES - hr, 1), pltpu.roll(kp, hr, 1))
    kpe = kp * ck_ref[...] + kp_sw * sk_ref[...]
    o += LANES
    fl = lat[:, o:o + LANES] + bf_ref[...]

    rq = _rms(cq, qn_ref[...]).astype(BF16)
    qa = _dot_nt(wuqt_ref[...], rq)
    cqt = cqt_ref[...]
    sqt = sqt_ref[...]
    pe = MLA_NOPE
    for h in range(MLA_HEADS):
        hs = slice(h * LANES, (h + 1) * LANES)
        blk = qa[hs, :]
        q_sw = jnp.concatenate([blk[:pe], blk[pe + hr:pe + 2 * hr], blk[pe:pe + hr], blk[pe + 2 * hr:]], axis=0)
        qt_ref[hs, :] = (blk * cqt + q_sw * sqt).astype(BF16)
    rkv = _rms(ckv, kvn_ref[...]).astype(BF16)
    kk = _dot(rkv, wuk_ref[...])
    for h in range(MLA_HEADS):
        hs = slice(h * LANES, (h + 1) * LANES)
        k_ref[:, hs] = (kk[:, hs] + kpe).astype(BF16)
    vt_ref[:MLA_HW, :] = (_dot_nt(wuvt_ref[...], rkv) + ones_col).astype(BF16)

    lane = lax.broadcasted_iota(jnp.int32, (1, LANES), 1)
    lf = (jnp.minimum(fl, 0.0) - jnp.log1p(jnp.exp(-jnp.abs(fl)))) * LOG2E
    lf = jnp.where(lane < FOX_HEADS, lf, 0.0)
    row = lax.broadcasted_iota(jnp.int32, (tm, tm), 0)
    col = lax.broadcasted_iota(jnp.int32, (tm, tm), 1)
    tri = jnp.where(col <= row, 1.0, 0.0).astype(BF16)
    sums = _dot(tri, _pack3(lf))
    c_loc = sums + pltpu.roll(sums, LANES - FOX_HEADS, 1) + pltpu.roll(sums, LANES - 2 * FOX_HEADS, 1)
    c_loc = jnp.where(lane < FOX_HEADS, c_loc, 0.0)

    @pl.when(pl.program_id(1) == 0)
    def _():
        carry_ref[...] = jnp.zeros_like(carry_ref)

    c = c_loc + carry_ref[0:1, :]
    carry_ref[...] = jnp.broadcast_to(c[tm - 1:tm, :], carry_ref.shape)
    src = lax.broadcasted_iota(jnp.int32, (LANES, MLA_HW), 0)
    dst = lax.broadcasted_iota(jnp.int32, (LANES, MLA_HW), 1)
    src_h = src % FOX_HEADS
    c_lane = src_h * LANES + jnp.where(src_h % 2 == 0, FOX_DIM, 0) + src // FOX_HEADS
    place = jnp.where((dst == c_lane) & (src < 3 * FOX_HEADS), 1.0, 0.0).astype(BF16)
    k_extra = _dot(_pack3(c), place)

    fqt = _dot_nt(wfqt_ref[...], xb) * (FOX_DIM ** -0.5 * LOG2E)
    fk = _dot(xb, wfk_ref[...])
    fvt = _dot_nt(wfvt_ref[...], xb)
    half_row = lax.broadcasted_iota(jnp.int32, (FOX_DIM, tm), 0)
    neg_blk = jnp.where(half_row < 3, -1.0, 0.0).astype(BF16)
    ones_blk = jnp.where(half_row < 1, 1.0, 0.0).astype(BF16)
    lane_k = lax.broadcasted_iota(jnp.int32, (1, LANES), 1)
    for h in range(FOX_HEADS):
        off = (h % 2) * FOX_DIM
        base = MLA_HW + h * LANES
        data = slice(base + off, base + off + FOX_DIM)
        rest = slice(base + FOX_DIM - off, base + 2 * FOX_DIM - off)
        rows = slice(h * FOX_DIM, (h + 1) * FOX_DIM)
        qt_ref[data, :] = fqt[rows, :].astype(BF16)
        qt_ref[rest, :] = neg_blk
        vt_ref[data, :] = fvt[rows, :].astype(BF16)
        vt_ref[rest, :] = ones_blk
        pair = slice((h // 2) * LANES, (h // 2 + 1) * LANES)
        in_data = (lane_k >= off) & (lane_k < off + FOX_DIM)
        k_ref[:, base:base + LANES] = jnp.where(
            in_data, fk[:, pair], k_extra[:, h * LANES:(h + 1) * LANES]).astype(BF16)

    sg_ref[...] = _silu(_dot(xb, wg_ref[...])).astype(BF16)


def _input_specs(x, prev):
    b, s, d = x.shape
    tok = pl.BlockSpec((None, PROJ_TM, d), lambda bi, si: (bi, si, 0))
    if prev is None:
        return [x], [tok], [], []
    o, w_out, g, bias, layer = prev
    specs = [tok, tok] + [_layer_spec(a, layer) for a in (w_out, g, bias)]
    return [o, x, w_out, g, bias], specs, [tok], [jax.ShapeDtypeStruct((b, s, d), F32)]


def _even_proj(x, w, tabs, layer, prev=None):
    b, s, d = x.shape
    tm = PROJ_TM
    tok = lambda bi, si: (bi, si, 0)
    tok_t = lambda bi, si: (bi, 0, si)
    names = ("wlat", "wuqt", "wuk", "wuvt", "wfqt", "wfk", "wfvt", "wg", "qn", "kvn", "bf")
    operands, in_specs, x_spec, x_shape = _input_specs(x, prev)
    in_specs += [_layer_spec(w[k], layer) for k in names]
    in_specs += [pl.BlockSpec((LANES, tm), lambda bi, si: (0, si))] * 2
    in_specs += [pl.BlockSpec((tm, LANES), lambda bi, si: (si, 0))] * 2
    out_specs = [pl.BlockSpec((None, EVEN_HW, tm), tok_t), pl.BlockSpec((None, tm, EVEN_HW), tok),
                 pl.BlockSpec((None, EVEN_HW, tm), tok_t), pl.BlockSpec((None, tm, d), tok)]
    out_shape = [jax.ShapeDtypeStruct((b, EVEN_HW, s), BF16), jax.ShapeDtypeStruct((b, s, EVEN_HW), BF16),
                 jax.ShapeDtypeStruct((b, EVEN_HW, s), BF16), jax.ShapeDtypeStruct((b, s, d), BF16)]
    return pl.pallas_call(
        functools.partial(_even_proj_kernel, fused=prev is not None),
        grid=(b, s // tm),
        in_specs=in_specs,
        out_specs=out_specs + x_spec,
        out_shape=out_shape + x_shape,
        scratch_shapes=[pltpu.VMEM((8, LANES), F32)],
        compiler_params=_params(2),
        name="even_proj",
    )(*operands, *[w[k] for k in names], tabs["cqt"], tabs["sqt"], tabs["ck"], tabs["sk"])


def _attn_write_out(tile, accs, sg_ref, o_ref):
    t = ATT_T
    rows = pl.ds(pl.multiple_of(tile * t, t), t)
    row = lax.broadcasted_iota(jnp.int32, (LANES, t), 0)
    for hp in range(len(accs) // 2):
        a0 = accs[2 * hp]
        a1 = accs[2 * hp + 1]
        pair_t = jnp.where(row < MLA_V, a0 / a0[MLA_V:MLA_V + 1, :], a1 / a1[0:1, :])
        cols = slice(hp * LANES, (hp + 1) * LANES)
        o_ref[rows, cols] = (pair_t.T * sg_ref[rows, cols].astype(F32)).astype(BF16)


def _attn_query_tile(qi, prev, qt_ref, k_ref, vt_ref, sg_ref, o_ref, sa_ref, sb_ref, sc_ref, mc_ref):
    t = ATT_T
    hf = t // 2
    n_heads = qt_ref.shape[0] // LANES
    nq = qt_ref.shape[1] // t
    heads = [slice(h * LANES, (h + 1) * LANES) for h in range(n_heads)]

    def q_slice(tile, start=0, size=None):
        size = t if size is None else size
        return pl.ds(pl.multiple_of(tile * t + start, size), size)

    def key_slice(kj, start=0, size=None):
        size = t if size is None else size
        return pl.ds(pl.multiple_of(kj * t + start, size), size)

    def col_max(s):
        return jnp.max(s, axis=0, keepdims=True)

    def causal_mask(s):
        n = s.shape[0]
        keep = (lax.broadcasted_iota(jnp.int32, (n, n), 0) <= lax.broadcasted_iota(jnp.int32, (n, n), 1))
        return jnp.where(keep, s, -jnp.inf)

    def scores_head(q_tile, kj, buf, h):
        s = _dot(k_ref[key_slice(kj), heads[h]], qt_ref[heads[h], q_slice(q_tile)])
        buf[h] = s
        return col_max(s)

    def diag_scores_head(buf, h):
        early = _dot(k_ref[key_slice(qi, 0, hf), heads[h]], qt_ref[heads[h], q_slice(qi)])
        late = _dot(k_ref[key_slice(qi, hf, hf), heads[h]], qt_ref[heads[h], q_slice(qi, hf, hf)])
        early = jnp.concatenate([causal_mask(early[:, :hf]), early[:, hf:]], axis=1)
        late = causal_mask(late)
        buf[h, :hf, :] = early
        buf[h, hf:, hf:] = late
        return jnp.concatenate([col_max(early[:, :hf]),
                                jnp.maximum(col_max(early[:, hf:]), col_max(late))], axis=1)

    def process_head(kj, s, h, tile_max, m, acc):
        m_new = jnp.maximum(m, tile_max)
        p = jnp.exp2(s - m_new).astype(BF16)
        return m_new, jnp.exp2(m - m_new) * acc + _dot(vt_ref[heads[h], key_slice(kj)], p)

    def diag_process_head(buf, h, tile_max, m, acc):
        m_new = jnp.maximum(m, tile_max)
        p_early = jnp.exp2(buf[h, :hf, :] - m_new).astype(BF16)
        p_late = jnp.exp2(buf[h, hf:, hf:] - m_new[:, hf:]).astype(BF16)
        pv_early = _dot(vt_ref[heads[h], key_slice(qi, 0, hf)], p_early)
        pv_late = _dot(vt_ref[heads[h], key_slice(qi, hf, hf)], p_late)
        scaled = jnp.exp2(m - m_new) * acc + pv_early
        return m_new, jnp.concatenate([scaled[:, :hf], scaled[:, hf:] + pv_late], axis=1)

    def stage(kj, cur, nxt, tile_max, state, diag_next=False):
        next_max, out = [], []
        for h in range(n_heads):
            next_max.append(diag_scores_head(nxt, h) if diag_next else scores_head(qi, kj + 1, nxt, h))
            out.append(process_head(kj, cur[h], h, tile_max[h], *state[h]))
        return tuple(next_max), tuple(out)

    def two_stages(i, carry):
        tile_max, state = carry
        tile_max, state = stage(2 * i + 1, sa_ref, sb_ref, tile_max, state)
        return stage(2 * i + 2, sb_ref, sa_ref, tile_max, state)

    def last_tile(buf, tile_max, state):
        out = []
        for h in range(n_heads):
            mc_ref[h, 0:1, :] = scores_head(jnp.minimum(qi + 1, nq - 1), 0, sc_ref, h)
            out.append(diag_process_head(buf, h, tile_max[h], *state[h]))
        return tuple(out)

    def first_stage(state, diag_next=False):
        tile_max = tuple(mc_ref[h, 0:1, :] for h in range(n_heads))
        return stage(0, sc_ref, sa_ref, tile_max, state, diag_next)

    def fresh():
        return tuple((jnp.full((1, t), -jnp.inf, F32), jnp.zeros((LANES, t), F32)) for _ in range(n_heads))

    def accs_of(state):
        return tuple(acc for _, acc in state)

    def case_first(prev):
        tile_max = tuple(diag_scores_head(sa_ref, h) for h in range(n_heads))
        return accs_of(last_tile(sa_ref, tile_max, fresh()))

    def case_second(prev):
        _attn_write_out(qi - 1, prev, sg_ref, o_ref)
        tile_max, state = first_stage(fresh(), diag_next=True)
        return accs_of(last_tile(sa_ref, tile_max, state))

    def case_even(prev):
        _attn_write_out(qi - 1, prev, sg_ref, o_ref)
        tile_max, state = first_stage(fresh())
        tile_max, state = lax.fori_loop(0, (qi - 2) // 2, two_stages, (tile_max, state))
        tile_max, state = stage(qi - 1, sa_ref, sb_ref, tile_max, state, diag_next=True)
        return accs_of(last_tile(sb_ref, tile_max, state))

    def case_odd(prev):
        _attn_write_out(qi - 1, prev, sg_ref, o_ref)
        tile_max, state = first_stage(fresh())
        tile_max, state = lax.fori_loop(0, (qi - 3) // 2, two_stages, (tile_max, state))
        tile_max, state = stage(qi - 2, sa_ref, sb_ref, tile_max, state)
        tile_max, state = stage(qi - 1, sb_ref, sa_ref, tile_max, state, diag_next=True)
        return accs_of(last_tile(sa_ref, tile_max, state))

    def case_later(prev):
        return lax.cond(qi % 2 == 0, case_even, case_odd, prev)

    def case_not_first(prev):
        return lax.cond(qi == 1, case_second, case_later, prev)

    return lax.cond(qi == 0, case_first, case_not_first, prev)


def _causal_attn_kernel(qt_ref, k_ref, vt_ref, sg_ref, o_ref, sa_ref, sb_ref, sc_ref, mc_ref):
    n_heads = qt_ref.shape[0] // LANES
    nq = qt_ref.shape[1] // ATT_T

    def body(qi, prev):
        return _attn_query_tile(qi, prev, qt_ref, k_ref, vt_ref, sg_ref, o_ref, sa_ref, sb_ref, sc_ref, mc_ref)

    none_yet = tuple(jnp.zeros((LANES, ATT_T), F32) for _ in range(n_heads))
    last = lax.fori_loop(0, nq, body, none_yet)
    _attn_write_out(nq - 1, last, sg_ref, o_ref)


def _causal_attn(qt, k, vt, sg):
    b, s, hw = k.shape
    t = ATT_T
    gw = ATT_HG * LANES
    ow = ATT_HG * MLA_V
    score_buf = pltpu.VMEM((ATT_HG, t, t), F32)
    return pl.pallas_call(
        _causal_attn_kernel,
        grid=(b, hw // gw),
        in_specs=[pl.BlockSpec((None, gw, s), lambda bi, g: (bi, g, 0)),
                  pl.BlockSpec((None, s, gw), lambda bi, g: (bi, 0, g)),
                  pl.BlockSpec((None, gw, s), lambda bi, g: (bi, g, 0)),
                  pl.BlockSpec((None, s, ow), lambda bi, g: (bi, 0, g))],
        out_specs=pl.BlockSpec((None, s, ow), lambda bi, g: (bi, 0, g)),
        out_shape=jax.ShapeDtypeStruct((b, s, (hw // LANES) * MLA_V), BF16),
        scratch_shapes=[score_buf, score_buf, score_buf, pltpu.VMEM((ATT_HG, 8, t), F32)],
        compiler_params=_params(2),
        name="causal_attn",
    )(qt, k, vt, sg)


def _out_kernel(o_ref, x_ref, w_ref, g_ref, b_ref, y_ref):
    y_ref[...] = _residual_norm(o_ref, x_ref, w_ref, g_ref, b_ref)


def _out_proj(o, x, w_out, g, bias, layer):
    b, s, d = x.shape
    tm = OUT_TM
    tok = lambda bi, si: (bi, si, 0)
    return pl.pallas_call(
        _out_kernel,
        grid=(b, s // tm),
        in_specs=[pl.BlockSpec((None, tm, d), tok), pl.BlockSpec((None, tm, d), tok),
                  _layer_spec(w_out, layer), _layer_spec(g, layer), _layer_spec(bias, layer)],
        out_specs=pl.BlockSpec((None, tm, d), tok),
        out_shape=jax.ShapeDtypeStruct((b, s, d), F32),
        compiler_params=_params(2),
        name="out_proj_ln",
    )(o, x, w_out, g, bias)


def _odd_proj_kernel(*refs, fused):
    x, refs = _layer_input(refs, fused)
    (wqt_ref, wkv_ref, wg_ref, c_ref, s_ref, ct_ref, st_ref,
     qt_ref, k_ref, vt_ref, sg_ref) = refs[:11]
    if fused:
        refs[11][...] = x
    d = SWA_HEADS * SWA_DIM
    half = SWA_DIM // 2
    xb = x.astype(BF16)

    kv = _dot(xb, wkv_ref[...])
    vt_ref[...] = kv[:, LANES:].T.astype(BF16)
    kk = kv[:, :LANES]
    lane = lax.broadcasted_iota(jnp.int32, (1, LANES), 1)
    first_half = (lane % SWA_DIM) < half
    k_sw = jnp.where(first_half, pltpu.roll(kk, LANES - half, 1), pltpu.roll(kk, half, 1))
    k_ref[...] = (kk * c_ref[...] + k_sw * s_ref[...]).astype(BF16)

    qq = _dot_nt(wqt_ref[...], xb)
    ctt = ct_ref[...]
    stt = st_ref[...]
    scale = SWA_DIM ** -0.5 * LOG2E
    for p in range(d // LANES):
        blk = qq[p * LANES:(p + 1) * LANES, :]
        q_sw = jnp.concatenate([blk[half:2 * half], blk[:half], blk[3 * half:], blk[2 * half:3 * half]], axis=0)
        qt_ref[p * LANES:(p + 1) * LANES, :] = ((blk * ctt + q_sw * stt) * scale).astype(BF16)

    sg_ref[...] = _silu(_dot(xb, wg_ref[...])).astype(BF16)


def _layer_spec(a, layer):
    return pl.BlockSpec((None,) + a.shape[1:], lambda bi, si: (layer,) + (0,) * (a.ndim - 1))


def _odd_proj(x, w, tabs, layer, prev=None):
    b, s, d = x.shape
    tm = PROJ_TM
    tok = lambda bi, si: (bi, si, 0)
    tok_t = lambda bi, si: (bi, 0, si)
    tab = pl.BlockSpec((tm, LANES), lambda bi, si: (si, 0))
    tab_t = pl.BlockSpec((LANES, tm), lambda bi, si: (0, si))
    names = ("wqt", "wkv", "wg")
    operands, in_specs, x_spec, x_shape = _input_specs(x, prev)
    return pl.pallas_call(
        functools.partial(_odd_proj_kernel, fused=prev is not None),
        grid=(b, s // tm),
        in_specs=in_specs + [_layer_spec(w[k], layer) for k in names] + [tab, tab, tab_t, tab_t],
        out_specs=[pl.BlockSpec((None, d, tm), tok_t), pl.BlockSpec((None, tm, LANES), tok),
                   pl.BlockSpec((None, LANES, tm), tok_t), pl.BlockSpec((None, tm, d), tok)] + x_spec,
        out_shape=[jax.ShapeDtypeStruct((b, d, s), BF16), jax.ShapeDtypeStruct((b, s, LANES), BF16),
                   jax.ShapeDtypeStruct((b, LANES, s), BF16), jax.ShapeDtypeStruct((b, s, d), BF16)] + x_shape,
        compiler_params=_params(2),
        name="odd_proj",
    )(*operands, *[w[k] for k in names], tabs["c"], tabs["s"], tabs["ct"], tabs["st"])


def _swa_kernel(sink_ref, qt_ref, k_ref, vt_ref, sg_ref, o_ref):
    n_blk = qt_ref.shape[1] // QBLK
    n_pair = qt_ref.shape[0] // LANES
    win = 2 * QBLK
    qi = pl.program_id(1)
    row = lax.broadcasted_iota(jnp.int32, (LANES, QBLK), 0)
    lo_rows = row < SWA_DIM
    rel = (lax.broadcasted_iota(jnp.int32, (win, QBLK), 1) - lax.broadcasted_iota(jnp.int32, (win, QBLK), 0))
    sink_row = jnp.concatenate([jnp.full((1, QBLK), sink_ref[c] * LOG2E, F32) for c in range(2 * n_pair)],
                               axis=1)
    ones_blk = jnp.ones((16, win), BF16)

    def window_start(blk):
        return pl.multiple_of(jnp.maximum(qi * n_blk + blk - 1, 0) * QBLK, QBLK)

    def scores(blk):
        q0 = blk * QBLK
        cols = []
        for p in range(n_pair):
            qp = qt_ref[p * LANES:(p + 1) * LANES, q0:q0 + QBLK]
            zero = jnp.zeros_like(qp)
            cols.append(jnp.where(lo_rows, qp, zero))
            cols.append(jnp.where(lo_rows, zero, qp))
        s = _dot(k_ref[pl.ds(window_start(blk), win), :], jnp.concatenate(cols, axis=1))
        shift = jnp.where(qi * n_blk + blk == 0, 0, QBLK)
        diff = rel + shift
        band = (diff >= 0) & (diff < WINDOW)
        return jnp.concatenate([jnp.where(band, s[:, c * QBLK:(c + 1) * QBLK], -jnp.inf)
                                for c in range(2 * n_pair)], axis=1)

    def process(blk, s):
        q0 = blk * QBLK
        m = jnp.maximum(jnp.max(s, axis=0, keepdims=True), sink_row)
        p = jnp.exp2(s - m).astype(BF16)
        vt = jnp.concatenate([vt_ref[:, pl.ds(window_start(blk), win)], ones_blk], axis=0)
        ot = _dot(vt, p)
        inv = 1.0 / (ot[LANES:LANES + 1, :] + jnp.exp2(sink_row - m))
        for pr in range(n_pair):
            ca = slice((2 * pr) * QBLK, (2 * pr + 1) * QBLK)
            cb = slice((2 * pr + 1) * QBLK, (2 * pr + 2) * QBLK)
            pair = jnp.concatenate([ot[:SWA_DIM, ca] * inv[:, ca], ot[SWA_DIM:LANES, cb] * inv[:, cb]],
                                   axis=0).T
            cols = slice(pr * LANES, (pr + 1) * LANES)
            o_ref[q0:q0 + QBLK, cols] = (pair * sg_ref[q0:q0 + QBLK, cols].astype(F32)).astype(BF16)

    s_next = scores(0)
    for blk in range(n_blk):
        s_cur = s_next
        if blk + 1 < n_blk:
            s_next = scores(blk + 1)
        process(blk, s_cur)


def _swa_attn(sinks, qt, k, vt, sg):
    b, d, s = qt.shape
    tq = SWA_TQ
    tok = lambda bi, qi: (bi, qi, 0)
    return pl.pallas_call(
        _swa_kernel,
        grid=(b, s // tq),
        in_specs=[pl.BlockSpec(memory_space=pltpu.SMEM),
                  pl.BlockSpec((None, d, tq), lambda bi, qi: (bi, 0, qi)),
                  pl.BlockSpec((None, s, LANES), lambda bi, qi: (bi, 0, 0)),
                  pl.BlockSpec((None, LANES, s), lambda bi, qi: (bi, 0, 0)),
                  pl.BlockSpec((None, tq, d), tok)],
        out_specs=pl.BlockSpec((None, tq, d), tok),
        out_shape=jax.ShapeDtypeStruct((b, s, d), BF16),
        compiler_params=_params(2),
        name="swa_attn",
    )(sinks, qt, k, vt, sg)


def _pad_heads(w, n_heads, dim, offsets):
    k = w.shape[0]
    n_off = len(offsets)
    w4 = w.reshape(k, n_heads // n_off, n_off, dim)
    parts = [jnp.pad(w4[:, :, i, :], ((0, 0), (0, 0), (off, LANES - off - dim)))
             for i, off in enumerate(offsets)]
    return jnp.stack(parts, axis=2).reshape(k, n_heads * LANES)


def _even_weights(w_in, q_norm, w_uq, kv_norm, w_ukv, b_f):
    sizes = (MLA_Q_RANK, MLA_KV_RANK, MLA_ROPE, FOX_HEADS * FOX_DIM, FOX_HEADS * FOX_DIM,
             FOX_HEADS * FOX_DIM, FOX_HEADS, MLA_HEADS * MLA_V + FOX_HEADS * FOX_DIM)
    cuts = [int(c) for c in np.cumsum(sizes)[:-1]]
    w_cq, w_ckv, w_kpe, w_fq, w_fk, w_fv, w_f, w_g = jnp.split(w_in, cuts, axis=1)
    place = lambda w: _pad_heads(w, 1, MLA_ROPE, (MLA_NOPE,))
    wlat = jnp.concatenate([w_cq, w_ckv, place(w_kpe), _pad_heads(w_f, 1, FOX_HEADS, (0,))], axis=1)
    v_off = (0, MLA_V)
    ukv3 = w_ukv.reshape(MLA_KV_RANK, MLA_HEADS, MLA_NOPE + MLA_V)
    uk = ukv3[..., :MLA_NOPE].reshape(MLA_KV_RANK, MLA_HEADS * MLA_NOPE)
    uv = ukv3[..., MLA_NOPE:].reshape(MLA_KV_RANK, MLA_HEADS * MLA_V)
    bf = lambda a: a.astype(BF16)
    return dict(wlat=bf(wlat),
                wuqt=bf(_pad_heads(w_uq, MLA_HEADS, MLA_NOPE + MLA_ROPE, (0,)).T),
                wuk=bf(_pad_heads(uk, MLA_HEADS, MLA_NOPE, (0,))),
                wuvt=bf(_pad_heads(uv, MLA_HEADS, MLA_V, v_off).T),
                wfqt=bf(w_fq.T), wfk=bf(w_fk), wfvt=bf(w_fv.T),
                wg=bf(w_g), qn=q_norm.reshape(1, -1), kvn=kv_norm.reshape(1, -1),
                bf=jnp.pad(b_f, (0, LANES - FOX_HEADS)).reshape(1, LANES))


def _pair_heads(w):
    lead = w.shape[:-1]
    w4 = w.reshape(*lead, SWA_KV_HEADS, SWA_HEADS // SWA_KV_HEADS, -1)
    return jnp.swapaxes(w4, -3, -2).reshape(*lead, -1)


def _odd_weights(w_in, sinks, w_out):
    d = SWA_HEADS * SWA_DIM
    dkv = SWA_KV_HEADS * SWA_DIM
    w_q, w_k, w_v, w_g = jnp.split(w_in, [d, d + dkv, d + 2 * dkv], axis=1)
    pair_rows = lambda a: jnp.swapaxes(
        a.reshape(SWA_KV_HEADS, SWA_HEADS // SWA_KV_HEADS, SWA_DIM, -1), 0, 1).reshape(a.shape)
    return dict(wqt=pair_rows(w_q.T).astype(BF16), wkv=jnp.concatenate([w_k, w_v], axis=1).astype(BF16),
                wg=_pair_heads(w_g).astype(BF16),
                sinks=_pair_heads(sinks.reshape(-1, 1).T).reshape(-1),
                w_out=pair_rows(w_out).astype(BF16))


def _rope_tables(s):
    pos = np.arange(s, dtype=np.float64)

    def cs(dim):
        inv = ROPE_THETA ** (-np.arange(0, dim, 2, dtype=np.float64) / dim)
        ang = pos[:, None] * inv[None, :]
        cos, sin = np.cos(ang), np.sin(ang)
        return np.concatenate([cos, cos], axis=1), np.concatenate([-sin, sin], axis=1)

    c32, s32 = cs(MLA_ROPE)
    z = lambda n: np.zeros((s, n))
    pad = LANES - MLA_NOPE - MLA_ROPE
    scale = (MLA_NOPE + MLA_ROPE) ** -0.5 * LOG2E
    even = dict(cqt=(np.concatenate([np.ones((s, MLA_NOPE)), c32, z(pad)], axis=1) * scale).T,
                sqt=(np.concatenate([z(MLA_NOPE), s32, z(pad)], axis=1) * scale).T,
                ck=np.concatenate([z(MLA_NOPE), c32, z(pad)], axis=1),
                sk=np.concatenate([z(MLA_NOPE), s32, z(pad)], axis=1))
    c64, s64 = cs(SWA_DIM)
    odd = dict(c=np.concatenate([c64, c64], axis=1), s=np.concatenate([s64, s64], axis=1))
    odd.update(ct=odd["c"].T, st=odd["s"].T)
    as_f32 = lambda d: {k: jnp.asarray(np.ascontiguousarray(v), dtype=F32) for k, v in d.items()}
    return as_f32(even), as_f32(odd)


def kernel(x, even_w_in, even_q_norm, even_w_uq, even_kv_norm, even_w_ukv, even_b_f, even_w_out,
           even_ln_g, even_ln_b, odd_w_in, odd_sinks, odd_w_out, odd_ln_g, odd_ln_b):
    s = x.shape[1]
    even_tabs, odd_tabs = _rope_tables(s)
    bf = lambda a: a.astype(BF16)
    we = jax.vmap(_even_weights)(bf(even_w_in), even_q_norm, bf(even_w_uq), even_kv_norm, bf(even_w_ukv),
                                 even_b_f)
    wo = jax.vmap(_odd_weights)(bf(odd_w_in), odd_sinks, bf(odd_w_out))
    even_w_out = even_w_out.astype(BF16)
    row = lambda a: a[:, None, :]
    closing = {0: (even_w_out, row(even_ln_g), row(even_ln_b)),
               1: (wo["w_out"], row(odd_ln_g), row(odd_ln_b))}
    prev = None
    for layer in range(DEPTH):
        j = layer // 2
        if layer % 2 == 0:
            qt, k, vt, sg, *x_new = _even_proj(x, we, even_tabs, j, prev)
            x = x_new[0] if x_new else x
            o = _causal_attn(qt, k, vt, sg)
        else:
            qt, k, vt, sg, *x_new = _odd_proj(x, wo, odd_tabs, j, prev)
            x = x_new[0] if x_new else x
            o = _swa_attn(wo["sinks"][j], qt, k, vt, sg)
        prev = (o, *closing[layer % 2], j)
    o, w_out, g, bias, j = prev
    return _out_proj(o, x, w_out, g, bias, j)
```

```python
import functools
import math

import jax
import jax.numpy as jnp
import numpy as np
from jax import lax
from jax.experimental import pallas as pl
from jax.experimental.pallas import tpu as pltpu

D_MODEL = 1024
DEPTH = 4
ROPE_THETA = 10000.0
QBLK = 128
MLA_HEADS = 8
MLA_NOPE = 64
MLA_ROPE = 32
MLA_V = 64
MLA_Q_RANK = 256
MLA_KV_RANK = 128
FOX_HEADS = 8
FOX_DIM = 64
SWA_HEADS = 16
SWA_KV_HEADS = 2
SWA_DIM = 64
WINDOW = 128
RMS_EPS = 1e-6
LN_EPS = 1e-5
ALPHA = (2 * DEPTH) ** 0.25
LOG2E = math.log2(math.e)

LANES = 128
VMEM_LIMIT = 56 * 1024 * 1024
BF16 = jnp.bfloat16
F32 = jnp.float32

PROJ_TM = 512
OUT_TM = 1024
ATT_T = 512
ATT_HG = 4
SWA_TQ = 1024
EVEN_HEADS = MLA_HEADS + FOX_HEADS
EVEN_HW = EVEN_HEADS * LANES
MLA_HW = MLA_HEADS * LANES


def _dot(a, b):
    return jnp.dot(a, b, preferred_element_type=F32)


def _dot_nt(a, b):
    return lax.dot_general(a, b, (((1,), (1,)), ((), ())), preferred_element_type=F32)


def _rms(t, g):
    return t * lax.rsqrt(jnp.mean(t * t, axis=-1, keepdims=True) + RMS_EPS) * g


def _silu(g):
    return g / (1.0 + jnp.exp(-g))


def _pack3(v):
    hi = v.astype(BF16).astype(F32)
    r1 = v - hi
    mid = r1.astype(BF16).astype(F32)
    lo = r1 - mid
    return (hi + pltpu.roll(mid, FOX_HEADS, 1) + pltpu.roll(lo, 2 * FOX_HEADS, 1)).astype(BF16)


def _params(n_axes, flags=None):
    return pltpu.CompilerParams(dimension_semantics=("arbitrary",) * n_axes,
                                vmem_limit_bytes=VMEM_LIMIT, flags=flags)


def _residual_norm(o_ref, x_ref, w_ref, g_ref, b_ref):
    z = ALPHA * x_ref[...] + _dot(o_ref[...], w_ref[...])
    mu = jnp.mean(z, axis=-1, keepdims=True)
    zc = z - mu
    var = jnp.mean(zc * zc, axis=-1, keepdims=True)
    return zc * lax.rsqrt(var + LN_EPS) * g_ref[...] + b_ref[...]


def _layer_input(refs, fused):
    if not fused:
        return refs[0][...], refs[1:]
    x = _residual_norm(*refs[:5])
    return x, refs[5:]


def _even_proj_kernel(*refs, fused):
    x, refs = _layer_input(refs, fused)
    (wlat_ref, wuqt_ref, wuk_ref, wuvt_ref, wfqt_ref, wfk_ref, wfvt_ref, wg_ref, qn_ref, kvn_ref,
     bf_ref, cqt_ref, sqt_ref, ck_ref, sk_ref, qt_ref, k_ref, vt_ref, sg_ref) = refs[:19]
    carry_ref = refs[-1]
    if fused:
        refs[19][...] = x
    tm = x.shape[0]
    xb = x.astype(BF16)

    hrow = lax.broadcasted_iota(jnp.int32, (MLA_HW, 1), 0) % (2 * LANES)
    ones_col = jnp.where((hrow == MLA_V) | (hrow == LANES), 1.0, 0.0).astype(F32)
    lat = _dot(xb, wlat_ref[...])
    o = 0
    cq = lat[:, o:o + MLA_Q_RANK]
    o += MLA_Q_RANK
    ckv = lat[:, o:o + MLA_KV_RANK]
    o += MLA_KV_RANK
    kp = lat[:, o:o + LANES]
    hr = MLA_ROPE // 2
    lane_p = lax.broadcasted_iota(jnp.int32, (1, LANES), 1)
    kp_sw = jnp.where(lane_p < MLA_NOPE + hr, pltpu.roll(kp, LANES - hr, 1), pltpu.roll(kp, hr, 1))
    kpe = kp * ck_ref[...] + kp_sw * sk_ref[...]
    o += LANES
    fl = lat[:, o:o + LANES] + bf_ref[...]

    rq = _rms(cq, qn_ref[...]).astype(BF16)
    qa = _dot_nt(wuqt_ref[...], rq)
    cqt = cqt_ref[...]
    sqt = sqt_ref[...]
    pe = MLA_NOPE
    for h in range(MLA_HEADS):
        hs = slice(h * LANES, (h + 1) * LANES)
        blk = qa[hs, :]
        q_sw = jnp.concatenate([blk[:pe], blk[pe + hr:pe + 2 * hr], blk[pe:pe + hr], blk[pe + 2 * hr:]], axis=0)
        qt_ref[hs, :] = (blk * cqt + q_sw * sqt).astype(BF16)
    rkv = _rms(ckv, kvn_ref[...]).astype(BF16)
    kk = _dot(rkv, wuk_ref[...])
    for h in range(MLA_HEADS):
        hs = slice(h * LANES, (h + 1) * LANES)
        k_ref[:, hs] = (kk[:, hs] + kpe).astype(BF16)
    vt_ref[:MLA_HW, :] = (_dot_nt(wuvt_ref[...], rkv) + ones_col).astype(BF16)

    lane = lax.broadcasted_iota(jnp.int32, (1, LANES), 1)
    lf = (jnp.minimum(fl, 0.0) - jnp.log1p(jnp.exp(-jnp.abs(fl)))) * LOG2E
    lf = jnp.where(lane < FOX_HEADS, lf, 0.0)
    row = lax.broadcasted_iota(jnp.int32, (tm, tm), 0)
    col = lax.broadcasted_iota(jnp.int32, (tm, tm), 1)
    tri = jnp.where(col <= row, 1.0, 0.0).astype(BF16)
    sums = _dot(tri, _pack3(lf))
    c_loc = sums + pltpu.roll(sums, LANES - FOX_HEADS, 1) + pltpu.roll(sums, LANES - 2 * FOX_HEADS, 1)
    c_loc = jnp.where(lane < FOX_HEADS, c_loc, 0.0)

    @pl.when(pl.program_id(1) == 0)
    def _():
        carry_ref[...] = jnp.zeros_like(carry_ref)

    c = c_loc + carry_ref[0:1, :]
    carry_ref[...] = jnp.broadcast_to(c[tm - 1:tm, :], carry_ref.shape)
    src = lax.broadcasted_iota(jnp.int32, (LANES, MLA_HW), 0)
    dst = lax.broadcasted_iota(jnp.int32, (LANES, MLA_HW), 1)
    src_h = src % FOX_HEADS
    c_lane = src_h * LANES + jnp.where(src_h % 2 == 0, FOX_DIM, 0) + src // FOX_HEADS
    place = jnp.where((dst == c_lane) & (src < 3 * FOX_HEADS), 1.0, 0.0).astype(BF16)
    k_extra = _dot(_pack3(c), place)

    fqt = _dot_nt(wfqt_ref[...], xb) * (FOX_DIM ** -0.5 * LOG2E)
    fk = _dot(xb, wfk_ref[...])
    fvt = _dot_nt(wfvt_ref[...], xb)
    half_row = lax.broadcasted_iota(jnp.int32, (FOX_DIM, tm), 0)
    neg_blk = jnp.where(half_row < 3, -1.0, 0.0).astype(BF16)
    ones_blk = jnp.where(half_row < 1, 1.0, 0.0).astype(BF16)
    lane_k = lax.broadcasted_iota(jnp.int32, (1, LANES), 1)
    for h in range(FOX_HEADS):
        off = (h % 2) * FOX_DIM
        base = MLA_HW + h * LANES
        data = slice(base + off, base + off + FOX_DIM)
        rest = slice(base + FOX_DIM - off, base + 2 * FOX_DIM - off)
        rows = slice(h * FOX_DIM, (h + 1) * FOX_DIM)
        qt_ref[data, :] = fqt[rows, :].astype(BF16)
        qt_ref[rest, :] = neg_blk
        vt_ref[data, :] = fvt[rows, :].astype(BF16)
        vt_ref[rest, :] = ones_blk
        pair = slice((h // 2) * LANES, (h // 2 + 1) * LANES)
        in_data = (lane_k >= off) & (lane_k < off + FOX_DIM)
        k_ref[:, base:base + LANES] = jnp.where(
            in_data, fk[:, pair], k_extra[:, h * LANES:(h + 1) * LANES]).astype(BF16)

    sg_ref[...] = _silu(_dot(xb, wg_ref[...])).astype(BF16)


def _input_specs(x, prev):
    b, s, d = x.shape
    tok = pl.BlockSpec((None, PROJ_TM, d), lambda bi, si: (bi, si, 0))
    if prev is None:
        return [x], [tok], [], []
    o, w_out, g, bias, layer = prev
    specs = [tok, tok] + [_layer_spec(a, layer) for a in (w_out, g, bias)]
    return [o, x, w_out, g, bias], specs, [tok], [jax.ShapeDtypeStruct((b, s, d), F32)]


def _even_proj(x, w, tabs, layer, prev=None):
    b, s, d = x.shape
    tm = PROJ_TM
    tok = lambda bi, si: (bi, si, 0)
    tok_t = lambda bi, si: (bi, 0, si)
    names = ("wlat", "wuqt", "wuk", "wuvt", "wfqt", "wfk", "wfvt", "wg", "qn", "kvn", "bf")
    operands, in_specs, x_spec, x_shape = _input_specs(x, prev)
    in_specs += [_layer_spec(w[k], layer) for k in names]
    in_specs += [pl.BlockSpec((LANES, tm), lambda bi, si: (0, si))] * 2
    in_specs += [pl.BlockSpec((tm, LANES), lambda bi, si: (si, 0))] * 2
    out_specs = [pl.BlockSpec((None, EVEN_HW, tm), tok_t), pl.BlockSpec((None, tm, EVEN_HW), tok),
                 pl.BlockSpec((None, EVEN_HW, tm), tok_t), pl.BlockSpec((None, tm, d), tok)]
    out_shape = [jax.ShapeDtypeStruct((b, EVEN_HW, s), BF16), jax.ShapeDtypeStruct((b, s, EVEN_HW), BF16),
                 jax.ShapeDtypeStruct((b, EVEN_HW, s), BF16), jax.ShapeDtypeStruct((b, s, d), BF16)]
    return pl.pallas_call(
        functools.partial(_even_proj_kernel, fused=prev is not None),
        grid=(b, s // tm),
        in_specs=in_specs,
        out_specs=out_specs + x_spec,
        out_shape=out_shape + x_shape,
        scratch_shapes=[pltpu.VMEM((8, LANES), F32)],
        compiler_params=_params(2),
        name="even_proj",
    )(*operands, *[w[k] for k in names], tabs["cqt"], tabs["sqt"], tabs["ck"], tabs["sk"])


def _attn_write_out(tile, accs, sg_ref, o_ref):
    t = ATT_T
    rows = pl.ds(pl.multiple_of(tile * t, t), t)
    row = lax.broadcasted_iota(jnp.int32, (LANES, t), 0)
    for hp in range(len(accs) // 2):
        a0 = accs[2 * hp]
        a1 = accs[2 * hp + 1]
        pair_t = jnp.where(row < MLA_V, a0 / a0[MLA_V:MLA_V + 1, :], a1 / a1[0:1, :])
        cols = slice(hp * LANES, (hp + 1) * LANES)
        o_ref[rows, cols] = (pair_t.T * sg_ref[rows, cols].astype(F32)).astype(BF16)


def _attn_query_tile(qi, prev, qt_ref, k_ref, vt_ref, sg_ref, o_ref, sa_ref, sb_ref, sc_ref, mc_ref):
    t = ATT_T
    hf = t // 2
    n_heads = qt_ref.shape[0] // LANES
    nq = qt_ref.shape[1] // t
    heads = [slice(h * LANES, (h + 1) * LANES) for h in range(n_heads)]

    def q_slice(tile, start=0, size=None):
        size = t if size is None else size
        return pl.ds(pl.multiple_of(tile * t + start, size), size)

    def key_slice(kj, start=0, size=None):
        size = t if size is None else size
        return pl.ds(pl.multiple_of(kj * t + start, size), size)

    def col_max(s):
        return jnp.max(s, axis=0, keepdims=True)

    def causal_mask(s):
        n = s.shape[0]
        keep = (lax.broadcasted_iota(jnp.int32, (n, n), 0) <= lax.broadcasted_iota(jnp.int32, (n, n), 1))
        return jnp.where(keep, s, -jnp.inf)

    def scores_head(q_tile, kj, buf, h):
        s = _dot(k_ref[key_slice(kj), heads[h]], qt_ref[heads[h], q_slice(q_tile)])
        buf[h] = s
        return col_max(s)

    def diag_scores_head(buf, h):
        early = _dot(k_ref[key_slice(qi, 0, hf), heads[h]], qt_ref[heads[h], q_slice(qi)])
        late = _dot(k_ref[key_slice(qi, hf, hf), heads[h]], qt_ref[heads[h], q_slice(qi, hf, hf)])
        early = jnp.concatenate([causal_mask(early[:, :hf]), early[:, hf:]], axis=1)
        late = causal_mask(late)
        buf[h, :hf, :] = early
        buf[h, hf:, hf:] = late
        return jnp.concatenate([col_max(early[:, :hf]),
                                jnp.maximum(col_max(early[:, hf:]), col_max(late))], axis=1)

    def process_head(kj, s, h, tile_max, m, acc):
        m_new = jnp.maximum(m, tile_max)
        p = jnp.exp2(s - m_new).astype(BF16)
        return m_new, jnp.exp2(m - m_new) * acc + _dot(vt_ref[heads[h], key_slice(kj)], p)

    def diag_process_head(buf, h, tile_max, m, acc):
        m_new = jnp.maximum(m, tile_max)
        p_early = jnp.exp2(buf[h, :hf, :] - m_new).astype(BF16)
        p_late = jnp.exp2(buf[h, hf:, hf:] - m_new[:, hf:]).astype(BF16)
        pv_early = _dot(vt_ref[heads[h], key_slice(qi, 0, hf)], p_early)
        pv_late = _dot(vt_ref[heads[h], key_slice(qi, hf, hf)], p_late)
        scaled = jnp.exp2(m - m_new) * acc + pv_early
        return m_new, jnp.concatenate([scaled[:, :hf], scaled[:, hf:] + pv_late], axis=1)

    def stage(kj, cur, nxt, tile_max, state, diag_next=False):
        next_max, out = [], []
        for h in range(n_heads):
            next_max.append(diag_scores_head(nxt, h) if diag_next else scores_head(qi, kj + 1, nxt, h))
            out.append(process_head(kj, cur[h], h, tile_max[h], *state[h]))
        return tuple(next_max), tuple(out)

    def two_stages(i, carry):
        tile_max, state = carry
        tile_max, state = stage(2 * i + 1, sa_ref, sb_ref, tile_max, state)
        return stage(2 * i + 2, sb_ref, sa_ref, tile_max, state)

    def last_tile(buf, tile_max, state):
        out = []
        for h in range(n_heads):
            mc_ref[h, 0:1, :] = scores_head(jnp.minimum(qi + 1, nq - 1), 0, sc_ref, h)
            out.append(diag_process_head(buf, h, tile_max[h], *state[h]))
        return tuple(out)

    def first_stage(state, diag_next=False):
        tile_max = tuple(mc_ref[h, 0:1, :] for h in range(n_heads))
        return stage(0, sc_ref, sa_ref, tile_max, state, diag_next)

    def fresh():
        return tuple((jnp.full((1, t), -jnp.inf, F32), jnp.zeros((LANES, t), F32)) for _ in range(n_heads))

    def accs_of(state):
        return tuple(acc for _, acc in state)

    def case_first(prev):
        tile_max = tuple(diag_scores_head(sa_ref, h) for h in range(n_heads))
        return accs_of(last_tile(sa_ref, tile_max, fresh()))

    def case_second(prev):
        _attn_write_out(qi - 1, prev, sg_ref, o_ref)
        tile_max, state = first_stage(fresh(), diag_next=True)
        return accs_of(last_tile(sa_ref, tile_max, state))

    def case_even(prev):
        _attn_write_out(qi - 1, prev, sg_ref, o_ref)
        tile_max, state = first_stage(fresh())
        tile_max, state = lax.fori_loop(0, (qi - 2) // 2, two_stages, (tile_max, state))
        tile_max, state = stage(qi - 1, sa_ref, sb_ref, tile_max, state, diag_next=True)
        return accs_of(last_tile(sb_ref, tile_max, state))

    def case_odd(prev):
        _attn_write_out(qi - 1, prev, sg_ref, o_ref)
        tile_max, state = first_stage(fresh())
        tile_max, state = lax.fori_loop(0, (qi - 3) // 2, two_stages, (tile_max, state))
        tile_max, state = stage(qi - 2, sa_ref, sb_ref, tile_max, state)
        tile_max, state = stage(qi - 1, sb_ref, sa_ref, tile_max, state, diag_next=True)
        return accs_of(last_tile(sa_ref, tile_max, state))

    def case_later(prev):
        return lax.cond(qi % 2 == 0, case_even, case_odd, prev)

    def case_not_first(prev):
        return lax.cond(qi == 1, case_second, case_later, prev)

    return lax.cond(qi == 0, case_first, case_not_first, prev)


def _causal_attn_kernel(qt_ref, k_ref, vt_ref, sg_ref, o_ref, sa_ref, sb_ref, sc_ref, mc_ref):
    n_heads = qt_ref.shape[0] // LANES
    nq = qt_ref.shape[1] // ATT_T

    def body(qi, prev):
        return _attn_query_tile(qi, prev, qt_ref, k_ref, vt_ref, sg_ref, o_ref, sa_ref, sb_ref, sc_ref, mc_ref)

    none_yet = tuple(jnp.zeros((LANES, ATT_T), F32) for _ in range(n_heads))
    last = lax.fori_loop(0, nq, body, none_yet)
    _attn_write_out(nq - 1, last, sg_ref, o_ref)


def _causal_attn(qt, k, vt, sg):
    b, s, hw = k.shape
    t = ATT_T
    gw = ATT_HG * LANES
    ow = ATT_HG * MLA_V
    score_buf = pltpu.VMEM((ATT_HG, t, t), F32)
    return pl.pallas_call(
        _causal_attn_kernel,
        grid=(b, hw // gw),
        in_specs=[pl.BlockSpec((None, gw, s), lambda bi, g: (bi, g, 0)),
                  pl.BlockSpec((None, s, gw), lambda bi, g: (bi, 0, g)),
                  pl.BlockSpec((None, gw, s), lambda bi, g: (bi, g, 0)),
                  pl.BlockSpec((None, s, ow), lambda bi, g: (bi, 0, g))],
        out_specs=pl.BlockSpec((None, s, ow), lambda bi, g: (bi, 0, g)),
        out_shape=jax.ShapeDtypeStruct((b, s, (hw // LANES) * MLA_V), BF16),
        scratch_shapes=[score_buf, score_buf, score_buf, pltpu.VMEM((ATT_HG, 8, t), F32)],
        compiler_params=_params(2),
        name="causal_attn",
    )(qt, k, vt, sg)


def _out_kernel(o_ref, x_ref, w_ref, g_ref, b_ref, y_ref):
    y_ref[...] = _residual_norm(o_ref, x_ref, w_ref, g_ref, b_ref)


def _out_proj(o, x, w_out, g, bias, layer):
    b, s, d = x.shape
    tm = OUT_TM
    tok = lambda bi, si: (bi, si, 0)
    return pl.pallas_call(
        _out_kernel,
        grid=(b, s // tm),
        in_specs=[pl.BlockSpec((None, tm, d), tok), pl.BlockSpec((None, tm, d), tok),
                  _layer_spec(w_out, layer), _layer_spec(g, layer), _layer_spec(bias, layer)],
        out_specs=pl.BlockSpec((None, tm, d), tok),
        out_shape=jax.ShapeDtypeStruct((b, s, d), F32),
        compiler_params=_params(2),
        name="out_proj_ln",
    )(o, x, w_out, g, bias)


def _odd_proj_kernel(*refs, fused):
    x, refs = _layer_input(refs, fused)
    (wqt_ref, wkv_ref, wg_ref, c_ref, s_ref, ct_ref, st_ref,
     qt_ref, k_ref, vt_ref, sg_ref) = refs[:11]
    if fused:
        refs[11][...] = x
    d = SWA_HEADS * SWA_DIM
    half = SWA_DIM // 2
    xb = x.astype(BF16)

    kv = _dot(xb, wkv_ref[...])
    vt_ref[...] = kv[:, LANES:].T.astype(BF16)
    kk = kv[:, :LANES]
    lane = lax.broadcasted_iota(jnp.int32, (1, LANES), 1)
    first_half = (lane % SWA_DIM) < half
    k_sw = jnp.where(first_half, pltpu.roll(kk, LANES - half, 1), pltpu.roll(kk, half, 1))
    k_ref[...] = (kk * c_ref[...] + k_sw * s_ref[...]).astype(BF16)

    qq = _dot_nt(wqt_ref[...], xb)
    ctt = ct_ref[...]
    stt = st_ref[...]
    scale = SWA_DIM ** -0.5 * LOG2E
    for p in range(d // LANES):
        blk = qq[p * LANES:(p + 1) * LANES, :]
        q_sw = jnp.concatenate([blk[half:2 * half], blk[:half], blk[3 * half:], blk[2 * half:3 * half]], axis=0)
        qt_ref[p * LANES:(p + 1) * LANES, :] = ((blk * ctt + q_sw * stt) * scale).astype(BF16)

    sg_ref[...] = _silu(_dot(xb, wg_ref[...])).astype(BF16)


def _layer_spec(a, layer):
    return pl.BlockSpec((None,) + a.shape[1:], lambda bi, si: (layer,) + (0,) * (a.ndim - 1),
                        pipeline_mode=pl.Buffered(1))


def _odd_proj(x, w, tabs, layer, prev=None):
    b, s, d = x.shape
    tm = PROJ_TM
    tok = lambda bi, si: (bi, si, 0)
    tok_t = lambda bi, si: (bi, 0, si)
    tab = pl.BlockSpec((tm, LANES), lambda bi, si: (si, 0))
    tab_t = pl.BlockSpec((LANES, tm), lambda bi, si: (0, si))
    names = ("wqt", "wkv", "wg")
    operands, in_specs, x_spec, x_shape = _input_specs(x, prev)
    return pl.pallas_call(
        functools.partial(_odd_proj_kernel, fused=prev is not None),
        grid=(b, s // tm),
        in_specs=in_specs + [_layer_spec(w[k], layer) for k in names] + [tab, tab, tab_t, tab_t],
        out_specs=[pl.BlockSpec((None, d, tm), tok_t), pl.BlockSpec((None, tm, LANES), tok),
                   pl.BlockSpec((None, LANES, tm), tok_t), pl.BlockSpec((None, tm, d), tok)] + x_spec,
        out_shape=[jax.ShapeDtypeStruct((b, d, s), BF16), jax.ShapeDtypeStruct((b, s, LANES), BF16),
                   jax.ShapeDtypeStruct((b, LANES, s), BF16), jax.ShapeDtypeStruct((b, s, d), BF16)] + x_shape,
        compiler_params=_params(2),
        name="odd_proj",
    )(*operands, *[w[k] for k in names], tabs["c"], tabs["s"], tabs["ct"], tabs["st"])


def _swa_kernel(sink_ref, qt_ref, k_ref, vt_ref, sg_ref, o_ref):
    n_blk = qt_ref.shape[1] // QBLK
    n_pair = qt_ref.shape[0] // LANES
    win = 2 * QBLK
    qi = pl.program_id(1)
    row = lax.broadcasted_iota(jnp.int32, (LANES, QBLK), 0)
    lo_rows = row < SWA_DIM
    rel = (lax.broadcasted_iota(jnp.int32, (win, QBLK), 1) - lax.broadcasted_iota(jnp.int32, (win, QBLK), 0))
    sink_row = jnp.concatenate([jnp.full((1, QBLK), sink_ref[c] * LOG2E, F32) for c in range(2 * n_pair)],
                               axis=1)
    ones_blk = jnp.ones((16, win), BF16)

    def window_start(blk):
        return pl.multiple_of(jnp.maximum(qi * n_blk + blk - 1, 0) * QBLK, QBLK)

    def scores(blk):
        q0 = blk * QBLK
        cols = []
        for p in range(n_pair):
            qp = qt_ref[p * LANES:(p + 1) * LANES, q0:q0 + QBLK]
            zero = jnp.zeros_like(qp)
            cols.append(jnp.where(lo_rows, qp, zero))
            cols.append(jnp.where(lo_rows, zero, qp))
        s = _dot(k_ref[pl.ds(window_start(blk), win), :], jnp.concatenate(cols, axis=1))
        shift = jnp.where(qi * n_blk + blk == 0, 0, QBLK)
        diff = rel + shift
        band = (diff >= 0) & (diff < WINDOW)
        return jnp.concatenate([jnp.where(band, s[:, c * QBLK:(c + 1) * QBLK], -jnp.inf)
                                for c in range(2 * n_pair)], axis=1)

    def process(blk, s):
        q0 = blk * QBLK
        m = jnp.maximum(jnp.max(s, axis=0, keepdims=True), sink_row)
        p = jnp.exp2(s - m).astype(BF16)
        vt = jnp.concatenate([vt_ref[:, pl.ds(window_start(blk), win)], ones_blk], axis=0)
        ot = _dot(vt, p)
        inv = 1.0 / (ot[LANES:LANES + 1, :] + jnp.exp2(sink_row - m))
        for pr in range(n_pair):
            ca = slice((2 * pr) * QBLK, (2 * pr + 1) * QBLK)
            cb = slice((2 * pr + 1) * QBLK, (2 * pr + 2) * QBLK)
            pair = jnp.concatenate([ot[:SWA_DIM, ca] * inv[:, ca], ot[SWA_DIM:LANES, cb] * inv[:, cb]],
                                   axis=0).T
            cols = slice(pr * LANES, (pr + 1) * LANES)
            o_ref[q0:q0 + QBLK, cols] = (pair * sg_ref[q0:q0 + QBLK, cols].astype(F32)).astype(BF16)

    s_next = scores(0)
    for blk in range(n_blk):
        s_cur = s_next
        if blk + 1 < n_blk:
            s_next = scores(blk + 1)
        process(blk, s_cur)


def _swa_attn(sinks, qt, k, vt, sg):
    b, d, s = qt.shape
    tq = SWA_TQ
    tok = lambda bi, qi: (bi, qi, 0)
    return pl.pallas_call(
        _swa_kernel,
        grid=(b, s // tq),
        in_specs=[pl.BlockSpec(memory_space=pltpu.SMEM),
                  pl.BlockSpec((None, d, tq), lambda bi, qi: (bi, 0, qi)),
                  pl.BlockSpec((None, s, LANES), lambda bi, qi: (bi, 0, 0)),
                  pl.BlockSpec((None, LANES, s), lambda bi, qi: (bi, 0, 0)),
                  pl.BlockSpec((None, tq, d), tok)],
        out_specs=pl.BlockSpec((None, tq, d), tok),
        out_shape=jax.ShapeDtypeStruct((b, s, d), BF16),
        compiler_params=_params(2),
        name="swa_attn",
    )(sinks, qt, k, vt, sg)


def _pad_heads(w, n_heads, dim, offsets):
    k = w.shape[0]
    n_off = len(offsets)
    w4 = w.reshape(k, n_heads // n_off, n_off, dim)
    parts = [jnp.pad(w4[:, :, i, :], ((0, 0), (0, 0), (off, LANES - off - dim)))
             for i, off in enumerate(offsets)]
    return jnp.stack(parts, axis=2).reshape(k, n_heads * LANES)


def _even_weights(w_in, q_norm, w_uq, kv_norm, w_ukv, b_f):
    sizes = (MLA_Q_RANK, MLA_KV_RANK, MLA_ROPE, FOX_HEADS * FOX_DIM, FOX_HEADS * FOX_DIM,
             FOX_HEADS * FOX_DIM, FOX_HEADS, MLA_HEADS * MLA_V + FOX_HEADS * FOX_DIM)
    cuts = [int(c) for c in np.cumsum(sizes)[:-1]]
    w_cq, w_ckv, w_kpe, w_fq, w_fk, w_fv, w_f, w_g = jnp.split(w_in, cuts, axis=1)
    place = lambda w: _pad_heads(w, 1, MLA_ROPE, (MLA_NOPE,))
    wlat = jnp.concatenate([w_cq, w_ckv, place(w_kpe), _pad_heads(w_f, 1, FOX_HEADS, (0,))], axis=1)
    v_off = (0, MLA_V)
    ukv3 = w_ukv.reshape(MLA_KV_RANK, MLA_HEADS, MLA_NOPE + MLA_V)
    uk = ukv3[..., :MLA_NOPE].reshape(MLA_KV_RANK, MLA_HEADS * MLA_NOPE)
    uv = ukv3[..., MLA_NOPE:].reshape(MLA_KV_RANK, MLA_HEADS * MLA_V)
    bf = lambda a: a.astype(BF16)
    return dict(wlat=bf(wlat),
                wuqt=bf(_pad_heads(w_uq, MLA_HEADS, MLA_NOPE + MLA_ROPE, (0,)).T),
                wuk=bf(_pad_heads(uk, MLA_HEADS, MLA_NOPE, (0,))),
                wuvt=bf(_pad_heads(uv, MLA_HEADS, MLA_V, v_off).T),
                wfqt=bf(w_fq.T), wfk=bf(w_fk), wfvt=bf(w_fv.T),
                wg=bf(w_g), qn=q_norm.reshape(1, -1), kvn=kv_norm.reshape(1, -1),
                bf=jnp.pad(b_f, (0, LANES - FOX_HEADS)).reshape(1, LANES))


def _pair_heads(w):
    lead = w.shape[:-1]
    w4 = w.reshape(*lead, SWA_KV_HEADS, SWA_HEADS // SWA_KV_HEADS, -1)
    return jnp.swapaxes(w4, -3, -2).reshape(*lead, -1)


def _odd_weights(w_in, sinks, w_out):
    d = SWA_HEADS * SWA_DIM
    dkv = SWA_KV_HEADS * SWA_DIM
    w_q, w_k, w_v, w_g = jnp.split(w_in, [d, d + dkv, d + 2 * dkv], axis=1)
    pair_rows = lambda a: jnp.swapaxes(
        a.reshape(SWA_KV_HEADS, SWA_HEADS // SWA_KV_HEADS, SWA_DIM, -1), 0, 1).reshape(a.shape)
    return dict(wqt=pair_rows(w_q.T).astype(BF16), wkv=jnp.concatenate([w_k, w_v], axis=1).astype(BF16),
                wg=_pair_heads(w_g).astype(BF16),
                sinks=_pair_heads(sinks.reshape(-1, 1).T).reshape(-1),
                w_out=pair_rows(w_out).astype(BF16))


def _rope_tables(s):
    pos = np.arange(s, dtype=np.float64)

    def cs(dim):
        inv = ROPE_THETA ** (-np.arange(0, dim, 2, dtype=np.float64) / dim)
        ang = pos[:, None] * inv[None, :]
        cos, sin = np.cos(ang), np.sin(ang)
        return np.concatenate([cos, cos], axis=1), np.concatenate([-sin, sin], axis=1)

    c32, s32 = cs(MLA_ROPE)
    z = lambda n: np.zeros((s, n))
    pad = LANES - MLA_NOPE - MLA_ROPE
    scale = (MLA_NOPE + MLA_ROPE) ** -0.5 * LOG2E
    even = dict(cqt=(np.concatenate([np.ones((s, MLA_NOPE)), c32, z(pad)], axis=1) * scale).T,
                sqt=(np.concatenate([z(MLA_NOPE), s32, z(pad)], axis=1) * scale).T,
                ck=np.concatenate([z(MLA_NOPE), c32, z(pad)], axis=1),
                sk=np.concatenate([z(MLA_NOPE), s32, z(pad)], axis=1))
    c64, s64 = cs(SWA_DIM)
    odd = dict(c=np.concatenate([c64, c64], axis=1), s=np.concatenate([s64, s64], axis=1))
    odd.update(ct=odd["c"].T, st=odd["s"].T)
    as_f32 = lambda d: {k: jnp.asarray(np.ascontiguousarray(v), dtype=F32) for k, v in d.items()}
    return as_f32(even), as_f32(odd)


def kernel(x, even_w_in, even_q_norm, even_w_uq, even_kv_norm, even_w_ukv, even_b_f, even_w_out,
           even_ln_g, even_ln_b, odd_w_in, odd_sinks, odd_w_out, odd_ln_g, odd_ln_b):
    s = x.shape[1]
    even_tabs, odd_tabs = _rope_tables(s)
    bf = lambda a: a.astype(BF16)
    we = jax.vmap(_even_weights)(bf(even_w_in), even_q_norm, bf(even_w_uq), even_kv_norm, bf(even_w_ukv),
                                 even_b_f)
    wo = jax.vmap(_odd_weights)(bf(odd_w_in), odd_sinks, bf(odd_w_out))
    even_w_out = even_w_out.astype(BF16)
    row = lambda a: a[:, None, :]
    closing = {0: (even_w_out, row(even_ln_g), row(even_ln_b)),
               1: (wo["w_out"], row(odd_ln_g), row(odd_ln_b))}
    prev = None
    for layer in range(DEPTH):
        j = layer // 2
        if layer % 2 == 0:
            qt, k, vt, sg, *x_new = _even_proj(x, we, even_tabs, j, prev)
            x = x_new[0] if x_new else x
            o = _causal_attn(qt, k, vt, sg)
        else:
            qt, k, vt, sg, *x_new = _odd_proj(x, wo, odd_tabs, j, prev)
            x = x_new[0] if x_new else x
            o = _swa_attn(wo["sinks"][j], qt, k, vt, sg)
        prev = (o, *closing[layer % 2], j)
    o, w_out, g, bias, j = prev
    return _out_proj(o, x, w_out, g, bias, j)
```
